```python
import jax, jax.numpy as jnp
from jax import lax
import numpy as np

D_MODEL = 1024
BATCH = 4
SEQ = 8192
DEPTH = 1

GLA_HEADS = 4
GLA_DK = D_MODEL // 16
GLA_DV = D_MODEL // 8
GLA_WIDTH = GLA_HEADS * GLA_DV
GLA_GATE_RANK = 16
GLA_GATE_TAU = 16.0
GLA_CHUNK = 64
SWA_Q_HEADS = 8
SWA_KV_HEADS = 2
SWA_HEAD_DIM = D_MODEL // 16
SWA_WIDTH = SWA_Q_HEADS * SWA_HEAD_DIM
WINDOW = 128
SWA_BLOCK = 128
ROPE_THETA = 10000.0
RMS_EPS = 1e-6
D_MIX = GLA_WIDTH + SWA_WIDTH
COL_SIZES = (
    GLA_HEADS * GLA_DK,
    GLA_HEADS * GLA_DK,
    GLA_WIDTH,
    GLA_GATE_RANK,
    GLA_WIDTH,
    SWA_WIDTH,
    SWA_KV_HEADS * SWA_HEAD_DIM,
    SWA_KV_HEADS * SWA_HEAD_DIM,
    SWA_WIDTH,
)
D_IN = (4 * GLA_HEADS * GLA_DK // 2) + 2 * GLA_WIDTH + GLA_GATE_RANK + 2 * SWA_WIDTH + 2 * SWA_KV_HEADS * SWA_HEAD_DIM

kernel_name = "hymba_gla_swa_sink_adaln"


def rmsnorm(x, g):
    xf = x.astype(jnp.float32)
    y = xf * lax.rsqrt(jnp.mean(xf * xf, axis=-1, keepdims=True) + RMS_EPS)
    return (y * g.astype(jnp.float32)).astype(x.dtype)


def split_cols(t, sizes):
    outs = []
    start = 0
    for s in sizes:
        outs.append(t[..., start:start + s])
        start += s
    return outs


def rope(t, positions):
    hd = t.shape[-1]
    inv_freq = 1.0 / (ROPE_THETA ** (jnp.arange(0, hd, 2, dtype=jnp.float32) / hd))
    ang = positions.astype(jnp.float32)[..., None] * inv_freq
    cos = jnp.cos(ang)[:, :, None, :]
    sin = jnp.sin(ang)[:, :, None, :]
    tf = t.astype(jnp.float32)
    t1, t2 = tf[..., : hd // 2], tf[..., hd // 2:]
    return jnp.concatenate([t1 * cos - t2 * sin, t2 * cos + t1 * sin], axis=-1).astype(t.dtype)


def gla_chunked(q, k, v, log_a):
    B, S, H, dk = q.shape
    dv = v.shape[-1]
    C = GLA_CHUNK
    N = S // C

    def chunks(t):
        return t.reshape(B, N, C, H, t.shape[-1]).transpose(0, 3, 1, 2, 4).astype(jnp.float32)

    qc = chunks(q) * (dk ** -0.5)
    kc = chunks(k)
    vc = chunks(v)
    b = jnp.cumsum(chunks(log_a), axis=3)
    b_last = b[:, :, :, -1:, :]
    q_d = qc * jnp.exp(b)
    k_d = kc * jnp.exp(-b)
    k_tail = kc * jnp.exp(b_last - b)
    causal = jnp.tril(jnp.ones((C, C), dtype=bool))
    scores = jnp.einsum('bhnid,bhnjd->bhnij', q_d, k_d)
    scores = jnp.where(causal, scores, 0.0)
    o_intra = jnp.einsum('bhnij,bhnjv->bhniv', scores, vc)
    u = jnp.einsum('bhncd,bhncv->bhndv', k_tail, vc)
    decay = jnp.exp(b_last[:, :, :, 0, :])

    def step(state, inp):
        dec, un = inp
        return dec[..., None] * state + un, state

    state0 = jnp.zeros((B, H, dk, dv), jnp.float32)
    _, s_prev = lax.scan(step, state0, (decay.transpose(2, 0, 1, 3), u.transpose(2, 0, 1, 3, 4)))
    s_prev = s_prev.transpose(1, 2, 0, 3, 4)
    o = o_intra + jnp.einsum('bhncd,bhndv->bhncv', q_d, s_prev)
    return o.transpose(0, 2, 3, 1, 4).reshape(B, S, H, dv)


def sliding_window_sink_attention(q, k, v, sinks):
    B, S, Hq, hd = q.shape
    Hkv = k.shape[2]
    G = Hq // Hkv
    L = SWA_BLOCK
    N = S // L
    qb = q.reshape(B, N, L, Hkv, G, hd)

    def band(t):
        tb = t.reshape(B, N, L, Hkv, hd)
        prev = jnp.concatenate([jnp.zeros_like(tb[:, :1]), tb[:, :-1]], axis=1)
        return jnp.concatenate([prev, tb], axis=2)

    kb = band(k)
    vb = band(v)
    scores = jnp.einsum('bnqhgd,bnkhd->bhgnqk', qb, kb).astype(jnp.float32) * (hd ** -0.5)
    qi = jnp.arange(L)[None, :, None]
    kj = jnp.arange(2 * L)[None, None, :]
    blk = jnp.arange(N)[:, None, None]
    dist = qi + L - kj
    valid = (dist >= 0) & (dist < WINDOW) & (blk * L + kj - L >= 0)
    scores = jnp.where(valid, scores, -jnp.inf)
    sink = jnp.broadcast_to(sinks.astype(jnp.float32).reshape(Hkv, G, 1, 1, 1), scores.shape[:-1] + (1,))
    probs = jax.nn.softmax(jnp.concatenate([scores, sink], axis=-1), axis=-1)[..., :-1]
    o = jnp.einsum('bhgnqk,bnkhd->bnqhgd', probs.astype(v.dtype), vb)
    return o.reshape(B, S, Hq * hd)


def setup_inputs(seed: int = 0) -> dict:
    key = jax.random.key(seed)
    ks = jax.random.split(key, 16)
    f32 = jnp.float32
    x = jax.random.normal(ks[0], (BATCH, SEQ, D_MODEL), f32)
    c = jax.random.normal(ks[1], (BATCH, D_MODEL), f32)
    offsets = jax.random.randint(ks[2], (BATCH, 1), 0, 4096, dtype=jnp.int32)
    positions = jnp.arange(SEQ, dtype=jnp.int32)[None, :] + offsets
    w_ada = jax.random.normal(ks[3], (DEPTH, D_MODEL, 3 * D_MODEL), f32) * (0.5 * D_MODEL ** -0.5)
    gate_bias = jnp.concatenate([jnp.zeros((2 * D_MODEL,), f32), jnp.ones((D_MODEL,), f32)])
    b_ada = gate_bias[None, :] + 0.02 * jax.random.normal(ks[4], (DEPTH, 3 * D_MODEL), f32)
    g_norm = 1.0 + 0.02 * jax.random.normal(ks[5], (DEPTH, D_MODEL), f32)
    w_in = jax.random.normal(ks[6], (DEPTH, D_MODEL, D_IN), f32) * (D_MODEL ** -0.5)
    w_decay = jax.random.normal(ks[7], (DEPTH, GLA_GATE_RANK, GLA_HEADS * GLA_DK), f32) * (GLA_GATE_RANK ** -0.5)
    b_decay = 0.1 * jax.random.normal(ks[8], (DEPTH, GLA_HEADS * GLA_DK), f32)
    g_gla_head = 1.0 + 0.02 * jax.random.normal(ks[9], (DEPTH, GLA_WIDTH), f32)
    sinks = 0.5 * jax.random.normal(ks[10], (DEPTH, SWA_Q_HEADS), f32)
    w_out = jax.random.normal(ks[11], (DEPTH, D_MIX, D_MODEL), f32) * (D_MIX ** -0.5)
    g_final = 1.0 + 0.02 * jax.random.normal(ks[12], (D_MODEL,), f32)
    return {"x": x, "c": c, "positions": positions, "w_ada": w_ada, "b_ada": b_ada,
            "g_norm": g_norm, "w_in": w_in, "w_decay": w_decay, "b_decay": b_decay,
            "g_gla_head": g_gla_head, "sinks": sinks, "w_out": w_out, "g_final": g_final}


def reference(x, c, positions, w_ada, b_ada, g_norm, w_in, w_decay, b_decay, g_gla_head, sinks, w_out, g_final):
    B, S, _ = x.shape
    for l in range(DEPTH):
        mod = jnp.dot(jax.nn.silu(c.astype(jnp.float32)), w_ada[l].astype(jnp.float32)) + b_ada[l].astype(jnp.float32)
        shift, scale, gate = jnp.split(mod, 3, axis=-1)
        h = (rmsnorm(x, g_norm[l]).astype(jnp.float32) * (1.0 + scale[:, None, :]) + shift[:, None, :]).astype(x.dtype)
        proj = jnp.einsum('bsd,de->bse', h, w_in[l])
        gq, gk, gv, ga, gz, sq, sk, sv, sz = split_cols(proj, COL_SIZES)
        z = jnp.einsum('bsr,rk->bsk', ga, w_decay[l]) + b_decay[l]
        log_a = jax.nn.log_sigmoid(z.astype(jnp.float32)) / GLA_GATE_TAU
        o_gla = gla_chunked(gq.reshape(B, S, GLA_HEADS, GLA_DK), gk.reshape(B, S, GLA_HEADS, GLA_DK),
                            gv.reshape(B, S, GLA_HEADS, GLA_DV), log_a.reshape(B, S, GLA_HEADS, GLA_DK))
        o_gla = rmsnorm(o_gla, g_gla_head[l].reshape(GLA_HEADS, GLA_DV)).reshape(B, S, GLA_WIDTH)
        o_gla = (o_gla * jax.nn.silu(gz.astype(jnp.float32))).astype(x.dtype)
        q = rope(sq.reshape(B, S, SWA_Q_HEADS, SWA_HEAD_DIM), positions)
        k = rope(sk.reshape(B, S, SWA_KV_HEADS, SWA_HEAD_DIM), positions)
        v = sv.reshape(B, S, SWA_KV_HEADS, SWA_HEAD_DIM)
        o_swa = sliding_window_sink_attention(q, k, v, sinks[l])
        o_swa = (o_swa.astype(jnp.float32) * jax.nn.silu(sz.astype(jnp.float32))).astype(x.dtype)
        y = jnp.einsum('bse,ed->bsd', jnp.concatenate([o_gla, o_swa], axis=-1), w_out[l])
        x = (x.astype(jnp.float32) + gate[:, None, :] * y.astype(jnp.float32)).astype(x.dtype)
    return rmsnorm(x, g_final)
```

```python
import functools

import jax
import jax.numpy as jnp
import numpy as np
from jax import lax
from jax.experimental import pallas as pl
from jax.experimental.pallas import tpu as pltpu

D_MODEL = 1024
GLA_HEADS = 4
GLA_DK = 64
GLA_DV = 128
GLA_WIDTH = GLA_HEADS * GLA_DV
GLA_QK = GLA_HEADS * GLA_DK
GLA_GATE_RANK = 16
GLA_GATE_TAU = 16.0
GLA_CHUNK = 64
SWA_Q_HEADS = 8
SWA_KV_HEADS = 2
SWA_GROUP = SWA_Q_HEADS // SWA_KV_HEADS
SWA_HEAD_DIM = 64
SWA_WIDTH = SWA_Q_HEADS * SWA_HEAD_DIM
SWA_KV_WIDTH = SWA_KV_HEADS * SWA_HEAD_DIM
WINDOW = 128
SWA_BLOCK = 128
ROPE_THETA = 10000.0
RMS_EPS = 1e-6

LANES = 128
SEQ_TILE = 256
NEG_BIG = -1e30
VMEM_LIMIT_BYTES = 56 * 1024 * 1024

BF16 = jnp.bfloat16
F32 = jnp.float32


def _dot(a, b):
    return jnp.dot(a, b, preferred_element_type=F32)


def _dot_nt(a, b):
    return lax.dot_general(a, b, (((1,), (1,)), ((), ())), preferred_element_type=F32)


def _dot_tn(a, b):
    return lax.dot_general(a, b, (((0,), (0,)), ((), ())), preferred_element_type=F32)


def _silu(x):
    return x * (1.0 / (1.0 + jnp.exp(-x)))


def _log_sigmoid(z):
    return -(jnp.maximum(-z, 0.0) + jnp.log1p(jnp.exp(-jnp.abs(z))))


def _adaln_body(c_ref, w_ref, b_ref, o_ref):
    a = _silu(c_ref[...]).astype(BF16)
    o_ref[...] = _dot(a, w_ref[...].astype(BF16)) + b_ref[...]


def _adaln(c_pad, w_ada, b_ada):
    rows = c_pad.shape[0]
    n_out = w_ada.shape[1]
    col_block = D_MODEL
    return pl.pallas_call(
        _adaln_body,
        grid=(n_out // col_block,),
        in_specs=[
            pl.BlockSpec((rows, D_MODEL), lambda j: (0, 0)),
            pl.BlockSpec((D_MODEL, col_block), lambda j: (0, j)),
            pl.BlockSpec((1, col_block), lambda j: (0, j)),
        ],
        out_specs=pl.BlockSpec((rows, col_block), lambda j: (0, j)),
        out_shape=jax.ShapeDtypeStruct((rows, n_out), F32),
        name="adaln_mod",
    )(c_pad, w_ada, b_ada)


def _layer_body(sinks_ref, x_ref, pos_ref, mod_ref, gnorm_ref, invf_ref, tri_ref, ones_ref,
                w_gqk_ref, w_gv_ref, w_ga_ref, w_gz_ref, w_sq_ref, w_skv_ref, w_sz_ref,
                w_dec_ref, b_dec_ref, g_gla_ref, w_out_ref, g_fin_ref,
                o_ref, state_ref, k_scr, v_scr, cat_scr):
    ts = SEQ_TILE
    t = pl.program_id(1)

    @pl.when(t == 0)
    def _():
        state_ref[...] = jnp.zeros_like(state_ref)
        k_scr[0:SWA_BLOCK, :] = jnp.zeros((SWA_BLOCK, SWA_KV_WIDTH), BF16)
        v_scr[0:SWA_BLOCK, :] = jnp.zeros((SWA_BLOCK, SWA_KV_WIDTH), BF16)

    x = x_ref[0]
    shift = mod_ref[0, 0:1, :]
    scale = mod_ref[0, 1:2, :]
    gate = mod_ref[0, 2:3, :]

    ms = jnp.mean(x * x, axis=-1, keepdims=True)
    h = (x * lax.rsqrt(ms + RMS_EPS)) * gnorm_ref[...] * (1.0 + scale) + shift
    hb = h.astype(BF16)

    gqk = _dot(hb, w_gqk_ref[...])
    gv = _dot(hb, w_gv_ref[...])
    ga = _dot(hb, w_ga_ref[...])
    gz = _dot(hb, w_gz_ref[...])
    sq = _dot(hb, w_sq_ref[...])
    skv = _dot(hb, w_skv_ref[...])
    sz = _dot(hb, w_sz_ref[...])

    z = _dot(ga.astype(BF16), w_dec_ref[...]) + b_dec_ref[...]
    log_a = _log_sigmoid(z) * (1.0 / GLA_GATE_TAU)
    la_hi = log_a.astype(BF16)
    la_lo = (log_a - la_hi.astype(F32)).astype(BF16)
    la2 = jnp.concatenate([la_hi, la_lo], axis=1)
    b2 = _dot(tri_ref[...], la2)
    bl2 = _dot(ones_ref[...], la2)
    b = b2[:, :GLA_QK] + b2[:, GLA_QK:]
    b_last = bl2[:, :GLA_QK] + bl2[:, GLA_QK:]
    gq = gqk[:, :GLA_QK]
    gk = gqk[:, GLA_QK:]
    q_d = (gq * (GLA_DK ** -0.5) * jnp.exp(b)).astype(BF16)
    k_d = (gk * jnp.exp(-b)).astype(BF16)
    k_tail = (gk * jnp.exp(b_last - b)).astype(BF16)
    decay = jnp.exp(b_last)
    gvb = gv.astype(BF16)

    ci = lax.broadcasted_iota(jnp.int32, (GLA_CHUNK, GLA_CHUNK), 0)
    cj = lax.broadcasted_iota(jnp.int32, (GLA_CHUNK, GLA_CHUNK), 1)
    causal = cj <= ci

    n_chunks = ts // GLA_CHUNK
    g_gla = g_gla_ref[...]
    for hd in range(GLA_HEADS):
        ks = slice(hd * GLA_DK, (hd + 1) * GLA_DK)
        vs = slice(hd * GLA_DV, (hd + 1) * GLA_DV)
        st = state_ref[hd]
        outs = []
        for c in range(n_chunks):
            rs = slice(c * GLA_CHUNK, (c + 1) * GLA_CHUNK)
            qd = q_d[rs, ks]
            kd = k_d[rs, ks]
            kt = k_tail[rs, ks]
            vv = gvb[rs, vs]
            s = jnp.where(causal, _dot_nt(qd, kd), 0.0)
            o = _dot(s.astype(BF16), vv) + _dot_nt(qd, st.astype(BF16))
            outs.append(o)
            u_t = _dot_tn(vv, kt)
            st = st * decay[c * GLA_CHUNK:c * GLA_CHUNK + 1, ks] + u_t
        state_ref[hd] = st
        o_h = jnp.concatenate(outs, axis=0)
        o_ms = jnp.mean(o_h * o_h, axis=-1, keepdims=True)
        o_n = o_h * lax.rsqrt(o_ms + RMS_EPS) * g_gla[:, vs]
        cat_scr[:, vs] = (o_n * _silu(gz[:, vs])).astype(BF16)

    pos = pos_ref[0].astype(F32)
    ang_t = invf_ref[...] * pos
    cos_f = jnp.transpose(jnp.cos(ang_t))
    sin_f = jnp.transpose(jnp.sin(ang_t))
    lane = lax.broadcasted_iota(jnp.int32, (1, LANES), 1)
    first_half = (lane % SWA_HEAD_DIM) < (SWA_HEAD_DIM // 2)
    sin_s = jnp.where(first_half, -sin_f, sin_f)

    def rope(tv):
        rot = jnp.where(first_half,
                        pltpu.roll(tv, LANES - SWA_HEAD_DIM // 2, axis=1),
                        pltpu.roll(tv, SWA_HEAD_DIM // 2, axis=1))
        return tv * cos_f + rot * sin_s

    q_parts = [rope(sq[:, j * LANES:(j + 1) * LANES]) for j in range(SWA_WIDTH // LANES)]
    qb = (jnp.concatenate(q_parts, axis=1) * (SWA_HEAD_DIM ** -0.5)).astype(BF16)
    k_scr[SWA_BLOCK:SWA_BLOCK + ts, :] = rope(skv[:, :SWA_KV_WIDTH]).astype(BF16)
    v_scr[SWA_BLOCK:SWA_BLOCK + ts, :] = skv[:, SWA_KV_WIDTH:].astype(BF16)

    qi = lax.broadcasted_iota(jnp.int32, (SWA_BLOCK, 2 * SWA_BLOCK), 0)
    kj = lax.broadcasted_iota(jnp.int32, (SWA_BLOCK, 2 * SWA_BLOCK), 1)
    dist = qi + SWA_BLOCK - kj
    valid = (dist >= 0) & (dist < WINDOW)
    n_blocks = ts // SWA_BLOCK
    for n in range(n_blocks):
        if n == 0:
            ok = valid & ((kj >= SWA_BLOCK) | (t > 0))
        else:
            ok = valid
        bias = jnp.where(ok, 0.0, NEG_BIG)
        rs = slice(n * SWA_BLOCK, (n + 1) * SWA_BLOCK)
        bs = slice(n * SWA_BLOCK, (n + 2) * SWA_BLOCK)
        kb = k_scr[bs, :]
        vb = v_scr[bs, :]
        o_parts = []
        for hq in range(SWA_Q_HEADS):
            kv = hq // SWA_GROUP
            q_h = qb[rs, hq * SWA_HEAD_DIM:(hq + 1) * SWA_HEAD_DIM]
            k_h = kb[:, kv * SWA_HEAD_DIM:(kv + 1) * SWA_HEAD_DIM]
            v_h = vb[:, kv * SWA_HEAD_DIM:(kv + 1) * SWA_HEAD_DIM]
            s = _dot_nt(q_h, k_h) + bias
            sink = sinks_ref[hq]
            m = jnp.maximum(jnp.max(s, axis=-1, keepdims=True), sink)
            e = jnp.exp(s - m)
            den = jnp.sum(e, axis=-1, keepdims=True) + jnp.exp(sink - m)
            p = (e * (1.0 / den)).astype(BF16)
            o_parts.append(_dot(p, v_h))
        o_swa = jnp.concatenate(o_parts, axis=1)
        cat_scr[rs, GLA_WIDTH:] = (o_swa * _silu(sz[rs, :])).astype(BF16)

    k_scr[0:SWA_BLOCK, :] = k_scr[ts:ts + SWA_BLOCK, :]
    v_scr[0:SWA_BLOCK, :] = v_scr[ts:ts + SWA_BLOCK, :]

    y = _dot(cat_scr[...], w_out_ref[...])
    xo = x + gate * y
    ms_o = jnp.mean(xo * xo, axis=-1, keepdims=True)
    o_ref[0] = xo * lax.rsqrt(ms_o + RMS_EPS) * g_fin_ref[...]


def _chunk_matrices(ts):
    r = np.arange(ts)
    same = (r[:, None] // GLA_CHUNK) == (r[None, :] // GLA_CHUNK)
    tri = same & (r[None, :] <= r[:, None])
    return jnp.asarray(tri, BF16), jnp.asarray(same, BF16)


def _rope_inv_freq_column():
    half = SWA_HEAD_DIM // 2
    inv = 1.0 / (ROPE_THETA ** (jnp.arange(0, SWA_HEAD_DIM, 2, dtype=F32) / SWA_HEAD_DIM))
    return jnp.tile(inv, LANES // half).reshape(LANES, 1)


def kernel(x, c, positions, w_ada, b_ada, g_norm, w_in, w_decay, b_decay, g_gla_head, sinks, w_out, g_final):
    B, S, D = x.shape
    ts = SEQ_TILE
    assert D == D_MODEL and S % ts == 0 and ts % SWA_BLOCK == 0
    assert w_ada.shape[0] == 1, "one layer"

    rows = 8
    c_pad = jnp.zeros((rows, D), F32).at[:B].set(c.astype(F32))
    mod = _adaln(c_pad, w_ada[0], b_ada[0][None, :])[:B].reshape(B, 3, D)

    wi = w_in[0].astype(BF16)
    o0 = 0
    cols = {}
    for name, n in (("gq", GLA_QK), ("gk", GLA_QK), ("gv", GLA_WIDTH), ("ga", GLA_GATE_RANK),
                    ("gz", GLA_WIDTH), ("sq", SWA_WIDTH), ("sk", SWA_KV_WIDTH), ("sv", SWA_KV_WIDTH),
                    ("sz", SWA_WIDTH)):
        cols[name] = wi[:, o0:o0 + n]
        o0 += n
    w_gqk = jnp.concatenate([cols["gq"], cols["gk"]], axis=1)
    w_skv = jnp.concatenate([cols["sk"], cols["sv"]], axis=1)
    w_ga = jnp.pad(cols["ga"], ((0, 0), (0, LANES - GLA_GATE_RANK)))
    w_dec = jnp.pad(w_decay[0].astype(BF16), ((0, LANES - GLA_GATE_RANK), (0, 0)))
    tri, ones_blk = _chunk_matrices(ts)

    const2 = lambda b, t, s: (0, 0)
    full = lambda a: pl.BlockSpec(a.shape, const2)
    operands = [
        x,
        positions.reshape(B, 1, S),
        mod,
        g_norm[0][None, :],
        _rope_inv_freq_column(),
        tri, ones_blk,
        w_gqk, cols["gv"], w_ga, cols["gz"], cols["sq"], w_skv, cols["sz"],
        w_dec, b_decay[0][None, :], g_gla_head[0][None, :],
        w_out[0].astype(BF16), g_final[None, :],
    ]
    in_specs = [
        pl.BlockSpec((1, ts, D), lambda b, t, s: (b, t, 0)),
        pl.BlockSpec((1, 1, ts), lambda b, t, s: (b, 0, t)),
        pl.BlockSpec((1, 3, D), lambda b, t, s: (b, 0, 0)),
    ] + [full(a) for a in operands[3:]]

    grid_spec = pltpu.PrefetchScalarGridSpec(
        num_scalar_prefetch=1,
        grid=(B, S // ts),
        in_specs=in_specs,
        out_specs=pl.BlockSpec((1, ts, D), lambda b, t, s: (b, t, 0)),
        scratch_shapes=[
            pltpu.VMEM((GLA_HEADS, GLA_DV, GLA_DK), F32),
            pltpu.VMEM((ts + SWA_BLOCK, SWA_KV_WIDTH), BF16),
            pltpu.VMEM((ts + SWA_BLOCK, SWA_KV_WIDTH), BF16),
            pltpu.VMEM((ts, GLA_WIDTH + SWA_WIDTH), BF16),
        ],
    )
    return pl.pallas_call(
        _layer_body,
        grid_spec=grid_spec,
        out_shape=jax.ShapeDtypeStruct((B, S, D), x.dtype),
        compiler_params=pltpu.CompilerParams(
            dimension_semantics=("arbitrary", "arbitrary"),
            vmem_limit_bytes=VMEM_LIMIT_BYTES,
        ),
        name="hymba_layer",
    )(sinks[0].astype(F32), *operands)
```

```python
import functools

import jax
import jax.numpy as jnp
import numpy as np
from jax import lax
from jax.experimental import pallas as pl
from jax.experimental.pallas import tpu as pltpu

D_MODEL = 1024
GLA_HEADS = 4
GLA_DK = 64
GLA_DV = 128
GLA_WIDTH = GLA_HEADS * GLA_DV
GLA_QK = GLA_HEADS * GLA_DK
GLA_GATE_RANK = 16
GLA_GATE_TAU = 16.0
GLA_CHUNK = 64
SWA_Q_HEADS = 8
SWA_KV_HEADS = 2
SWA_GROUP = SWA_Q_HEADS // SWA_KV_HEADS
SWA_HEAD_DIM = 64
SWA_WIDTH = SWA_Q_HEADS * SWA_HEAD_DIM
SWA_KV_WIDTH = SWA_KV_HEADS * SWA_HEAD_DIM
WINDOW = 128
SWA_BLOCK = 128
ROPE_THETA = 10000.0
RMS_EPS = 1e-6

LANES = 128
SEQ_TILE = 512
LOG2E = 1.4426950408889634
NEG_BIG = -1e30
VMEM_LIMIT_BYTES = 56 * 1024 * 1024

BF16 = jnp.bfloat16
F32 = jnp.float32


def _dot(a, b):
    return jnp.dot(a, b, preferred_element_type=F32)


def _dot_nt(a, b):
    return lax.dot_general(a, b, (((1,), (1,)), ((), ())), preferred_element_type=F32)


def _dot_tn(a, b):
    return lax.dot_general(a, b, (((0,), (0,)), ((), ())), preferred_element_type=F32)


def _silu(x):
    return x * (1.0 / (1.0 + jnp.exp(-x)))


def _log_sigmoid(z):
    return -(jnp.maximum(-z, 0.0) + jnp.log1p(jnp.exp(-jnp.abs(z))))


def _adaln_body(c_ref, w_ref, b_ref, o_ref):
    a = _silu(c_ref[...]).astype(BF16)
    o_ref[...] = _dot(a, w_ref[...].astype(BF16)) + b_ref[...]


def _adaln(c_pad, w_ada, b_ada):
    rows = c_pad.shape[0]
    n_out = w_ada.shape[1]
    col_block = D_MODEL
    return pl.pallas_call(
        _adaln_body,
        grid=(n_out // col_block,),
        in_specs=[
            pl.BlockSpec((rows, D_MODEL), lambda j: (0, 0)),
            pl.BlockSpec((D_MODEL, col_block), lambda j: (0, j)),
            pl.BlockSpec((1, col_block), lambda j: (0, j)),
        ],
        out_specs=pl.BlockSpec((rows, col_block), lambda j: (0, j)),
        out_shape=jax.ShapeDtypeStruct((rows, n_out), F32),
        name="adaln_mod",
    )(c_pad, w_ada, b_ada)


def _layer_body(sinks_ref, x_ref, pos_ref, mod_ref, gnorm_ref, invf_ref, tri_ref, ones_ref,
                w_gqk_ref, w_gv_ref, w_ga_ref, w_gz_ref, w_sq_ref, w_skv_ref, w_sz_ref,
                w_dec_ref, b_dec_ref, g_gla_ref, w_out_ref, g_fin_ref,
                o_ref, state_ref, k_scr, v_scr, cat_scr):
    ts = SEQ_TILE
    t = pl.program_id(1)

    @pl.when(t == 0)
    def _():
        state_ref[...] = jnp.zeros_like(state_ref)
        k_scr[:, 0:SWA_BLOCK, :] = jnp.zeros((SWA_KV_HEADS, SWA_BLOCK, SWA_HEAD_DIM), BF16)
        v_scr[:, 0:SWA_BLOCK, :] = jnp.zeros((SWA_KV_HEADS, SWA_BLOCK, SWA_HEAD_DIM), BF16)

    x = x_ref[0]
    shift = mod_ref[0, 0:1, :]
    scale = mod_ref[0, 1:2, :]
    gate = mod_ref[0, 2:3, :]

    ms = jnp.mean(x * x, axis=-1, keepdims=True)
    h = (x * lax.rsqrt(ms + RMS_EPS)) * (gnorm_ref[...] * (1.0 + scale)) + shift
    hb = h.astype(BF16)

    gqk = _dot(hb, w_gqk_ref[...])
    gv = _dot(hb, w_gv_ref[...])
    ga = _dot(hb, w_ga_ref[...])
    gz = _dot(hb, w_gz_ref[...])
    sq = _dot(hb, w_sq_ref[...])
    skv = _dot(hb, w_skv_ref[...])
    sz = _dot(hb, w_sz_ref[...])

    z = _dot(ga.astype(BF16), w_dec_ref[...]) + b_dec_ref[...]
    log_a = _log_sigmoid(z) * (1.0 / GLA_GATE_TAU)
    la_hi = log_a.astype(BF16)
    la_lo = (log_a - la_hi.astype(F32)).astype(BF16)
    la2 = jnp.concatenate([la_hi, la_lo], axis=1)
    b2 = _dot(tri_ref[...], la2)
    bl2 = _dot(ones_ref[...], la2)
    b = b2[:, :GLA_QK] + b2[:, GLA_QK:]
    b_last = bl2[:, :GLA_QK] + bl2[:, GLA_QK:]
    gq = gqk[:, :GLA_QK]
    gk = gqk[:, GLA_QK:]
    q_d = (gq * (GLA_DK ** -0.5) * jnp.exp(b)).astype(BF16)
    k_d = (gk * jnp.exp(-b)).astype(BF16)
    k_tail = (gk * jnp.exp(b_last - b)).astype(BF16)
    decay = jnp.exp(b_last)
    gvb = gv.astype(BF16)

    ci = lax.broadcasted_iota(jnp.int32, (GLA_CHUNK, GLA_CHUNK), 0)
    cj = lax.broadcasted_iota(jnp.int32, (GLA_CHUNK, GLA_CHUNK), 1)
    causal = cj <= ci

    n_chunks = ts // GLA_CHUNK
    g_gla = g_gla_ref[...]
    for hd in range(GLA_HEADS):
        ks = slice(hd * GLA_DK, (hd + 1) * GLA_DK)
        vs = slice(hd * GLA_DV, (hd + 1) * GLA_DV)
        st = state_ref[hd]
        outs = []
        for c in range(n_chunks):
            rs = slice(c * GLA_CHUNK, (c + 1) * GLA_CHUNK)
            qd = q_d[rs, ks]
            kd = k_d[rs, ks]
            kt = k_tail[rs, ks]
            vv = gvb[rs, vs]
            s = jnp.where(causal, _dot_nt(qd, kd), 0.0)
            o = _dot(s.astype(BF16), vv) + _dot_nt(qd, st.astype(BF16))
            outs.append(o)
            u_t = _dot_tn(vv, kt)
            st = st * decay[c * GLA_CHUNK:c * GLA_CHUNK + 1, ks] + u_t
        state_ref[hd] = st
        o_h = jnp.concatenate(outs, axis=0)
        o_ms = jnp.mean(o_h * o_h, axis=-1, keepdims=True)
        o_n = o_h * lax.rsqrt(o_ms + RMS_EPS) * g_gla[:, vs]
        cat_scr[:, vs] = (o_n * _silu(gz[:, vs])).astype(BF16)

    pos = pos_ref[0].astype(F32)
    ang_t = invf_ref[...] * pos
    reps = LANES // (SWA_HEAD_DIM // 2)
    cos_f = jnp.transpose(jnp.concatenate([jnp.cos(ang_t)] * reps, axis=0))
    sin_f = jnp.transpose(jnp.concatenate([jnp.sin(ang_t)] * reps, axis=0))
    lane = lax.broadcasted_iota(jnp.int32, (1, LANES), 1)
    first_half = (lane % SWA_HEAD_DIM) < (SWA_HEAD_DIM // 2)
    sin_s = jnp.where(first_half, -sin_f, sin_f)

    def rope(tv):
        rot = jnp.where(first_half,
                        pltpu.roll(tv, LANES - SWA_HEAD_DIM // 2, axis=1),
                        pltpu.roll(tv, SWA_HEAD_DIM // 2, axis=1))
        return tv * cos_f + rot * sin_s

    q_parts = [rope(sq[:, j * LANES:(j + 1) * LANES]) * (SWA_HEAD_DIM ** -0.5 * LOG2E)
               for j in range(SWA_WIDTH // LANES)]
    q_rot = jnp.concatenate(q_parts, axis=1)
    k_rot = rope(skv[:, :SWA_KV_WIDTH]).astype(BF16)
    for kv in range(SWA_KV_HEADS):
        hs = slice(kv * SWA_HEAD_DIM, (kv + 1) * SWA_HEAD_DIM)
        k_scr[kv, SWA_BLOCK:SWA_BLOCK + ts, :] = k_rot[:, hs]
        v_scr[kv, SWA_BLOCK:SWA_BLOCK + ts, :] = skv[:, SWA_KV_WIDTH + kv * SWA_HEAD_DIM:
                                                     SWA_KV_WIDTH + (kv + 1) * SWA_HEAD_DIM].astype(BF16)

    gq_lanes = SWA_GROUP * SWA_BLOCK
    kj = lax.broadcasted_iota(jnp.int32, (2 * SWA_BLOCK, gq_lanes), 0)
    ql = lax.broadcasted_iota(jnp.int32, (2 * SWA_BLOCK, gq_lanes), 1)
    dist = (ql % SWA_BLOCK) + SWA_BLOCK - kj
    valid = (dist >= 0) & (dist < WINDOW)
    head_of_lane = lax.broadcasted_iota(jnp.int32, (1, gq_lanes), 1) // SWA_BLOCK
    n_blocks = ts // SWA_BLOCK
    for n in range(n_blocks):
        if n == 0:
            ok = valid & ((kj >= SWA_BLOCK) | (t > 0))
        else:
            ok = valid
        bias = jnp.where(ok, 0.0, NEG_BIG)
        rs = slice(n * SWA_BLOCK, (n + 1) * SWA_BLOCK)
        bs = slice(n * SWA_BLOCK, (n + 2) * SWA_BLOCK)
        o_groups = []
        for kv in range(SWA_KV_HEADS):
            sink = jnp.zeros((1, gq_lanes), F32)
            for j in range(SWA_GROUP):
                sink = jnp.where(head_of_lane == j, sinks_ref[kv * SWA_GROUP + j] * LOG2E, sink)
            gs = slice(kv * SWA_GROUP * SWA_HEAD_DIM, (kv + 1) * SWA_GROUP * SWA_HEAD_DIM)
            q_t = jnp.transpose(q_rot[rs, gs])
            rhs = jnp.concatenate([q_t[j * SWA_HEAD_DIM:(j + 1) * SWA_HEAD_DIM, :] for j in range(SWA_GROUP)],
                                  axis=1).astype(BF16)
            s_t = _dot(k_scr[kv, bs, :], rhs) + bias
            m = jnp.maximum(jnp.max(s_t, axis=0, keepdims=True), sink)
            e = jnp.exp2(s_t - m)
            den = jnp.sum(e, axis=0, keepdims=True) + jnp.exp2(sink - m)
            o_t = _dot_tn(v_scr[kv, bs, :], e.astype(BF16)) * (1.0 / den)
            o_stack = jnp.concatenate([o_t[:, j * SWA_BLOCK:(j + 1) * SWA_BLOCK] for j in range(SWA_GROUP)],
                                      axis=0)
            o_groups.append(jnp.transpose(o_stack))
        o_swa = jnp.concatenate(o_groups, axis=1)
        cat_scr[rs, GLA_WIDTH:] = (o_swa * _silu(sz[rs, :])).astype(BF16)

    for kv in range(SWA_KV_HEADS):
        k_scr[kv, 0:SWA_BLOCK, :] = k_scr[kv, ts:ts + SWA_BLOCK, :]
        v_scr[kv, 0:SWA_BLOCK, :] = v_scr[kv, ts:ts + SWA_BLOCK, :]

    y = _dot(cat_scr[...], w_out_ref[...])
    xo = x + gate * y
    ms_o = jnp.mean(xo * xo, axis=-1, keepdims=True)
    o_ref[0] = xo * lax.rsqrt(ms_o + RMS_EPS) * g_fin_ref[...]


def _chunk_matrices(ts):
    r = np.arange(ts)
    same = (r[:, None] // GLA_CHUNK) == (r[None, :] // GLA_CHUNK)
    tri = same & (r[None, :] <= r[:, None])
    return jnp.asarray(tri, BF16), jnp.asarray(same, BF16)


def _rope_inv_freq_column():
    inv = 1.0 / (ROPE_THETA ** (jnp.arange(0, SWA_HEAD_DIM, 2, dtype=F32) / SWA_HEAD_DIM))
    return inv.reshape(SWA_HEAD_DIM // 2, 1)


def kernel(x, c, positions, w_ada, b_ada, g_norm, w_in, w_decay, b_decay, g_gla_head, sinks, w_out, g_final):
    B, S, D = x.shape
    ts = SEQ_TILE
    assert D == D_MODEL and S % ts == 0 and ts % SWA_BLOCK == 0
    assert w_ada.shape[0] == 1, "one layer"

    rows = 8
    c_pad = jnp.zeros((rows, D), F32).at[:B].set(c.astype(F32))
    mod = _adaln(c_pad, w_ada[0], b_ada[0][None, :])[:B].reshape(B, 3, D)

    wi = w_in[0].astype(BF16)
    o0 = 0
    cols = {}
    for name, n in (("gq", GLA_QK), ("gk", GLA_QK), ("gv", GLA_WIDTH), ("ga", GLA_GATE_RANK),
                    ("gz", GLA_WIDTH), ("sq", SWA_WIDTH), ("sk", SWA_KV_WIDTH), ("sv", SWA_KV_WIDTH),
                    ("sz", SWA_WIDTH)):
        cols[name] = wi[:, o0:o0 + n]
        o0 += n
    w_gqk = jnp.concatenate([cols["gq"], cols["gk"]], axis=1)
    w_skv = jnp.concatenate([cols["sk"], cols["sv"]], axis=1)
    w_ga = jnp.pad(cols["ga"], ((0, 0), (0, LANES - GLA_GATE_RANK)))
    w_dec = jnp.pad(w_decay[0].astype(BF16), ((0, LANES - GLA_GATE_RANK), (0, 0)))
    tri, ones_blk = _chunk_matrices(ts)

    const2 = lambda b, t, s: (0, 0)
    full = lambda a: pl.BlockSpec(a.shape, const2)
    operands = [
        x,
        positions.reshape(B, 1, S),
        mod,
        g_norm[0][None, :],
        _rope_inv_freq_column(),
        tri, ones_blk,
        w_gqk, cols["gv"], w_ga, cols["gz"], cols["sq"], w_skv, cols["sz"],
        w_dec, b_decay[0][None, :], g_gla_head[0][None, :],
        w_out[0].astype(BF16), g_final[None, :],
    ]
    in_specs = [
        pl.BlockSpec((1, ts, D), lambda b, t, s: (b, t, 0)),
        pl.BlockSpec((1, 1, ts), lambda b, t, s: (b, 0, t)),
        pl.BlockSpec((1, 3, D), lambda b, t, s: (b, 0, 0)),
    ] + [full(a) for a in operands[3:]]

    grid_spec = pltpu.PrefetchScalarGridSpec(
        num_scalar_prefetch=1,
        grid=(B, S // ts),
        in_specs=in_specs,
        out_specs=pl.BlockSpec((1, ts, D), lambda b, t, s: (b, t, 0)),
        scratch_shapes=[
            pltpu.VMEM((GLA_HEADS, GLA_DV, GLA_DK), F32),
            pltpu.VMEM((SWA_KV_HEADS, ts + SWA_BLOCK, SWA_HEAD_DIM), BF16),
            pltpu.VMEM((SWA_KV_HEADS, ts + SWA_BLOCK, SWA_HEAD_DIM), BF16),
            pltpu.VMEM((ts, GLA_WIDTH + SWA_WIDTH), BF16),
        ],
    )
    return pl.pallas_call(
        _layer_body,
        grid_spec=grid_spec,
        out_shape=jax.ShapeDtypeStruct((B, S, D), x.dtype),
        compiler_params=pltpu.CompilerParams(
            dimension_semantics=("arbitrary", "arbitrary"),
            vmem_limit_bytes=VMEM_LIMIT_BYTES,
        ),
        name="hymba_layer",
    )(sinks[0].astype(F32), *operands)
```

```python
import jax
import jax.numpy as jnp
import numpy as np
from jax import lax
from jax.experimental import pallas as pl
from jax.experimental.pallas import tpu as pltpu

D_MODEL = 1024
GLA_HEADS = 4
GLA_DK = 64
GLA_DV = 128
GLA_WIDTH = GLA_HEADS * GLA_DV
GLA_QK = GLA_HEADS * GLA_DK
GLA_GATE_RANK = 16
GLA_GATE_TAU = 16.0
GLA_CHUNK = 64
SWA_Q_HEADS = 8
SWA_KV_HEADS = 2
SWA_GROUP = SWA_Q_HEADS // SWA_KV_HEADS
SWA_HEAD_DIM = 64
SWA_WIDTH = SWA_Q_HEADS * SWA_HEAD_DIM
SWA_KV_WIDTH = SWA_KV_HEADS * SWA_HEAD_DIM
WINDOW = 128
SWA_BLOCK = 128
ROPE_THETA = 10000.0
RMS_EPS = 1e-6

LANES = 128
SEQ_TILE = 256
LOG2E = 1.4426950408889634
NEG_BIG = -1e30
VMEM_LIMIT_BYTES = 56 * 1024 * 1024

P_GQK = 0
P_GV = P_GQK + 2 * GLA_QK
P_GA = P_GV + GLA_WIDTH
P_GZ = P_GA + LANES
P_SQ = P_GZ + GLA_WIDTH
P_SKV = P_SQ + SWA_WIDTH
P_SZ = P_SKV + 2 * SWA_KV_WIDTH
P_COLS = P_SZ + SWA_WIDTH
PROJ_CHUNK = 256

BF16 = jnp.bfloat16
F32 = jnp.float32


def _dot(a, b):
    return jnp.dot(a, b, preferred_element_type=F32)


def _dot_nt(a, b):
    return lax.dot_general(a, b, (((1,), (1,)), ((), ())), preferred_element_type=F32)


def _dot_tn(a, b):
    return lax.dot_general(a, b, (((0,), (0,)), ((), ())), preferred_element_type=F32)


def _silu(x):
    return x * (1.0 / (1.0 + jnp.exp(-x)))


def _log_sigmoid(z):
    return -(jnp.maximum(-z, 0.0) + jnp.log1p(jnp.exp(-jnp.abs(z))))


def _adaln_body(c_ref, w_ref, b_ref, o_ref):
    a = _silu(c_ref[...]).astype(BF16)
    o_ref[...] = _dot(a, w_ref[...].astype(BF16)) + b_ref[...]


def _adaln(c_pad, w_ada, b_ada):
    rows = c_pad.shape[0]
    n_out = w_ada.shape[1]
    col_block = D_MODEL
    return pl.pallas_call(
        _adaln_body,
        grid=(n_out // col_block,),
        in_specs=[
            pl.BlockSpec((rows, D_MODEL), lambda j: (0, 0)),
            pl.BlockSpec((D_MODEL, col_block), lambda j: (0, j)),
            pl.BlockSpec((1, col_block), lambda j: (0, j)),
        ],
        out_specs=pl.BlockSpec((rows, col_block), lambda j: (0, j)),
        out_shape=jax.ShapeDtypeStruct((rows, n_out), F32),
        name="adaln_mod",
    )(c_pad, w_ada, b_ada)


def _layer_body(sinks_ref, x_ref, xn_ref, pos_ref, mod_ref, gnorm_ref, invf_ref, tri_ref, ones_ref,
                w_in_ref, w_dec_ref, b_dec_ref, g_gla_ref, w_out_ref, g_fin_ref,
                o_ref, state_ref, k_scr, v_scr, p0_scr, p1_scr, cat0_scr, cat1_scr):
    ts = SEQ_TILE
    i = pl.program_id(1)

    shift = mod_ref[0, 0:1, :]
    scale = mod_ref[0, 1:2, :]
    gate = mod_ref[0, 2:3, :]
    norm_gain = gnorm_ref[...] * (1.0 + scale)

    def project(x, p_scr):
        ms = jnp.mean(x * x, axis=-1, keepdims=True)
        hb = ((x * lax.rsqrt(ms + RMS_EPS)) * norm_gain + shift).astype(BF16)
        for c0 in range(0, P_COLS, PROJ_CHUNK):
            c1 = min(c0 + PROJ_CHUNK, P_COLS)
            p_scr[:, c0:c1] = _dot(hb, w_in_ref[:, c0:c1])
            yield

    def gla(p_scr, cat_scr):
        z = _dot(p_scr[:, P_GA:P_GA + LANES].astype(BF16), w_dec_ref[...]) + b_dec_ref[...]
        yield
        log_a = _log_sigmoid(z) * (1.0 / GLA_GATE_TAU)
        la_hi = log_a.astype(BF16)
        la_lo = (log_a - la_hi.astype(F32)).astype(BF16)
        la2 = jnp.concatenate([la_hi, la_lo], axis=1)
        b2 = _dot(tri_ref[...], la2)
        bl2 = _dot(ones_ref[...], la2)
        yield
        b = b2[:, :GLA_QK] + b2[:, GLA_QK:]
        b_last = bl2[:, :GLA_QK] + bl2[:, GLA_QK:]
        gq = p_scr[:, P_GQK:P_GQK + GLA_QK]
        gk = p_scr[:, P_GQK + GLA_QK:P_GQK + 2 * GLA_QK]
        q_d = gq * (GLA_DK ** -0.5) * jnp.exp(b)
        k_d = (gk * jnp.exp(-b)).astype(BF16)
        k_tail = gk * jnp.exp(b_last - b)
        decay_t = jnp.transpose(jnp.exp(b_last))

        n_chunks = ts // GLA_CHUNK
        row_i = lax.broadcasted_iota(jnp.int32, (ts, ts), 0)
        col_i = lax.broadcasted_iota(jnp.int32, (ts, ts), 1)
        same_chunk = (row_i // GLA_CHUNK) == (col_i // GLA_CHUNK)
        causal = same_chunk & (col_i <= row_i)
        pair_lane = lax.broadcasted_iota(jnp.int32, (1, LANES), 1)
        reps = ts // LANES

        heads = []
        for hd in range(GLA_HEADS):
            pr = slice((hd // 2) * LANES, (hd // 2 + 1) * LANES)
            in_head = (pair_lane // GLA_DK) == (hd % 2)
            q_sel = jnp.where(in_head, q_d[:, pr], 0.0)
            k_sel = jnp.where(in_head, k_tail[:, pr], 0.0)
            q_dup = q_sel + pltpu.roll(q_sel, GLA_DK, axis=1)
            k_dup = k_sel + pltpu.roll(k_sel, GLA_DK, axis=1)
            q_blk = jnp.where(same_chunk, jnp.concatenate([q_dup] * reps, axis=1), 0.0).astype(BF16)
            k_blk = jnp.where(same_chunk, jnp.concatenate([k_dup] * reps, axis=1), 0.0).astype(BF16)
            v_h = p_scr[:, P_GV + hd * GLA_DV:P_GV + (hd + 1) * GLA_DV].astype(BF16)
            s_raw = _dot_nt(q_sel.astype(BF16), k_d[:, pr])
            u_all = _dot_tn(k_blk, v_h)
            heads.append((q_blk, v_h, s_raw, u_all))
        yield

        staged = []
        for hd, (q_blk, v_h, s_raw, u_all) in enumerate(heads):
            s = jnp.where(causal, s_raw, 0.0).astype(BF16)
            st = state_ref[hd]
            entering = []
            for c in range(n_chunks):
                entering.append(st)
                last = c * GLA_CHUNK + GLA_CHUNK - 1
                d_col = decay_t[hd * GLA_DK:(hd + 1) * GLA_DK, last:last + 1]
                st = st * d_col + u_all[c * GLA_DK:(c + 1) * GLA_DK, :]
            state_ref[hd] = st
            s_stack = jnp.concatenate(entering, axis=0).astype(BF16)
            staged.append((s, v_h, q_blk, s_stack))
        yield

        outs = [_dot(s, v_h) + _dot(q_blk, s_stack) for (s, v_h, q_blk, s_stack) in staged]
        yield

        g_gla = g_gla_ref[...]
        for hd, o_h in enumerate(outs):
            vs = slice(hd * GLA_DV, (hd + 1) * GLA_DV)
            o_ms = jnp.mean(o_h * o_h, axis=-1, keepdims=True)
            o_n = o_h * lax.rsqrt(o_ms + RMS_EPS) * g_gla[:, vs]
            cat_scr[:, vs] = (o_n * _silu(p_scr[:, P_GZ + hd * GLA_DV:P_GZ + (hd + 1) * GLA_DV])).astype(BF16)

    def swa(p_scr, cat_scr, rows, first_tile):
        pos = pos_ref[0, :, rows].astype(F32)
        ang_t = invf_ref[...] * pos
        reps = LANES // (SWA_HEAD_DIM // 2)
        cos_f = jnp.transpose(jnp.concatenate([jnp.cos(ang_t)] * reps, axis=0))
        sin_f = jnp.transpose(jnp.concatenate([jnp.sin(ang_t)] * reps, axis=0))
        lane = lax.broadcasted_iota(jnp.int32, (1, LANES), 1)
        first_half = (lane % SWA_HEAD_DIM) < (SWA_HEAD_DIM // 2)
        sin_s = jnp.where(first_half, -sin_f, sin_f)
        yield

        def rope(tv):
            rot = jnp.where(first_half,
                            pltpu.roll(tv, LANES - SWA_HEAD_DIM // 2, axis=1),
                            pltpu.roll(tv, SWA_HEAD_DIM // 2, axis=1))
            return tv * cos_f + rot * sin_s

        q_parts = [rope(p_scr[:, P_SQ + j * LANES:P_SQ + (j + 1) * LANES]) * (SWA_HEAD_DIM ** -0.5 * LOG2E)
                   for j in range(SWA_WIDTH // LANES)]
        q_rot = jnp.concatenate(q_parts, axis=1)
        k_rot = rope(p_scr[:, P_SKV:P_SKV + SWA_KV_WIDTH]).astype(BF16)
        for kv in range(SWA_KV_HEADS):
            hs = slice(kv * SWA_HEAD_DIM, (kv + 1) * SWA_HEAD_DIM)
            k_scr[kv, SWA_BLOCK:SWA_BLOCK + ts, :] = k_rot[:, hs]
            v0 = P_SKV + SWA_KV_WIDTH + kv * SWA_HEAD_DIM
            v_scr[kv, SWA_BLOCK:SWA_BLOCK + ts, :] = p_scr[:, v0:v0 + SWA_HEAD_DIM].astype(BF16)
        yield

        gq_lanes = SWA_GROUP * SWA_BLOCK
        kj = lax.broadcasted_iota(jnp.int32, (2 * SWA_BLOCK, gq_lanes), 0)
        ql = lax.broadcasted_iota(jnp.int32, (2 * SWA_BLOCK, gq_lanes), 1)
        dist = (ql % SWA_BLOCK) + SWA_BLOCK - kj
        valid = (dist >= 0) & (dist < WINDOW)
        head_of_lane = lax.broadcasted_iota(jnp.int32, (1, gq_lanes), 1) // SWA_BLOCK
        n_blocks = ts // SWA_BLOCK
        units = [(n, kv) for n in range(n_blocks) for kv in range(SWA_KV_HEADS)]

        scores = []
        for n, kv in units:
            rs = slice(n * SWA_BLOCK, (n + 1) * SWA_BLOCK)
            bs = slice(n * SWA_BLOCK, (n + 2) * SWA_BLOCK)
            gs = slice(kv * SWA_GROUP * SWA_HEAD_DIM, (kv + 1) * SWA_GROUP * SWA_HEAD_DIM)
            q_t = jnp.transpose(q_rot[rs, gs])
            rhs = jnp.concatenate([q_t[j * SWA_HEAD_DIM:(j + 1) * SWA_HEAD_DIM, :] for j in range(SWA_GROUP)],
                                  axis=1).astype(BF16)
            scores.append(_dot(k_scr[kv, bs, :], rhs))
        yield

        probs = []
        for (n, kv), s_raw in zip(units, scores):
            if n == 0 and first_tile is not None:
                ok = valid & ((kj >= SWA_BLOCK) | jnp.logical_not(first_tile))
            else:
                ok = valid
            s_t = s_raw + jnp.where(ok, 0.0, NEG_BIG)
            sink = jnp.zeros((1, gq_lanes), F32)
            for j in range(SWA_GROUP):
                sink = jnp.where(head_of_lane == j, sinks_ref[kv * SWA_GROUP + j] * LOG2E, sink)
            m = jnp.maximum(jnp.max(s_t, axis=0, keepdims=True), sink)
            e = jnp.exp2(s_t - m)
            den = jnp.sum(e, axis=0, keepdims=True) + jnp.exp2(sink - m)
            probs.append((e.astype(BF16), 1.0 / den))
            yield

        outs = []
        for (n, kv), (e, inv_den) in zip(units, probs):
            bs = slice(n * SWA_BLOCK, (n + 2) * SWA_BLOCK)
            outs.append(_dot_tn(v_scr[kv, bs, :], e) * inv_den)
        yield

        for n in range(n_blocks):
            rs = slice(n * SWA_BLOCK, (n + 1) * SWA_BLOCK)
            o_groups = []
            for kv in range(SWA_KV_HEADS):
                o_t = outs[n * SWA_KV_HEADS + kv]
                o_stack = jnp.concatenate([o_t[:, j * SWA_BLOCK:(j + 1) * SWA_BLOCK] for j in range(SWA_GROUP)],
                                          axis=0)
                o_groups.append(jnp.transpose(o_stack))
            o_swa = jnp.concatenate(o_groups, axis=1)
            cat_scr[rs, GLA_WIDTH:] = (o_swa * _silu(p_scr[rs, P_SZ:P_SZ + SWA_WIDTH])).astype(BF16)

        for kv in range(SWA_KV_HEADS):
            k_scr[kv, 0:SWA_BLOCK, :] = k_scr[kv, ts:ts + SWA_BLOCK, :]
            v_scr[kv, 0:SWA_BLOCK, :] = v_scr[kv, ts:ts + SWA_BLOCK, :]

    def finish(cat_scr, rows):
        x = x_ref[0, rows, :]
        y = _dot(cat_scr[...], w_out_ref[...])
        xo = x + gate * y
        ms_o = jnp.mean(xo * xo, axis=-1, keepdims=True)
        o_ref[0, rows, :] = xo * lax.rsqrt(ms_o + RMS_EPS) * g_fin_ref[...]

    def interleave(primary, filler):
        live = list(primary)
        while live:
            for g in list(live):
                if next(g, StopIteration) is StopIteration:
                    live.remove(g)
                next(filler, None)
        for _ in filler:
            pass

    rows0 = slice(0, ts)
    rows1 = slice(ts, 2 * ts)

    @pl.when(i == 0)
    def _():
        state_ref[...] = jnp.zeros_like(state_ref)
        k_scr[:, 0:SWA_BLOCK, :] = jnp.zeros((SWA_KV_HEADS, SWA_BLOCK, SWA_HEAD_DIM), BF16)
        v_scr[:, 0:SWA_BLOCK, :] = jnp.zeros((SWA_KV_HEADS, SWA_BLOCK, SWA_HEAD_DIM), BF16)
        for _ in project(x_ref[0, rows0, :], p0_scr):
            pass

    interleave([gla(p0_scr, cat0_scr), swa(p0_scr, cat0_scr, rows0, i == 0)], project(x_ref[0, rows1, :], p1_scr))
    finish(cat0_scr, rows0)
    interleave([gla(p1_scr, cat1_scr), swa(p1_scr, cat1_scr, rows1, None)], project(xn_ref[0], p0_scr))
    finish(cat1_scr, rows1)


def _chunk_matrices(ts):
    r = np.arange(ts)
    same = (r[:, None] // GLA_CHUNK) == (r[None, :] // GLA_CHUNK)
    tri = same & (r[None, :] <= r[:, None])
    return jnp.asarray(tri, BF16), jnp.asarray(same, BF16)


def _rope_inv_freq_column():
    inv = 1.0 / (ROPE_THETA ** (jnp.arange(0, SWA_HEAD_DIM, 2, dtype=F32) / SWA_HEAD_DIM))
    return inv.reshape(SWA_HEAD_DIM // 2, 1)


def kernel(x, c, positions, w_ada, b_ada, g_norm, w_in, w_decay, b_decay, g_gla_head, sinks, w_out, g_final):
    B, S, D = x.shape
    ts = SEQ_TILE
    assert D == D_MODEL and S % (2 * ts) == 0 and ts % SWA_BLOCK == 0
    assert w_ada.shape[0] == 1, "one layer"
    n_tiles = S // ts

    rows = 8
    c_pad = jnp.zeros((rows, D), F32).at[:B].set(c.astype(F32))
    mod = _adaln(c_pad, w_ada[0], b_ada[0][None, :])[:B].reshape(B, 3, D)

    wi = w_in[0].astype(BF16)
    o0 = 0
    cols = {}
    for name, n in (("gq", GLA_QK), ("gk", GLA_QK), ("gv", GLA_WIDTH), ("ga", GLA_GATE_RANK),
                    ("gz", GLA_WIDTH), ("sq", SWA_WIDTH), ("sk", SWA_KV_WIDTH), ("sv", SWA_KV_WIDTH),
                    ("sz", SWA_WIDTH)):
        cols[name] = wi[:, o0:o0 + n]
        o0 += n
    w_ga = jnp.pad(cols["ga"], ((0, 0), (0, LANES - GLA_GATE_RANK)))
    w_in_b = jnp.concatenate([cols["gq"], cols["gk"], cols["gv"], w_ga, cols["gz"], cols["sq"],
                              cols["sk"], cols["sv"], cols["sz"]], axis=1)
    assert w_in_b.shape == (D, P_COLS)
    w_dec = jnp.pad(w_decay[0].astype(BF16), ((0, LANES - GLA_GATE_RANK), (0, 0)))
    tri, ones_blk = _chunk_matrices(ts)

    const2 = lambda b, i, s: (0, 0)
    full = lambda a: pl.BlockSpec(a.shape, const2)
    operands = [
        x,
        x,
        positions.reshape(B, 1, S),
        mod,
        g_norm[0][None, :],
        _rope_inv_freq_column(),
        tri, ones_blk,
        w_in_b,
        w_dec, b_decay[0][None, :], g_gla_head[0][None, :],
        w_out[0].astype(BF16), g_final[None, :],
    ]
    in_specs = [
        pl.BlockSpec((1, 2 * ts, D), lambda b, i, s: (b, i, 0)),
        pl.BlockSpec((1, ts, D), lambda b, i, s: (b, jnp.minimum(2 * i + 2, n_tiles - 1), 0)),
        pl.BlockSpec((1, 1, 2 * ts), lambda b, i, s: (b, 0, i)),
        pl.BlockSpec((1, 3, D), lambda b, i, s: (b, 0, 0)),
    ] + [full(a) for a in operands[4:]]

    grid_spec = pltpu.PrefetchScalarGridSpec(
        num_scalar_prefetch=1,
        grid=(B, n_tiles // 2),
        in_specs=in_specs,
        out_specs=pl.BlockSpec((1, 2 * ts, D), lambda b, i, s: (b, i, 0)),
        scratch_shapes=[
            pltpu.VMEM((GLA_HEADS, GLA_DK, GLA_DV), F32),
            pltpu.VMEM((SWA_KV_HEADS, ts + SWA_BLOCK, SWA_HEAD_DIM), BF16),
            pltpu.VMEM((SWA_KV_HEADS, ts + SWA_BLOCK, SWA_HEAD_DIM), BF16),
            pltpu.VMEM((ts, P_COLS), F32),
            pltpu.VMEM((ts, P_COLS), F32),
            pltpu.VMEM((ts, GLA_WIDTH + SWA_WIDTH), BF16),
            pltpu.VMEM((ts, GLA_WIDTH + SWA_WIDTH), BF16),
        ],
    )
    return pl.pallas_call(
        _layer_body,
        grid_spec=grid_spec,
        out_shape=jax.ShapeDtypeStruct((B, S, D), x.dtype),
        compiler_params=pltpu.CompilerParams(
            dimension_semantics=("arbitrary", "arbitrary"),
            vmem_limit_bytes=VMEM_LIMIT_BYTES,
        ),
        name="hymba_layer",
    )(sinks[0].astype(F32), *operands)
```

```python
import jax
import jax.numpy as jnp
import numpy as np
from jax import lax
from jax.experimental import pallas as pl
from jax.experimental.pallas import tpu as pltpu

D_MODEL = 1024
GLA_HEADS = 4
GLA_DK = 64
GLA_DV = 128
GLA_WIDTH = GLA_HEADS * GLA_DV
GLA_QK = GLA_HEADS * GLA_DK
GLA_GATE_RANK = 16
GLA_GATE_TAU = 16.0
GLA_CHUNK = 64
SWA_Q_HEADS = 8
SWA_KV_HEADS = 2
SWA_GROUP = SWA_Q_HEADS // SWA_KV_HEADS
SWA_HEAD_DIM = 64
SWA_WIDTH = SWA_Q_HEADS * SWA_HEAD_DIM
SWA_KV_WIDTH = SWA_KV_HEADS * SWA_HEAD_DIM
WINDOW = 128
SWA_BLOCK = 128
ROPE_THETA = 10000.0
RMS_EPS = 1e-6

LANES = 128
MXU_COLS = 256
SEQ_TILE = 256
LOG2E = 1.4426950408889634
NEG_BIG = -1e30
VMEM_LIMIT_BYTES = 56 * 1024 * 1024

W_GA = 0
W_SK = W_GA + LANES
W_SV = W_SK + SWA_KV_WIDTH
W_GQ = W_SV + SWA_KV_WIDTH
W_GK = W_GQ + GLA_QK
W_GV = W_GK + GLA_QK
W_SQ = W_GV + GLA_WIDTH
W_GZ = W_SQ + SWA_WIDTH
W_SZ = W_GZ + GLA_WIDTH
W_COLS = W_SZ + SWA_WIDTH

SWA_UNITS = (SEQ_TILE // SWA_BLOCK) * SWA_KV_HEADS

BF16 = jnp.bfloat16
F32 = jnp.float32


def _dot(a, b):
    return jnp.dot(a, b, preferred_element_type=F32)


def _dot_nt(a, b):
    return lax.dot_general(a, b, (((1,), (1,)), ((), ())), preferred_element_type=F32)


def _dot_tn(a, b):
    return lax.dot_general(a, b, (((0,), (0,)), ((), ())), preferred_element_type=F32)


def _silu(x):
    return x * (1.0 / (1.0 + jnp.exp(-x)))


def _log_sigmoid(z):
    return jnp.minimum(z, 0.0) - jnp.log(1.0 + jnp.exp(-jnp.abs(z)))


def _adaln_body(c_ref, w_ref, b_ref, o_ref):
    a = _silu(c_ref[...]).astype(BF16)
    o_ref[...] = _dot(a, w_ref[...].astype(BF16)) + b_ref[...]


def _adaln(c_pad, w_ada, b_ada):
    rows = c_pad.shape[0]
    n_out = w_ada.shape[1]
    col_block = D_MODEL
    return pl.pallas_call(
        _adaln_body,
        grid=(n_out // col_block,),
        in_specs=[
            pl.BlockSpec((rows, D_MODEL), lambda j: (0, 0)),
            pl.BlockSpec((D_MODEL, col_block), lambda j: (0, j)),
            pl.BlockSpec((1, col_block), lambda j: (0, j)),
        ],
        out_specs=pl.BlockSpec((rows, col_block), lambda j: (0, j)),
        out_shape=jax.ShapeDtypeStruct((rows, n_out), F32),
        name="adaln_mod",
    )(c_pad, w_ada, b_ada)


_OPERAND_BUFFERS = (
    ("qs", (GLA_HEADS, SEQ_TILE, LANES), BF16),
    ("qb", (GLA_HEADS, SEQ_TILE, SEQ_TILE), BF16),
    ("kb", (GLA_HEADS, SEQ_TILE, SEQ_TILE), BF16),
    ("kd", (SEQ_TILE, GLA_QK), BF16),
    ("gv", (SEQ_TILE, GLA_WIDTH), BF16),
    ("dec", (GLA_QK, SEQ_TILE), F32),
    ("gg", (SEQ_TILE, GLA_WIDTH), BF16),
    ("sg", (SEQ_TILE, SWA_WIDTH), BF16),
    ("qt", (SWA_UNITS, SWA_HEAD_DIM, SWA_GROUP * SWA_BLOCK), BF16),
    ("kk", (SWA_KV_HEADS, SEQ_TILE + SWA_BLOCK, SWA_HEAD_DIM), BF16),
    ("vv", (SWA_KV_HEADS, SEQ_TILE + SWA_BLOCK, SWA_HEAD_DIM), BF16),
)


class _Operands:
    def __init__(self, refs):
        for (name, _, _), ref in zip(_OPERAND_BUFFERS, refs):
            setattr(self, name, ref)


def _layer_body(sinks_ref, x_ref, xn_ref, pos_ref, posn_ref, mod_ref, gnorm_ref, invf_ref, tri_ref,
                w_in_ref, w_dec_ref, b_dec_ref, g_gla_ref, w_out_ref, g_fin_ref,
                o_ref, state_ref, cat0_scr, cat1_scr, *operand_refs):
    ts = SEQ_TILE
    i = pl.program_id(1)
    nbuf = len(_OPERAND_BUFFERS)
    ops0 = _Operands(operand_refs[:nbuf])
    ops1 = _Operands(operand_refs[nbuf:])

    shift = mod_ref[0, 0:1, :]
    scale = mod_ref[0, 1:2, :]
    gate = mod_ref[0, 2:3, :]
    norm_gain = gnorm_ref[...] * (1.0 + scale)
    n_chunks = ts // GLA_CHUNK
    n_blocks = ts // SWA_BLOCK
    units = [(n, kv) for n in range(n_blocks) for kv in range(SWA_KV_HEADS)]

    def proj(hb, c0, width):
        return _dot(hb, w_in_ref[:, c0:c0 + width])

    def prepare(x, pos, ops, prev_ops):
        ms = jnp.mean(x * x, axis=-1, keepdims=True)
        hb = ((x * lax.rsqrt(ms + RMS_EPS)) * norm_gain + shift).astype(BF16)
        yield

        d0 = proj(hb, W_GA, LANES + 2 * SWA_KV_WIDTH)
        yield

        ang_t = invf_ref[...] * pos.astype(F32)
        reps = LANES // (SWA_HEAD_DIM // 2)
        cos_f = jnp.transpose(jnp.concatenate([jnp.cos(ang_t)] * reps, axis=0))
        sin_f = jnp.transpose(jnp.concatenate([jnp.sin(ang_t)] * reps, axis=0))
        lane = lax.broadcasted_iota(jnp.int32, (1, LANES), 1)
        first_half = (lane % SWA_HEAD_DIM) < (SWA_HEAD_DIM // 2)
        sin_s = jnp.where(first_half, -sin_f, sin_f)

        def rope(tv):
            rot = jnp.where(first_half,
                            pltpu.roll(tv, LANES - SWA_HEAD_DIM // 2, axis=1),
                            pltpu.roll(tv, SWA_HEAD_DIM // 2, axis=1))
            return tv * cos_f + rot * sin_s
        yield

        z = _dot(d0[:, 0:LANES].astype(BF16), w_dec_ref[...]) + b_dec_ref[...]
        k_rot = rope(d0[:, LANES:LANES + SWA_KV_WIDTH]).astype(BF16)
        for kv in range(SWA_KV_HEADS):
            hs = slice(kv * SWA_HEAD_DIM, (kv + 1) * SWA_HEAD_DIM)
            v0 = LANES + SWA_KV_WIDTH + kv * SWA_HEAD_DIM
            ops.kk[kv, SWA_BLOCK:SWA_BLOCK + ts, :] = k_rot[:, hs]
            ops.vv[kv, SWA_BLOCK:SWA_BLOCK + ts, :] = d0[:, v0:v0 + SWA_HEAD_DIM].astype(BF16)
            if prev_ops is None:
                ops.kk[kv, 0:SWA_BLOCK, :] = jnp.zeros((SWA_BLOCK, SWA_HEAD_DIM), BF16)
                ops.vv[kv, 0:SWA_BLOCK, :] = jnp.zeros((SWA_BLOCK, SWA_HEAD_DIM), BF16)
            else:
                ops.kk[kv, 0:SWA_BLOCK, :] = prev_ops.kk[kv, ts:ts + SWA_BLOCK, :]
                ops.vv[kv, 0:SWA_BLOCK, :] = prev_ops.vv[kv, ts:ts + SWA_BLOCK, :]
        yield

        log_a = _log_sigmoid(z) * (1.0 / GLA_GATE_TAU)
        la_hi = log_a.astype(BF16)
        la_lo = (log_a - la_hi.astype(F32)).astype(BF16)
        b2 = _dot(tri_ref[...], jnp.concatenate([la_hi, la_lo], axis=1))
        gq = proj(hb, W_GQ, GLA_QK)
        yield
        gk = proj(hb, W_GK, GLA_QK)
        b = b2[:, :GLA_QK] + b2[:, GLA_QK:]
        b_last = jnp.concatenate(
            [jnp.broadcast_to(b[(c + 1) * GLA_CHUNK - 1:(c + 1) * GLA_CHUNK, :], (GLA_CHUNK, GLA_QK))
             for c in range(n_chunks)], axis=0)
        ops.dec[...] = jnp.transpose(jnp.exp(b_last))
        q_d = gq * (GLA_DK ** -0.5) * jnp.exp(b)
        yield
        ops.kd[...] = (gk * jnp.exp(-b)).astype(BF16)
        k_tail = gk * jnp.exp(b_last - b)

        row_i = lax.broadcasted_iota(jnp.int32, (ts, ts), 0)
        col_i = lax.broadcasted_iota(jnp.int32, (ts, ts), 1)
        same_chunk = (row_i // GLA_CHUNK) == (col_i // GLA_CHUNK)
        pair_lane = lax.broadcasted_iota(jnp.int32, (1, LANES), 1)
        tile_reps = ts // LANES
        for hd in range(GLA_HEADS):
            pr = slice((hd // 2) * LANES, (hd // 2 + 1) * LANES)
            in_head = (pair_lane // GLA_DK) == (hd % 2)
            q_sel = jnp.where(in_head, q_d[:, pr], 0.0)
            k_sel = jnp.where(in_head, k_tail[:, pr], 0.0)
            q_dup = q_sel + pltpu.roll(q_sel, GLA_DK, axis=1)
            k_dup = k_sel + pltpu.roll(k_sel, GLA_DK, axis=1)
            ops.qs[hd] = q_sel.astype(BF16)
            ops.qb[hd] = jnp.where(same_chunk, jnp.concatenate([q_dup] * tile_reps, axis=1), 0.0).astype(BF16)
            ops.kb[hd] = jnp.where(same_chunk, jnp.concatenate([k_dup] * tile_reps, axis=1), 0.0).astype(BF16)
            if hd % 2 == 1:
                yield

        for c0 in range(0, GLA_WIDTH, MXU_COLS):
            ops.gv[:, c0:c0 + MXU_COLS] = proj(hb, W_GV + c0, MXU_COLS).astype(BF16)
            yield

        for kv in range(SWA_KV_HEADS):
            sq = proj(hb, W_SQ + kv * MXU_COLS, MXU_COLS)
            q_rot = jnp.concatenate([rope(sq[:, j * LANES:(j + 1) * LANES]) for j in range(MXU_COLS // LANES)],
                                    axis=1) * (SWA_HEAD_DIM ** -0.5 * LOG2E)
            for n in range(n_blocks):
                q_t = jnp.transpose(q_rot[n * SWA_BLOCK:(n + 1) * SWA_BLOCK, :])
                ops.qt[n * SWA_KV_HEADS + kv] = jnp.concatenate(
                    [q_t[j * SWA_HEAD_DIM:(j + 1) * SWA_HEAD_DIM, :] for j in range(SWA_GROUP)],
                    axis=1).astype(BF16)
            yield

        for c0 in range(0, GLA_WIDTH, MXU_COLS):
            ops.gg[:, c0:c0 + MXU_COLS] = _silu(proj(hb, W_GZ + c0, MXU_COLS)).astype(BF16)
            yield
        for c0 in range(0, SWA_WIDTH, MXU_COLS):
            ops.sg[:, c0:c0 + MXU_COLS] = _silu(proj(hb, W_SZ + c0, MXU_COLS)).astype(BF16)
            yield

    def gla(ops, cat_scr):
        row_i = lax.broadcasted_iota(jnp.int32, (ts, ts), 0)
        col_i = lax.broadcasted_iota(jnp.int32, (ts, ts), 1)
        causal = ((row_i // GLA_CHUNK) == (col_i // GLA_CHUNK)) & (col_i <= row_i)
        heads = []
        for hd in range(GLA_HEADS):
            pr = slice((hd // 2) * LANES, (hd // 2 + 1) * LANES)
            v_h = ops.gv[:, hd * GLA_DV:(hd + 1) * GLA_DV]
            s = jnp.where(causal, _dot_nt(ops.qs[hd], ops.kd[:, pr]), 0.0).astype(BF16)
            u_all = _dot_tn(ops.kb[hd], v_h)
            heads.append((v_h, s, u_all))
            if hd % 2 == 1:
                yield

        staged = []
        for hd, (v_h, s, u_all) in enumerate(heads):
            st = state_ref[hd]
            entering = []
            for c in range(n_chunks):
                entering.append(st)
                last = c * GLA_CHUNK + GLA_CHUNK - 1
                d_col = ops.dec[hd * GLA_DK:(hd + 1) * GLA_DK, last:last + 1]
                st = st * d_col + u_all[c * GLA_DK:(c + 1) * GLA_DK, :]
            state_ref[hd] = st
            s_stack = jnp.concatenate(entering, axis=0).astype(BF16)
            staged.append((s, v_h, s_stack))
        yield

        outs = []
        for hd, (s, v_h, s_stack) in enumerate(staged):
            outs.append(_dot(s, v_h) + _dot(ops.qb[hd], s_stack))
            if hd % 2 == 1:
                yield

        g_gla = g_gla_ref[...]
        for hd, o_h in enumerate(outs):
            vs = slice(hd * GLA_DV, (hd + 1) * GLA_DV)
            o_ms = jnp.mean(o_h * o_h, axis=-1, keepdims=True)
            o_n = o_h * lax.rsqrt(o_ms + RMS_EPS) * g_gla[:, vs]
            cat_scr[:, vs] = (o_n * ops.gg[:, vs].astype(F32)).astype(BF16)

    def swa(ops, cat_scr, first_tile):
        gq_lanes = SWA_GROUP * SWA_BLOCK
        kj = lax.broadcasted_iota(jnp.int32, (2 * SWA_BLOCK, gq_lanes), 0)
        ql = lax.broadcasted_iota(jnp.int32, (2 * SWA_BLOCK, gq_lanes), 1)
        dist = (ql % SWA_BLOCK) + SWA_BLOCK - kj
        valid = (dist >= 0) & (dist < WINDOW)
        head_of_lane = lax.broadcasted_iota(jnp.int32, (1, gq_lanes), 1) // SWA_BLOCK

        scores = []
        for u, (n, kv) in enumerate(units):
            bs = slice(n * SWA_BLOCK, (n + 2) * SWA_BLOCK)
            scores.append(_dot(ops.kk[kv, bs, :], ops.qt[u]))
            if u % 2 == 1:
                yield

        probs = []
        for (n, kv), s_raw in zip(units, scores):
            if n == 0 and first_tile is not None:
                ok = valid & ((kj >= SWA_BLOCK) | jnp.logical_not(first_tile))
            else:
                ok = valid
            s_t = s_raw + jnp.where(ok, 0.0, NEG_BIG)
            sink = jnp.zeros((1, gq_lanes), F32)
            for j in range(SWA_GROUP):
                sink = jnp.where(head_of_lane == j, sinks_ref[kv * SWA_GROUP + j] * LOG2E, sink)
            m = jnp.maximum(jnp.max(s_t, axis=0, keepdims=True), sink)
            e = jnp.exp2(s_t - m)
            den = jnp.sum(e, axis=0, keepdims=True) + jnp.exp2(sink - m)
            probs.append((e.astype(BF16), 1.0 / den))
            yield

        outs = []
        for (n, kv), (e, inv_den) in zip(units, probs):
            bs = slice(n * SWA_BLOCK, (n + 2) * SWA_BLOCK)
            outs.append(_dot_tn(ops.vv[kv, bs, :], e) * inv_den)
        yield

        for n in range(n_blocks):
            rs = slice(n * SWA_BLOCK, (n + 1) * SWA_BLOCK)
            o_groups = []
            for kv in range(SWA_KV_HEADS):
                o_t = outs[n * SWA_KV_HEADS + kv]
                o_stack = jnp.concatenate([o_t[:, j * SWA_BLOCK:(j + 1) * SWA_BLOCK] for j in range(SWA_GROUP)],
                                          axis=0)
                o_groups.append(jnp.transpose(o_stack))
            o_swa = jnp.concatenate(o_groups, axis=1)
            cat_scr[rs, GLA_WIDTH:] = (o_swa * ops.sg[rs, :].astype(F32)).astype(BF16)

    def finish(cat_scr, rows):
        y = _dot(cat_scr[...], w_out_ref[...])
        yield
        x = x_ref[0, rows, :]
        xo = x + gate * y
        ms_o = jnp.mean(xo * xo, axis=-1, keepdims=True)
        o_ref[0, rows, :] = xo * lax.rsqrt(ms_o + RMS_EPS) * g_fin_ref[...]

    def interleave(primary, filler):
        live = list(primary)
        while live:
            for g in list(live):
                if next(g, StopIteration) is StopIteration:
                    live.remove(g)
                next(filler, None)
        for _ in filler:
            pass

    rows0 = slice(0, ts)
    rows1 = slice(ts, 2 * ts)

    @pl.when(i == 0)
    def _():
        state_ref[...] = jnp.zeros_like(state_ref)
        for _ in prepare(x_ref[0, rows0, :], pos_ref[0, :, rows0], ops0, None):
            pass

    interleave([gla(ops0, cat0_scr), swa(ops0, cat0_scr, i == 0)],
               prepare(x_ref[0, rows1, :], pos_ref[0, :, rows1], ops1, ops0))
    interleave([finish(cat0_scr, rows0), gla(ops1, cat1_scr), swa(ops1, cat1_scr, None)],
               prepare(xn_ref[0], posn_ref[0], ops0, ops1))
    for _ in finish(cat1_scr, rows1):
        pass


def _chunk_cumsum_matrix(ts):
    r = np.arange(ts)
    same = (r[:, None] // GLA_CHUNK) == (r[None, :] // GLA_CHUNK)
    return jnp.asarray(same & (r[None, :] <= r[:, None]), BF16)


def _rope_inv_freq_column():
    inv = 1.0 / (ROPE_THETA ** (jnp.arange(0, SWA_HEAD_DIM, 2, dtype=F32) / SWA_HEAD_DIM))
    return inv.reshape(SWA_HEAD_DIM // 2, 1)


def kernel(x, c, positions, w_ada, b_ada, g_norm, w_in, w_decay, b_decay, g_gla_head, sinks, w_out, g_final):
    B, S, D = x.shape
    ts = SEQ_TILE
    assert D == D_MODEL and S % (2 * ts) == 0 and ts % SWA_BLOCK == 0
    assert w_ada.shape[0] == 1, "one layer"
    n_tiles = S // ts

    rows = 8
    c_pad = jnp.zeros((rows, D), F32).at[:B].set(c.astype(F32))
    mod = _adaln(c_pad, w_ada[0], b_ada[0][None, :])[:B].reshape(B, 3, D)

    wi = w_in[0].astype(BF16)
    o0 = 0
    cols = {}
    for name, n in (("gq", GLA_QK), ("gk", GLA_QK), ("gv", GLA_WIDTH), ("ga", GLA_GATE_RANK),
                    ("gz", GLA_WIDTH), ("sq", SWA_WIDTH), ("sk", SWA_KV_WIDTH), ("sv", SWA_KV_WIDTH),
                    ("sz", SWA_WIDTH)):
        cols[name] = wi[:, o0:o0 + n]
        o0 += n
    w_ga = jnp.pad(cols["ga"], ((0, 0), (0, LANES - GLA_GATE_RANK)))
    w_in_b = jnp.concatenate([w_ga, cols["sk"], cols["sv"], cols["gq"], cols["gk"], cols["gv"],
                              cols["sq"], cols["gz"], cols["sz"]], axis=1)
    assert w_in_b.shape == (D, W_COLS)
    w_dec = jnp.pad(w_decay[0].astype(BF16), ((0, LANES - GLA_GATE_RANK), (0, 0)))
    tri = _chunk_cumsum_matrix(ts)
    pos3 = positions.reshape(B, 1, S)

    const2 = lambda b, i, s: (0, 0)
    full = lambda a: pl.BlockSpec(a.shape, const2)
    next_tile = lambda i: jnp.minimum(2 * i + 2, n_tiles - 1)
    operands = [
        x, x, pos3, pos3,
        mod,
        g_norm[0][None, :],
        _rope_inv_freq_column(),
        tri,
        w_in_b,
        w_dec, b_decay[0][None, :], g_gla_head[0][None, :],
        w_out[0].astype(BF16), g_final[None, :],
    ]
    in_specs = [
        pl.BlockSpec((1, 2 * ts, D), lambda b, i, s: (b, i, 0)),
        pl.BlockSpec((1, ts, D), lambda b, i, s: (b, next_tile(i), 0)),
        pl.BlockSpec((1, 1, 2 * ts), lambda b, i, s: (b, 0, i)),
        pl.BlockSpec((1, 1, ts), lambda b, i, s: (b, 0, next_tile(i))),
        pl.BlockSpec((1, 3, D), lambda b, i, s: (b, 0, 0)),
    ] + [full(a) for a in operands[5:]]

    operand_scratch = [pltpu.VMEM(shape, dtype) for _, shape, dtype in _OPERAND_BUFFERS]
    grid_spec = pltpu.PrefetchScalarGridSpec(
        num_scalar_prefetch=1,
        grid=(B, n_tiles // 2),
        in_specs=in_specs,
        out_specs=pl.BlockSpec((1, 2 * ts, D), lambda b, i, s: (b, i, 0)),
        scratch_shapes=[
            pltpu.VMEM((GLA_HEADS, GLA_DK, GLA_DV), F32),
            pltpu.VMEM((ts, GLA_WIDTH + SWA_WIDTH), BF16),
            pltpu.VMEM((ts, GLA_WIDTH + SWA_WIDTH), BF16),
        ] + operand_scratch + operand_scratch,
    )
    return pl.pallas_call(
        _layer_body,
        grid_spec=grid_spec,
        out_shape=jax.ShapeDtypeStruct((B, S, D), x.dtype),
        compiler_params=pltpu.CompilerParams(
            dimension_semantics=("arbitrary", "arbitrary"),
            vmem_limit_bytes=VMEM_LIMIT_BYTES,
        ),
        name="hymba_layer",
    )(sinks[0].astype(F32), *operands)
```

```python
import jax
import jax.numpy as jnp
import numpy as np
from jax import lax
from jax.experimental import pallas as pl
from jax.experimental.pallas import tpu as pltpu

D_MODEL = 1024
GLA_HEADS = 4
GLA_DK = 64
GLA_DV = 128
GLA_WIDTH = GLA_HEADS * GLA_DV
GLA_QK = GLA_HEADS * GLA_DK
GLA_GATE_RANK = 16
GLA_GATE_TAU = 16.0
GLA_CHUNK = 64
SWA_Q_HEADS = 8
SWA_KV_HEADS = 2
SWA_GROUP = SWA_Q_HEADS // SWA_KV_HEADS
SWA_HEAD_DIM = 64
SWA_WIDTH = SWA_Q_HEADS * SWA_HEAD_DIM
SWA_KV_WIDTH = SWA_KV_HEADS * SWA_HEAD_DIM
WINDOW = 128
SWA_BLOCK = 128
ROPE_THETA = 10000.0
RMS_EPS = 1e-6

LANES = 128
MXU_COLS = 256
PROJ_COLS = 2 * MXU_COLS
SEQ_TILE = 256
STEP_TILES = 4
LOG2E = 1.4426950408889634
NEG_BIG = -1e30
VMEM_LIMIT_BYTES = 56 * 1024 * 1024

W_GA = 0
W_SK = W_GA + LANES
W_SV = W_SK + SWA_KV_WIDTH
W_GQ = W_SV + SWA_KV_WIDTH
W_GK = W_GQ + GLA_QK
W_GV = W_GK + GLA_QK
W_SQ = W_GV + GLA_WIDTH
W_GZ = W_SQ + SWA_WIDTH
W_SZ = W_GZ + GLA_WIDTH
W_COLS = W_SZ + SWA_WIDTH

SWA_UNITS = (SEQ_TILE // SWA_BLOCK) * SWA_KV_HEADS

BF16 = jnp.bfloat16
F32 = jnp.float32


def _dot(a, b):
    return jnp.dot(a, b, preferred_element_type=F32)


def _dot_nt(a, b):
    return lax.dot_general(a, b, (((1,), (1,)), ((), ())), preferred_element_type=F32)


def _dot_tn(a, b):
    return lax.dot_general(a, b, (((0,), (0,)), ((), ())), preferred_element_type=F32)


def _silu(x):
    return x * (1.0 / (1.0 + jnp.exp(-x)))


def _log_sigmoid(z):
    return jnp.minimum(z, 0.0) - jnp.log(1.0 + jnp.exp(-jnp.abs(z)))


def _adaln_body(c_ref, w_ref, b_ref, o_ref):
    a = _silu(c_ref[...]).astype(BF16)
    o_ref[...] = _dot(a, w_ref[...].astype(BF16)) + b_ref[...]


def _adaln(c_pad, w_ada, b_ada):
    rows = c_pad.shape[0]
    n_out = w_ada.shape[1]
    col_block = D_MODEL
    return pl.pallas_call(
        _adaln_body,
        grid=(n_out // col_block,),
        in_specs=[
            pl.BlockSpec((rows, D_MODEL), lambda j: (0, 0)),
            pl.BlockSpec((D_MODEL, col_block), lambda j: (0, j)),
            pl.BlockSpec((1, col_block), lambda j: (0, j)),
        ],
        out_specs=pl.BlockSpec((rows, col_block), lambda j: (0, j)),
        out_shape=jax.ShapeDtypeStruct((rows, n_out), F32),
        name="adaln_mod",
    )(c_pad, w_ada, b_ada)


_SRC_GA = 2 * GLA_QK + GLA_WIDTH
_SRC_TAIL = _SRC_GA + GLA_GATE_RANK
_SRC_COLS = _SRC_TAIL + GLA_WIDTH + SWA_WIDTH + 2 * SWA_KV_WIDTH + SWA_WIDTH
W_ROW_BLOCK = 128


def _regroup_body(w_ref, o_ref):
    w = w_ref[...]
    lane = lax.broadcasted_iota(jnp.int32, (1, LANES), 1)
    ga = jnp.where(lane < GLA_GATE_RANK, w[:, _SRC_GA:_SRC_GA + LANES], 0.0)
    tail = w[:, _SRC_TAIL:_SRC_COLS]
    t_sq = GLA_WIDTH
    t_sk = t_sq + SWA_WIDTH
    t_sv = t_sk + SWA_KV_WIDTH
    t_sz = t_sv + SWA_KV_WIDTH
    pieces = (
        (W_GA, ga),
        (W_SK, tail[:, t_sk:t_sk + SWA_KV_WIDTH]),
        (W_SV, tail[:, t_sv:t_sv + SWA_KV_WIDTH]),
        (W_GQ, w[:, 0:GLA_QK]),
        (W_GK, w[:, GLA_QK:2 * GLA_QK]),
        (W_GV, w[:, 2 * GLA_QK:2 * GLA_QK + GLA_WIDTH]),
        (W_SQ, tail[:, t_sq:t_sq + SWA_WIDTH]),
        (W_GZ, tail[:, 0:GLA_WIDTH]),
        (W_SZ, tail[:, t_sz:t_sz + SWA_WIDTH]),
    )
    for c0, piece in pieces:
        o_ref[:, c0:c0 + piece.shape[1]] = piece.astype(BF16)


def _regroup_w_in(w):
    d, n = w.shape
    assert n == _SRC_COLS and d % W_ROW_BLOCK == 0
    return pl.pallas_call(
        _regroup_body,
        grid=(d // W_ROW_BLOCK,),
        in_specs=[pl.BlockSpec((W_ROW_BLOCK, n), lambda r: (r, 0))],
        out_specs=pl.BlockSpec((W_ROW_BLOCK, W_COLS), lambda r: (r, 0)),
        out_shape=jax.ShapeDtypeStruct((d, W_COLS), BF16),
        name="regroup_w_in",
    )(w)


_OPERAND_BUFFERS = (
    ("qs", (GLA_HEADS, SEQ_TILE, LANES), BF16),
    ("qb", (GLA_HEADS, SEQ_TILE, SEQ_TILE), BF16),
    ("kb", (GLA_HEADS, SEQ_TILE, SEQ_TILE), BF16),
    ("kd", (SEQ_TILE, GLA_QK), BF16),
    ("gv", (SEQ_TILE, GLA_WIDTH), BF16),
    ("dec", (GLA_QK, SEQ_TILE), F32),
    ("gg", (SEQ_TILE, GLA_WIDTH), BF16),
    ("sg", (SEQ_TILE, SWA_WIDTH), BF16),
    ("qt", (SWA_UNITS, SWA_HEAD_DIM, SWA_GROUP * SWA_BLOCK), BF16),
    ("kk", (SWA_KV_HEADS, SEQ_TILE + SWA_BLOCK, SWA_HEAD_DIM), BF16),
    ("vv", (SWA_KV_HEADS, SEQ_TILE + SWA_BLOCK, SWA_HEAD_DIM), BF16),
)


class _Operands:
    def __init__(self, refs):
        for (name, _, _), ref in zip(_OPERAND_BUFFERS, refs):
            setattr(self, name, ref)


def _layer_body(sinks_ref, x_ref, xn_ref, pos_ref, posn_ref, mod_ref, gnorm_ref, invf_ref, tri_ref,
                w_in_ref, w_dec_ref, b_dec_ref, g_gla_ref, w_out_ref, g_fin_ref,
                o_ref, state_ref, cat0_scr, cat1_scr, *operand_refs):
    ts = SEQ_TILE
    i = pl.program_id(1)
    nbuf = len(_OPERAND_BUFFERS)
    ops0 = _Operands(operand_refs[:nbuf])
    ops1 = _Operands(operand_refs[nbuf:])

    shift = mod_ref[0, 0:1, :]
    scale = mod_ref[0, 1:2, :]
    gate = mod_ref[0, 2:3, :]
    norm_gain = gnorm_ref[...] * (1.0 + scale)
    n_chunks = ts // GLA_CHUNK
    n_blocks = ts // SWA_BLOCK
    units = [(n, kv) for n in range(n_blocks) for kv in range(SWA_KV_HEADS)]

    def proj(hb, c0, width):
        return _dot(hb, w_in_ref[:, c0:c0 + width])

    def prepare(x, pos, ops, prev_ops):
        ms = jnp.mean(x * x, axis=-1, keepdims=True)
        hb = ((x * lax.rsqrt(ms + RMS_EPS)) * norm_gain + shift).astype(BF16)
        yield

        d0 = proj(hb, W_GA, LANES + 2 * SWA_KV_WIDTH)
        yield

        ang_t = invf_ref[...] * pos.astype(F32)
        reps = LANES // (SWA_HEAD_DIM // 2)
        cos_f = jnp.transpose(jnp.concatenate([jnp.cos(ang_t)] * reps, axis=0))
        sin_f = jnp.transpose(jnp.concatenate([jnp.sin(ang_t)] * reps, axis=0))
        lane = lax.broadcasted_iota(jnp.int32, (1, LANES), 1)
        first_half = (lane % SWA_HEAD_DIM) < (SWA_HEAD_DIM // 2)
        sin_s = jnp.where(first_half, -sin_f, sin_f)

        def rope(tv):
            rot = jnp.where(first_half,
                            pltpu.roll(tv, LANES - SWA_HEAD_DIM // 2, axis=1),
                            pltpu.roll(tv, SWA_HEAD_DIM // 2, axis=1))
            return tv * cos_f + rot * sin_s
        yield

        z = _dot(d0[:, 0:LANES].astype(BF16), w_dec_ref[...]) + b_dec_ref[...]
        k_rot = rope(d0[:, LANES:LANES + SWA_KV_WIDTH]).astype(BF16)
        for kv in range(SWA_KV_HEADS):
            hs = slice(kv * SWA_HEAD_DIM, (kv + 1) * SWA_HEAD_DIM)
            v0 = LANES + SWA_KV_WIDTH + kv * SWA_HEAD_DIM
            ops.kk[kv, SWA_BLOCK:SWA_BLOCK + ts, :] = k_rot[:, hs]
            ops.vv[kv, SWA_BLOCK:SWA_BLOCK + ts, :] = d0[:, v0:v0 + SWA_HEAD_DIM].astype(BF16)
            if prev_ops is None:
                ops.kk[kv, 0:SWA_BLOCK, :] = jnp.zeros((SWA_BLOCK, SWA_HEAD_DIM), BF16)
                ops.vv[kv, 0:SWA_BLOCK, :] = jnp.zeros((SWA_BLOCK, SWA_HEAD_DIM), BF16)
            else:
                ops.kk[kv, 0:SWA_BLOCK, :] = prev_ops.kk[kv, ts:ts + SWA_BLOCK, :]
                ops.vv[kv, 0:SWA_BLOCK, :] = prev_ops.vv[kv, ts:ts + SWA_BLOCK, :]
        yield

        log_a = _log_sigmoid(z) * (1.0 / GLA_GATE_TAU)
        la_hi = log_a.astype(BF16)
        la_lo = (log_a - la_hi.astype(F32)).astype(BF16)
        b2 = _dot(tri_ref[...], jnp.concatenate([la_hi, la_lo], axis=1))
        gqk = proj(hb, W_GQ, 2 * GLA_QK)
        gq = gqk[:, :GLA_QK]
        gk = gqk[:, GLA_QK:]
        yield
        b = b2[:, :GLA_QK] + b2[:, GLA_QK:]
        b_last = jnp.concatenate(
            [jnp.broadcast_to(b[(c + 1) * GLA_CHUNK - 1:(c + 1) * GLA_CHUNK, :], (GLA_CHUNK, GLA_QK))
             for c in range(n_chunks)], axis=0)
        ops.dec[...] = jnp.transpose(jnp.exp(b_last))
        q_d = gq * (GLA_DK ** -0.5) * jnp.exp(b)
        yield
        ops.kd[...] = (gk * jnp.exp(-b)).astype(BF16)
        k_tail = gk * jnp.exp(b_last - b)

        row_i = lax.broadcasted_iota(jnp.int32, (ts, ts), 0)
        col_i = lax.broadcasted_iota(jnp.int32, (ts, ts), 1)
        same_chunk = (row_i // GLA_CHUNK) == (col_i // GLA_CHUNK)
        pair_lane = lax.broadcasted_iota(jnp.int32, (1, LANES), 1)
        tile_reps = ts // LANES
        for hd in range(GLA_HEADS):
            pr = slice((hd // 2) * LANES, (hd // 2 + 1) * LANES)
            in_head = (pair_lane // GLA_DK) == (hd % 2)
            q_sel = jnp.where(in_head, q_d[:, pr], 0.0)
            k_sel = jnp.where(in_head, k_tail[:, pr], 0.0)
            q_dup = q_sel + pltpu.roll(q_sel, GLA_DK, axis=1)
            k_dup = k_sel + pltpu.roll(k_sel, GLA_DK, axis=1)
            ops.qs[hd] = q_sel.astype(BF16)
            ops.qb[hd] = jnp.where(same_chunk, jnp.concatenate([q_dup] * tile_reps, axis=1), 0.0).astype(BF16)
            ops.kb[hd] = jnp.where(same_chunk, jnp.concatenate([k_dup] * tile_reps, axis=1), 0.0).astype(BF16)
            if hd % 2 == 1:
                yield

        for c0 in range(0, GLA_WIDTH, PROJ_COLS):
            ops.gv[:, c0:c0 + PROJ_COLS] = proj(hb, W_GV + c0, PROJ_COLS).astype(BF16)
            yield

        sq_all = proj(hb, W_SQ, SWA_WIDTH)
        for kv in range(SWA_KV_HEADS):
            sq = sq_all[:, kv * MXU_COLS:(kv + 1) * MXU_COLS]
            q_rot = jnp.concatenate([rope(sq[:, j * LANES:(j + 1) * LANES]) for j in range(MXU_COLS // LANES)],
                                    axis=1) * (SWA_HEAD_DIM ** -0.5 * LOG2E)
            for n in range(n_blocks):
                q_t = jnp.transpose(q_rot[n * SWA_BLOCK:(n + 1) * SWA_BLOCK, :])
                ops.qt[n * SWA_KV_HEADS + kv] = jnp.concatenate(
                    [q_t[j * SWA_HEAD_DIM:(j + 1) * SWA_HEAD_DIM, :] for j in range(SWA_GROUP)],
                    axis=1).astype(BF16)
            yield

        for c0 in range(0, GLA_WIDTH, PROJ_COLS):
            ops.gg[:, c0:c0 + PROJ_COLS] = _silu(proj(hb, W_GZ + c0, PROJ_COLS)).astype(BF16)
            yield
        for c0 in range(0, SWA_WIDTH, PROJ_COLS):
            ops.sg[:, c0:c0 + PROJ_COLS] = _silu(proj(hb, W_SZ + c0, PROJ_COLS)).astype(BF16)
            yield

    def gla(ops, cat_scr):
        row_i = lax.broadcasted_iota(jnp.int32, (ts, ts), 0)
        col_i = lax.broadcasted_iota(jnp.int32, (ts, ts), 1)
        causal = ((row_i // GLA_CHUNK) == (col_i // GLA_CHUNK)) & (col_i <= row_i)
        heads = []
        for hd in range(GLA_HEADS):
            pr = slice((hd // 2) * LANES, (hd // 2 + 1) * LANES)
            v_h = ops.gv[:, hd * GLA_DV:(hd + 1) * GLA_DV]
            s = jnp.where(causal, _dot_nt(ops.qs[hd], ops.kd[:, pr]), 0.0).astype(BF16)
            u_all = _dot_tn(ops.kb[hd], v_h)
            heads.append((v_h, s, u_all))
            if hd % 2 == 1:
                yield

        staged = []
        for hd, (v_h, s, u_all) in enumerate(heads):
            st = state_ref[hd]
            entering = []
            for c in range(n_chunks):
                entering.append(st)
                last = c * GLA_CHUNK + GLA_CHUNK - 1
                d_col = ops.dec[hd * GLA_DK:(hd + 1) * GLA_DK, last:last + 1]
                st = st * d_col + u_all[c * GLA_DK:(c + 1) * GLA_DK, :]
            state_ref[hd] = st
            s_stack = jnp.concatenate(entering, axis=0).astype(BF16)
            staged.append((s, v_h, s_stack))
        yield

        outs = []
        for hd, (s, v_h, s_stack) in enumerate(staged):
            outs.append(_dot(s, v_h) + _dot(ops.qb[hd], s_stack))
            if hd % 2 == 1:
                yield

        g_gla = g_gla_ref[...]
        for hd, o_h in enumerate(outs):
            vs = slice(hd * GLA_DV, (hd + 1) * GLA_DV)
            o_ms = jnp.mean(o_h * o_h, axis=-1, keepdims=True)
            o_n = o_h * lax.rsqrt(o_ms + RMS_EPS) * g_gla[:, vs]
            cat_scr[:, vs] = (o_n * ops.gg[:, vs].astype(F32)).astype(BF16)

    def swa(ops, cat_scr, first_tile):
        gq_lanes = SWA_GROUP * SWA_BLOCK
        kj = lax.broadcasted_iota(jnp.int32, (2 * SWA_BLOCK, gq_lanes), 0)
        ql = lax.broadcasted_iota(jnp.int32, (2 * SWA_BLOCK, gq_lanes), 1)
        dist = (ql % SWA_BLOCK) + SWA_BLOCK - kj
        valid = (dist >= 0) & (dist < WINDOW)
        head_of_lane = lax.broadcasted_iota(jnp.int32, (1, gq_lanes), 1) // SWA_BLOCK

        scores = []
        for u, (n, kv) in enumerate(units):
            bs = slice(n * SWA_BLOCK, (n + 2) * SWA_BLOCK)
            scores.append(_dot(ops.kk[kv, bs, :], ops.qt[u]))
            if u % 2 == 1:
                yield

        probs = []
        for (n, kv), s_raw in zip(units, scores):
            if n == 0 and first_tile is not None:
                ok = valid & ((kj >= SWA_BLOCK) | jnp.logical_not(first_tile))
            else:
                ok = valid
            s_t = s_raw + jnp.where(ok, 0.0, NEG_BIG)
            sink = jnp.zeros((1, gq_lanes), F32)
            for j in range(SWA_GROUP):
                sink = jnp.where(head_of_lane == j, sinks_ref[kv * SWA_GROUP + j] * LOG2E, sink)
            m = jnp.maximum(jnp.max(s_t, axis=0, keepdims=True), sink)
            e = jnp.exp2(s_t - m)
            den = jnp.sum(e, axis=0, keepdims=True) + jnp.exp2(sink - m)
            probs.append((e.astype(BF16), 1.0 / den))
            yield

        outs = []
        for (n, kv), (e, inv_den) in zip(units, probs):
            bs = slice(n * SWA_BLOCK, (n + 2) * SWA_BLOCK)
            outs.append(_dot_tn(ops.vv[kv, bs, :], e) * inv_den)
        yield

        for n in range(n_blocks):
            rs = slice(n * SWA_BLOCK, (n + 1) * SWA_BLOCK)
            o_groups = []
            for kv in range(SWA_KV_HEADS):
                o_t = outs[n * SWA_KV_HEADS + kv]
                o_stack = jnp.concatenate([o_t[:, j * SWA_BLOCK:(j + 1) * SWA_BLOCK] for j in range(SWA_GROUP)],
                                          axis=0)
                o_groups.append(jnp.transpose(o_stack))
            o_swa = jnp.concatenate(o_groups, axis=1)
            cat_scr[rs, GLA_WIDTH:] = (o_swa * ops.sg[rs, :].astype(F32)).astype(BF16)

    def finish(cat_scr, rows):
        y = _dot(cat_scr[...], w_out_ref[...])
        yield
        x = x_ref[0, rows, :]
        xo = x + gate * y
        ms_o = jnp.mean(xo * xo, axis=-1, keepdims=True)
        o_ref[0, rows, :] = xo * lax.rsqrt(ms_o + RMS_EPS) * g_fin_ref[...]

    def interleave(primary, filler):
        live = list(primary)
        while live:
            for g in list(live):
                if next(g, StopIteration) is StopIteration:
                    live.remove(g)
                next(filler, None)
        for _ in filler:
            pass

    rows = [slice(k * ts, (k + 1) * ts) for k in range(STEP_TILES)]
    ops = (ops0, ops1)
    cats = (cat0_scr, cat1_scr)

    @pl.when(i == 0)
    def _():
        state_ref[...] = jnp.zeros_like(state_ref)
        for _ in prepare(x_ref[0, rows[0], :], pos_ref[0, :, rows[0]], ops0, None):
            pass

    for k in range(STEP_TILES):
        cur, nxt = ops[k % 2], ops[(k + 1) % 2]
        primary = [gla(cur, cats[k % 2]), swa(cur, cats[k % 2], (i == 0) if k == 0 else None)]
        if k > 0:
            primary.insert(0, finish(cats[(k - 1) % 2], rows[k - 1]))
        if k + 1 < STEP_TILES:
            filler = prepare(x_ref[0, rows[k + 1], :], pos_ref[0, :, rows[k + 1]], nxt, cur)
        else:
            filler = prepare(xn_ref[0], posn_ref[0], nxt, cur)
        interleave(primary, filler)
    for _ in finish(cats[(STEP_TILES - 1) % 2], rows[STEP_TILES - 1]):
        pass


def _chunk_cumsum_matrix(ts):
    r = np.arange(ts)
    same = (r[:, None] // GLA_CHUNK) == (r[None, :] // GLA_CHUNK)
    return jnp.asarray(same & (r[None, :] <= r[:, None]), BF16)


def _rope_inv_freq_column():
    inv = 1.0 / (ROPE_THETA ** (jnp.arange(0, SWA_HEAD_DIM, 2, dtype=F32) / SWA_HEAD_DIM))
    return inv.reshape(SWA_HEAD_DIM // 2, 1)


def kernel(x, c, positions, w_ada, b_ada, g_norm, w_in, w_decay, b_decay, g_gla_head, sinks, w_out, g_final):
    B, S, D = x.shape
    ts = SEQ_TILE
    assert D == D_MODEL and S % (STEP_TILES * ts) == 0 and ts % SWA_BLOCK == 0 and STEP_TILES % 2 == 0
    assert w_ada.shape[0] == 1, "one layer"
    n_tiles = S // ts

    rows = 8
    c_pad = jnp.zeros((rows, D), F32).at[:B].set(c.astype(F32))
    mod = _adaln(c_pad, w_ada[0], b_ada[0][None, :])[:B].reshape(B, 3, D)

    w_in_b = _regroup_w_in(w_in[0])
    w_dec = jnp.pad(w_decay[0].astype(BF16), ((0, LANES - GLA_GATE_RANK), (0, 0)))
    tri = _chunk_cumsum_matrix(ts)
    pos3 = positions.reshape(B, 1, S)

    const2 = lambda b, i, s: (0, 0)
    full = lambda a: pl.BlockSpec(a.shape, const2)
    next_tile = lambda i: jnp.minimum(STEP_TILES * (i + 1), n_tiles - 1)
    operands = [
        x, x, pos3, pos3,
        mod,
        g_norm[0][None, :],
        _rope_inv_freq_column(),
        tri,
        w_in_b,
        w_dec, b_decay[0][None, :], g_gla_head[0][None, :],
        w_out[0].astype(BF16), g_final[None, :],
    ]
    in_specs = [
        pl.BlockSpec((1, STEP_TILES * ts, D), lambda b, i, s: (b, i, 0)),
        pl.BlockSpec((1, ts, D), lambda b, i, s: (b, next_tile(i), 0)),
        pl.BlockSpec((1, 1, STEP_TILES * ts), lambda b, i, s: (b, 0, i)),
        pl.BlockSpec((1, 1, ts), lambda b, i, s: (b, 0, next_tile(i))),
        pl.BlockSpec((1, 3, D), lambda b, i, s: (b, 0, 0)),
    ] + [full(a) for a in operands[5:]]

    operand_scratch = [pltpu.VMEM(shape, dtype) for _, shape, dtype in _OPERAND_BUFFERS]
    grid_spec = pltpu.PrefetchScalarGridSpec(
        num_scalar_prefetch=1,
        grid=(B, n_tiles // STEP_TILES),
        in_specs=in_specs,
        out_specs=pl.BlockSpec((1, STEP_TILES * ts, D), lambda b, i, s: (b, i, 0)),
        scratch_shapes=[
            pltpu.VMEM((GLA_HEADS, GLA_DK, GLA_DV), F32),
            pltpu.VMEM((ts, GLA_WIDTH + SWA_WIDTH), BF16),
            pltpu.VMEM((ts, GLA_WIDTH + SWA_WIDTH), BF16),
        ] + operand_scratch + operand_scratch,
    )
    return pl.pallas_call(
        _layer_body,
        grid_spec=grid_spec,
        out_shape=jax.ShapeDtypeStruct((B, S, D), x.dtype),
        compiler_params=pltpu.CompilerParams(
            dimension_semantics=("arbitrary", "arbitrary"),
            vmem_limit_bytes=VMEM_LIMIT_BYTES,
        ),
        name="hymba_layer",
    )(sinks[0].astype(F32), *operands)
```

```python
import jax
import jax.numpy as jnp
import numpy as np
from jax import lax
from jax.experimental import pallas as pl
from jax.experimental.pallas import tpu as pltpu

D_MODEL = 1024
GLA_HEADS = 4
GLA_DK = 64
GLA_DV = 128
GLA_WIDTH = GLA_HEADS * GLA_DV
GLA_QK = GLA_HEADS * GLA_DK
GLA_GATE_RANK = 16
GLA_GATE_TAU = 16.0
GLA_CHUNK = 64
SWA_Q_HEADS = 8
SWA_KV_HEADS = 2
SWA_GROUP = SWA_Q_HEADS // SWA_KV_HEADS
SWA_HEAD_DIM = 64
SWA_WIDTH = SWA_Q_HEADS * SWA_HEAD_DIM
SWA_KV_WIDTH = SWA_KV_HEADS * SWA_HEAD_DIM
WINDOW = 128
SWA_BLOCK = 128
ROPE_THETA = 10000.0
RMS_EPS = 1e-6

LANES = 128
MXU_COLS = 256
PROJ_COLS = 2 * MXU_COLS
SEQ_TILE = 256
STEP_TILES = 2
LOG2E = 1.4426950408889634
NEG_BIG = -1e30
VMEM_LIMIT_BYTES = 56 * 1024 * 1024

W_GA = 0
W_SK = W_GA + LANES
W_SV = W_SK + SWA_KV_WIDTH
W_GQ = W_SV + SWA_KV_WIDTH
W_GK = W_GQ + GLA_QK
W_GV = W_GK + GLA_QK
W_SQ = W_GV + GLA_WIDTH
W_GZ = W_SQ + SWA_WIDTH
W_SZ = W_GZ + GLA_WIDTH
W_COLS = W_SZ + SWA_WIDTH

SWA_UNITS = (SEQ_TILE // SWA_BLOCK) * SWA_KV_HEADS

BF16 = jnp.bfloat16
F32 = jnp.float32


def _dot(a, b):
    return jnp.dot(a, b, preferred_element_type=F32)


def _dot_nt(a, b):
    return lax.dot_general(a, b, (((1,), (1,)), ((), ())), preferred_element_type=F32)


def _dot_tn(a, b):
    return lax.dot_general(a, b, (((0,), (0,)), ((), ())), preferred_element_type=F32)


def _silu(x):
    return x * (1.0 / (1.0 + jnp.exp(-x)))


def _log_sigmoid(z):
    return jnp.minimum(z, 0.0) - jnp.log(1.0 + jnp.exp(-jnp.abs(z)))


def _adaln_body(c_ref, w_ref, b_ref, o_ref):
    a = _silu(c_ref[...]).astype(BF16)
    o_ref[...] = _dot(a, w_ref[...].astype(BF16)) + b_ref[...]


def _adaln(c_pad, w_ada, b_ada):
    rows = c_pad.shape[0]
    n_out = w_ada.shape[1]
    col_block = D_MODEL
    return pl.pallas_call(
        _adaln_body,
        grid=(n_out // col_block,),
        in_specs=[
            pl.BlockSpec((rows, D_MODEL), lambda j: (0, 0)),
            pl.BlockSpec((D_MODEL, col_block), lambda j: (0, j)),
            pl.BlockSpec((1, col_block), lambda j: (0, j)),
        ],
        out_specs=pl.BlockSpec((rows, col_block), lambda j: (0, j)),
        out_shape=jax.ShapeDtypeStruct((rows, n_out), F32),
        name="adaln_mod",
    )(c_pad, w_ada, b_ada)


_SRC_GA = 2 * GLA_QK + GLA_WIDTH
_SRC_GZ = _SRC_GA + GLA_GATE_RANK
_SRC_SQ = _SRC_GZ + GLA_WIDTH
_SRC_SK = _SRC_SQ + SWA_WIDTH
_SRC_SV = _SRC_SK + SWA_KV_WIDTH
_SRC_SZ = _SRC_SV + SWA_KV_WIDTH
_SRC_ROWS = _SRC_SZ + SWA_WIDTH
_REGROUP = (
    (W_SK, _SRC_SK, SWA_KV_WIDTH), (W_SV, _SRC_SV, SWA_KV_WIDTH),
    (W_GQ, 0, GLA_QK), (W_GK, GLA_QK, GLA_QK), (W_GV, 2 * GLA_QK, GLA_WIDTH),
    (W_SQ, _SRC_SQ, SWA_WIDTH), (W_GZ, _SRC_GZ, GLA_WIDTH), (W_SZ, _SRC_SZ, SWA_WIDTH),
)
W_COL_BLOCK = 256


def _regroup_body(wt_ref, o_ref):
    row = lax.broadcasted_iota(jnp.int32, (LANES, 1), 0)
    ga = jnp.where(row < GLA_GATE_RANK, wt_ref[_SRC_GA:_SRC_GA + LANES, :], 0.0)
    o_ref[:, W_GA:W_GA + LANES] = jnp.transpose(ga).astype(BF16)
    for dst, src, width in _REGROUP:
        o_ref[:, dst:dst + width] = jnp.transpose(wt_ref[src:src + width, :]).astype(BF16)


def _regroup_w_in(wt):
    n, d = wt.shape
    assert n == _SRC_ROWS and d % W_COL_BLOCK == 0
    return pl.pallas_call(
        _regroup_body,
        grid=(d // W_COL_BLOCK,),
        in_specs=[pl.BlockSpec((n, W_COL_BLOCK), lambda r: (0, r))],
        out_specs=pl.BlockSpec((W_COL_BLOCK, W_COLS), lambda r: (r, 0)),
        out_shape=jax.ShapeDtypeStruct((d, W_COLS), BF16),
        name="regroup_w_in",
    )(wt)


_OPERAND_BUFFERS = (
    ("qs", (GLA_HEADS, SEQ_TILE, LANES), BF16),
    ("qb", (GLA_HEADS, SEQ_TILE, SEQ_TILE), BF16),
    ("kb", (GLA_HEADS, SEQ_TILE, SEQ_TILE), BF16),
    ("kd", (SEQ_TILE, GLA_QK), BF16),
    ("gv", (SEQ_TILE, GLA_WIDTH), BF16),
    ("dec", (GLA_QK, SEQ_TILE), F32),
    ("gg", (SEQ_TILE, GLA_WIDTH), BF16),
    ("sg", (SEQ_TILE, SWA_WIDTH), BF16),
    ("qt", (SWA_UNITS, SWA_HEAD_DIM, SWA_GROUP * SWA_BLOCK), BF16),
    ("kk", (SWA_KV_HEADS, SEQ_TILE + SWA_BLOCK, SWA_HEAD_DIM), BF16),
    ("vv", (SWA_KV_HEADS, SEQ_TILE + SWA_BLOCK, SWA_HEAD_DIM), BF16),
)


class _Operands:
    def __init__(self, refs):
        for (name, _, _), ref in zip(_OPERAND_BUFFERS, refs):
            setattr(self, name, ref)


def _layer_body(sinks_ref, x_ref, xn_ref, pos_ref, posn_ref, mod_ref, gnorm_ref, invf_ref, tri_ref,
                w_in_ref, w_dec_ref, b_dec_ref, g_gla_ref, w_out_ref, g_fin_ref,
                o_ref, state_ref, cat0_scr, cat1_scr, *operand_refs):
    ts = SEQ_TILE
    i = pl.program_id(1)
    nbuf = len(_OPERAND_BUFFERS)
    ops0 = _Operands(operand_refs[:nbuf])
    ops1 = _Operands(operand_refs[nbuf:])

    shift = mod_ref[0, 0:1, :]
    scale = mod_ref[0, 1:2, :]
    gate = mod_ref[0, 2:3, :]
    norm_gain = gnorm_ref[...] * (1.0 + scale)
    n_chunks = ts // GLA_CHUNK
    n_blocks = ts // SWA_BLOCK
    units = [(n, kv) for n in range(n_blocks) for kv in range(SWA_KV_HEADS)]

    def proj(hb, c0, width):
        return _dot(hb, w_in_ref[:, c0:c0 + width])

    def prepare(x, pos, ops, prev_ops):
        ms = jnp.mean(x * x, axis=-1, keepdims=True)
        hb = ((x * lax.rsqrt(ms + RMS_EPS)) * norm_gain + shift).astype(BF16)
        yield

        d0 = proj(hb, W_GA, LANES + 2 * SWA_KV_WIDTH)
        yield

        ang_t = invf_ref[...] * pos.astype(F32)
        reps = LANES // (SWA_HEAD_DIM // 2)
        cos_f = jnp.transpose(jnp.concatenate([jnp.cos(ang_t)] * reps, axis=0))
        sin_f = jnp.transpose(jnp.concatenate([jnp.sin(ang_t)] * reps, axis=0))
        lane = lax.broadcasted_iota(jnp.int32, (1, LANES), 1)
        first_half = (lane % SWA_HEAD_DIM) < (SWA_HEAD_DIM // 2)
        sin_s = jnp.where(first_half, -sin_f, sin_f)

        def rope(tv):
            rot = jnp.where(first_half,
                            pltpu.roll(tv, LANES - SWA_HEAD_DIM // 2, axis=1),
                            pltpu.roll(tv, SWA_HEAD_DIM // 2, axis=1))
            return tv * cos_f + rot * sin_s
        yield

        z = _dot(d0[:, 0:LANES].astype(BF16), w_dec_ref[...]) + b_dec_ref[...]
        k_rot = rope(d0[:, LANES:LANES + SWA_KV_WIDTH]).astype(BF16)
        for kv in range(SWA_KV_HEADS):
            hs = slice(kv * SWA_HEAD_DIM, (kv + 1) * SWA_HEAD_DIM)
            v0 = LANES + SWA_KV_WIDTH + kv * SWA_HEAD_DIM
            ops.kk[kv, SWA_BLOCK:SWA_BLOCK + ts, :] = k_rot[:, hs]
            ops.vv[kv, SWA_BLOCK:SWA_BLOCK + ts, :] = d0[:, v0:v0 + SWA_HEAD_DIM].astype(BF16)
            if prev_ops is None:
                ops.kk[kv, 0:SWA_BLOCK, :] = jnp.zeros((SWA_BLOCK, SWA_HEAD_DIM), BF16)
                ops.vv[kv, 0:SWA_BLOCK, :] = jnp.zeros((SWA_BLOCK, SWA_HEAD_DIM), BF16)
            else:
                ops.kk[kv, 0:SWA_BLOCK, :] = prev_ops.kk[kv, ts:ts + SWA_BLOCK, :]
                ops.vv[kv, 0:SWA_BLOCK, :] = prev_ops.vv[kv, ts:ts + SWA_BLOCK, :]
        yield

        log_a = _log_sigmoid(z) * (1.0 / GLA_GATE_TAU)
        la_hi = log_a.astype(BF16)
        la_lo = (log_a - la_hi.astype(F32)).astype(BF16)
        b2 = _dot(tri_ref[...], jnp.concatenate([la_hi, la_lo], axis=1))
        gqk = proj(hb, W_GQ, 2 * GLA_QK)
        gq = gqk[:, :GLA_QK]
        gk = gqk[:, GLA_QK:]
        yield
        b = b2[:, :GLA_QK] + b2[:, GLA_QK:]
        b_last = jnp.concatenate(
            [jnp.broadcast_to(b[(c + 1) * GLA_CHUNK - 1:(c + 1) * GLA_CHUNK, :], (GLA_CHUNK, GLA_QK))
             for c in range(n_chunks)], axis=0)
        ops.dec[...] = jnp.transpose(jnp.exp(b_last))
        q_d = gq * (GLA_DK ** -0.5) * jnp.exp(b)
        yield
        ops.kd[...] = (gk * jnp.exp(-b)).astype(BF16)
        k_tail = gk * jnp.exp(b_last - b)

        row_i = lax.broadcasted_iota(jnp.int32, (ts, ts), 0)
        col_i = lax.broadcasted_iota(jnp.int32, (ts, ts), 1)
        same_chunk = (row_i // GLA_CHUNK) == (col_i // GLA_CHUNK)
        pair_lane = lax.broadcasted_iota(jnp.int32, (1, LANES), 1)
        tile_reps = ts // LANES
        for hd in range(GLA_HEADS):
            pr = slice((hd // 2) * LANES, (hd // 2 + 1) * LANES)
            in_head = (pair_lane // GLA_DK) == (hd % 2)
            q_sel = jnp.where(in_head, q_d[:, pr], 0.0)
            k_sel = jnp.where(in_head, k_tail[:, pr], 0.0)
            q_dup = q_sel + pltpu.roll(q_sel, GLA_DK, axis=1)
            k_dup = k_sel + pltpu.roll(k_sel, GLA_DK, axis=1)
            ops.qs[hd] = q_sel.astype(BF16)
            ops.qb[hd] = jnp.where(same_chunk, jnp.concatenate([q_dup] * tile_reps, axis=1), 0.0).astype(BF16)
            ops.kb[hd] = jnp.where(same_chunk, jnp.concatenate([k_dup] * tile_reps, axis=1), 0.0).astype(BF16)
            if hd % 2 == 1:
                yield

        for c0 in range(0, GLA_WIDTH, PROJ_COLS):
            ops.gv[:, c0:c0 + PROJ_COLS] = proj(hb, W_GV + c0, PROJ_COLS).astype(BF16)
            yield

        sq_all = proj(hb, W_SQ, SWA_WIDTH)
        for kv in range(SWA_KV_HEADS):
            sq = sq_all[:, kv * MXU_COLS:(kv + 1) * MXU_COLS]
            q_rot = jnp.concatenate([rope(sq[:, j * LANES:(j + 1) * LANES]) for j in range(MXU_COLS // LANES)],
                                    axis=1) * (SWA_HEAD_DIM ** -0.5 * LOG2E)
            for n in range(n_blocks):
                q_t = jnp.transpose(q_rot[n * SWA_BLOCK:(n + 1) * SWA_BLOCK, :])
                ops.qt[n * SWA_KV_HEADS + kv] = jnp.concatenate(
                    [q_t[j * SWA_HEAD_DIM:(j + 1) * SWA_HEAD_DIM, :] for j in range(SWA_GROUP)],
                    axis=1).astype(BF16)
            yield

        for c0 in range(0, GLA_WIDTH, PROJ_COLS):
            ops.gg[:, c0:c0 + PROJ_COLS] = _silu(proj(hb, W_GZ + c0, PROJ_COLS)).astype(BF16)
            yield
        for c0 in range(0, SWA_WIDTH, PROJ_COLS):
            ops.sg[:, c0:c0 + PROJ_COLS] = _silu(proj(hb, W_SZ + c0, PROJ_COLS)).astype(BF16)
            yield

    def gla(ops, cat_scr):
        row_i = lax.broadcasted_iota(jnp.int32, (ts, ts), 0)
        col_i = lax.broadcasted_iota(jnp.int32, (ts, ts), 1)
        causal = ((row_i // GLA_CHUNK) == (col_i // GLA_CHUNK)) & (col_i <= row_i)
        heads = []
        for hd in range(GLA_HEADS):
            pr = slice((hd // 2) * LANES, (hd // 2 + 1) * LANES)
            v_h = ops.gv[:, hd * GLA_DV:(hd + 1) * GLA_DV]
            s = jnp.where(causal, _dot_nt(ops.qs[hd], ops.kd[:, pr]), 0.0).astype(BF16)
            u_all = _dot_tn(ops.kb[hd], v_h)
            heads.append((v_h, s, u_all))
            if hd % 2 == 1:
                yield

        staged = []
        for hd, (v_h, s, u_all) in enumerate(heads):
            st = state_ref[hd]
            entering = []
            for c in range(n_chunks):
                entering.append(st)
                last = c * GLA_CHUNK + GLA_CHUNK - 1
                d_col = ops.dec[hd * GLA_DK:(hd + 1) * GLA_DK, last:last + 1]
                st = st * d_col + u_all[c * GLA_DK:(c + 1) * GLA_DK, :]
            state_ref[hd] = st
            s_stack = jnp.concatenate(entering, axis=0).astype(BF16)
            staged.append((s, v_h, s_stack))
        yield

        outs = []
        for hd, (s, v_h, s_stack) in enumerate(staged):
            outs.append(_dot(s, v_h) + _dot(ops.qb[hd], s_stack))
            if hd % 2 == 1:
                yield

        g_gla = g_gla_ref[...]
        for hd, o_h in enumerate(outs):
            vs = slice(hd * GLA_DV, (hd + 1) * GLA_DV)
            o_ms = jnp.mean(o_h * o_h, axis=-1, keepdims=True)
            o_n = o_h * lax.rsqrt(o_ms + RMS_EPS) * g_gla[:, vs]
            cat_scr[:, vs] = (o_n * ops.gg[:, vs].astype(F32)).astype(BF16)

    def swa(ops, cat_scr, first_tile):
        gq_lanes = SWA_GROUP * SWA_BLOCK
        kj = lax.broadcasted_iota(jnp.int32, (2 * SWA_BLOCK, gq_lanes), 0)
        ql = lax.broadcasted_iota(jnp.int32, (2 * SWA_BLOCK, gq_lanes), 1)
        dist = (ql % SWA_BLOCK) + SWA_BLOCK - kj
        valid = (dist >= 0) & (dist < WINDOW)
        head_of_lane = lax.broadcasted_iota(jnp.int32, (1, gq_lanes), 1) // SWA_BLOCK

        scores = []
        for u, (n, kv) in enumerate(units):
            bs = slice(n * SWA_BLOCK, (n + 2) * SWA_BLOCK)
            scores.append(_dot(ops.kk[kv, bs, :], ops.qt[u]))
            if u % 2 == 1:
                yield

        probs = []
        for (n, kv), s_raw in zip(units, scores):
            if n == 0 and first_tile is not None:
                ok = valid & ((kj >= SWA_BLOCK) | jnp.logical_not(first_tile))
            else:
                ok = valid
            s_t = s_raw + jnp.where(ok, 0.0, NEG_BIG)
            sink = jnp.zeros((1, gq_lanes), F32)
            for j in range(SWA_GROUP):
                sink = jnp.where(head_of_lane == j, sinks_ref[kv * SWA_GROUP + j] * LOG2E, sink)
            m = jnp.maximum(jnp.max(s_t, axis=0, keepdims=True), sink)
            e = jnp.exp2(s_t - m)
            den = jnp.sum(e, axis=0, keepdims=True) + jnp.exp2(sink - m)
            probs.append((e.astype(BF16), 1.0 / den))
            yield

        outs = []
        for (n, kv), (e, inv_den) in zip(units, probs):
            bs = slice(n * SWA_BLOCK, (n + 2) * SWA_BLOCK)
            outs.append(_dot_tn(ops.vv[kv, bs, :], e) * inv_den)
        yield

        for n in range(n_blocks):
            rs = slice(n * SWA_BLOCK, (n + 1) * SWA_BLOCK)
            o_groups = []
            for kv in range(SWA_KV_HEADS):
                o_t = outs[n * SWA_KV_HEADS + kv]
                o_stack = jnp.concatenate([o_t[:, j * SWA_BLOCK:(j + 1) * SWA_BLOCK] for j in range(SWA_GROUP)],
                                          axis=0)
                o_groups.append(jnp.transpose(o_stack))
            o_swa = jnp.concatenate(o_groups, axis=1)
            cat_scr[rs, GLA_WIDTH:] = (o_swa * ops.sg[rs, :].astype(F32)).astype(BF16)

    def finish(cat_scr, rows):
        y = _dot(cat_scr[...], w_out_ref[...])
        yield
        x = x_ref[0, rows, :]
        xo = x + gate * y
        ms_o = jnp.mean(xo * xo, axis=-1, keepdims=True)
        o_ref[0, rows, :] = xo * lax.rsqrt(ms_o + RMS_EPS) * g_fin_ref[...]

    def interleave(primary, filler):
        live = list(primary)
        while live:
            for g in list(live):
                if next(g, StopIteration) is StopIteration:
                    live.remove(g)
                next(filler, None)
        for _ in filler:
            pass

    rows = [slice(k * ts, (k + 1) * ts) for k in range(STEP_TILES)]
    ops = (ops0, ops1)
    cats = (cat0_scr, cat1_scr)

    @pl.when(i == 0)
    def _():
        state_ref[...] = jnp.zeros_like(state_ref)
        for _ in prepare(x_ref[0, rows[0], :], pos_ref[0, :, rows[0]], ops0, None):
            pass

    for k in range(STEP_TILES):
        cur, nxt = ops[k % 2], ops[(k + 1) % 2]
        primary = [gla(cur, cats[k % 2]), swa(cur, cats[k % 2], (i == 0) if k == 0 else None)]
        if k > 0:
            primary.insert(0, finish(cats[(k - 1) % 2], rows[k - 1]))
        if k + 1 < STEP_TILES:
            filler = prepare(x_ref[0, rows[k + 1], :], pos_ref[0, :, rows[k + 1]], nxt, cur)
        else:
            filler = prepare(xn_ref[0], posn_ref[0], nxt, cur)
        interleave(primary, filler)
    for _ in finish(cats[(STEP_TILES - 1) % 2], rows[STEP_TILES - 1]):
        pass


def _chunk_cumsum_matrix(ts):
    r = np.arange(ts)
    same = (r[:, None] // GLA_CHUNK) == (r[None, :] // GLA_CHUNK)
    return jnp.asarray(same & (r[None, :] <= r[:, None]), BF16)


def _rope_inv_freq_column():
    inv = 1.0 / (ROPE_THETA ** (jnp.arange(0, SWA_HEAD_DIM, 2, dtype=F32) / SWA_HEAD_DIM))
    return inv.reshape(SWA_HEAD_DIM // 2, 1)


def kernel(x, c, positions, w_ada, b_ada, g_norm, w_in, w_decay, b_decay, g_gla_head, sinks, w_out, g_final):
    B, S, D = x.shape
    ts = SEQ_TILE
    assert D == D_MODEL and S % (STEP_TILES * ts) == 0 and ts % SWA_BLOCK == 0 and STEP_TILES % 2 == 0
    assert w_ada.shape[0] == 1, "one layer"
    n_tiles = S // ts

    rows = 8
    c_pad = jnp.zeros((rows, D), F32).at[:B].set(c.astype(F32))
    mod = _adaln(c_pad, w_ada[0], b_ada[0][None, :])[:B].reshape(B, 3, D)

    w_in_b = _regroup_w_in(jnp.transpose(w_in[0]))
    w_dec = jnp.pad(w_decay[0].astype(BF16), ((0, LANES - GLA_GATE_RANK), (0, 0)))
    tri = _chunk_cumsum_matrix(ts)
    pos3 = positions.reshape(B, 1, S)

    const2 = lambda b, i, s: (0, 0)
    full = lambda a: pl.BlockSpec(a.shape, const2)
    next_tile = lambda i: jnp.minimum(STEP_TILES * (i + 1), n_tiles - 1)
    operands = [
        x, x, pos3, pos3,
        mod,
        g_norm[0][None, :],
        _rope_inv_freq_column(),
        tri,
        w_in_b,
        w_dec, b_decay[0][None, :], g_gla_head[0][None, :],
        w_out[0].astype(BF16), g_final[None, :],
    ]
    in_specs = [
        pl.BlockSpec((1, STEP_TILES * ts, D), lambda b, i, s: (b, i, 0)),
        pl.BlockSpec((1, ts, D), lambda b, i, s: (b, next_tile(i), 0)),
        pl.BlockSpec((1, 1, STEP_TILES * ts), lambda b, i, s: (b, 0, i)),
        pl.BlockSpec((1, 1, ts), lambda b, i, s: (b, 0, next_tile(i))),
        pl.BlockSpec((1, 3, D), lambda b, i, s: (b, 0, 0)),
    ] + [full(a) for a in operands[5:]]

    operand_scratch = [pltpu.VMEM(shape, dtype) for _, shape, dtype in _OPERAND_BUFFERS]
    grid_spec = pltpu.PrefetchScalarGridSpec(
        num_scalar_prefetch=1,
        grid=(B, n_tiles // STEP_TILES),
        in_specs=in_specs,
        out_specs=pl.BlockSpec((1, STEP_TILES * ts, D), lambda b, i, s: (b, i, 0)),
        scratch_shapes=[
            pltpu.VMEM((GLA_HEADS, GLA_DK, GLA_DV), F32),
            pltpu.VMEM((ts, GLA_WIDTH + SWA_WIDTH), BF16),
            pltpu.VMEM((ts, GLA_WIDTH + SWA_WIDTH), BF16),
        ] + operand_scratch + operand_scratch,
    )
    return pl.pallas_call(
        _layer_body,
        grid_spec=grid_spec,
        out_shape=jax.ShapeDtypeStruct((B, S, D), x.dtype),
        compiler_params=pltpu.CompilerParams(
            dimension_semantics=("arbitrary", "arbitrary"),
            vmem_limit_bytes=VMEM_LIMIT_BYTES,
        ),
        name="hymba_layer",
    )(sinks[0].astype(F32), *operands)
```

```python
import jax
import jax.numpy as jnp
import numpy as np
from jax import lax
from jax.experimental import pallas as pl
from jax.experimental.pallas import tpu as pltpu

D_MODEL = 1024
GLA_HEADS = 4
GLA_DK = 64
GLA_DV = 128
GLA_WIDTH = GLA_HEADS * GLA_DV
GLA_QK = GLA_HEADS * GLA_DK
GLA_GATE_RANK = 16
GLA_GATE_TAU = 16.0
GLA_CHUNK = 64
SWA_Q_HEADS = 8
SWA_KV_HEADS = 2
SWA_GROUP = SWA_Q_HEADS // SWA_KV_HEADS
SWA_HEAD_DIM = 64
SWA_WIDTH = SWA_Q_HEADS * SWA_HEAD_DIM
SWA_KV_WIDTH = SWA_KV_HEADS * SWA_HEAD_DIM
WINDOW = 128
SWA_BLOCK = 128
ROPE_THETA = 10000.0
RMS_EPS = 1e-6

LANES = 128
MXU_COLS = 256
PROJ_COLS = 2 * MXU_COLS
SEQ_TILE = 256
STEP_TILES = 2
LOG2E = 1.4426950408889634
NEG_BIG = -1e30
VMEM_LIMIT_BYTES = 56 * 1024 * 1024

W_GA = 0
W_SK = W_GA + LANES
W_SV = W_SK + SWA_KV_WIDTH
W_GQ = W_SV + SWA_KV_WIDTH
W_GK = W_GQ + GLA_QK
W_GV = W_GK + GLA_QK
W_SQ = W_GV + GLA_WIDTH
W_GZ = W_SQ + SWA_WIDTH
W_SZ = W_GZ + GLA_WIDTH
W_COLS = W_SZ + SWA_WIDTH

SWA_UNITS = (SEQ_TILE // SWA_BLOCK) * SWA_KV_HEADS

BF16 = jnp.bfloat16
F32 = jnp.float32


def _dot(a, b):
    return jnp.dot(a, b, preferred_element_type=F32)


def _dot_nt(a, b):
    return lax.dot_general(a, b, (((1,), (1,)), ((), ())), preferred_element_type=F32)


def _dot_tn(a, b):
    return lax.dot_general(a, b, (((0,), (0,)), ((), ())), preferred_element_type=F32)


def _silu(x):
    return x * (1.0 / (1.0 + jnp.exp(-x)))


def _log_sigmoid(z):
    return jnp.minimum(z, 0.0) - jnp.log(1.0 + jnp.exp(-jnp.abs(z)))


def _adaln_body(c_ref, w_ref, b_ref, o_ref):
    a = _silu(c_ref[...]).astype(BF16)
    o_ref[...] = _dot(a, w_ref[...].astype(BF16)) + b_ref[...]


def _adaln(c, w_ada, b_ada):
    rows = c.shape[0]
    n_out = w_ada.shape[1]
    col_block = D_MODEL
    return pl.pallas_call(
        _adaln_body,
        grid=(n_out // col_block,),
        in_specs=[
            pl.BlockSpec((rows, D_MODEL), lambda j: (0, 0)),
            pl.BlockSpec((D_MODEL, col_block), lambda j: (0, j)),
            pl.BlockSpec((1, col_block), lambda j: (0, j)),
        ],
        out_specs=pl.BlockSpec((rows, col_block), lambda j: (0, j)),
        out_shape=jax.ShapeDtypeStruct((rows, n_out), F32),
        name="adaln_mod",
    )(c, w_ada, b_ada)


_SRC_GA = 2 * GLA_QK + GLA_WIDTH
_SRC_GZ = _SRC_GA + GLA_GATE_RANK
_SRC_SQ = _SRC_GZ + GLA_WIDTH
_SRC_SK = _SRC_SQ + SWA_WIDTH
_SRC_SV = _SRC_SK + SWA_KV_WIDTH
_SRC_SZ = _SRC_SV + SWA_KV_WIDTH
_SRC_ROWS = _SRC_SZ + SWA_WIDTH
_REGROUP = (
    (W_SK, _SRC_SK, SWA_KV_WIDTH), (W_SV, _SRC_SV, SWA_KV_WIDTH),
    (W_GQ, 0, GLA_QK), (W_GK, GLA_QK, GLA_QK), (W_GV, 2 * GLA_QK, GLA_WIDTH),
    (W_SQ, _SRC_SQ, SWA_WIDTH), (W_GZ, _SRC_GZ, GLA_WIDTH), (W_SZ, _SRC_SZ, SWA_WIDTH),
)
W_COL_BLOCK = 256


def _prepare_weights_body(wt_ref, w_out_ref, o_ref, w_out_o_ref):
    w_out_o_ref[...] = w_out_ref[...].astype(BF16)
    row = lax.broadcasted_iota(jnp.int32, (LANES, 1), 0)
    ga = jnp.where(row < GLA_GATE_RANK, wt_ref[_SRC_GA:_SRC_GA + LANES, :], 0.0)
    o_ref[:, W_GA:W_GA + LANES] = jnp.transpose(ga).astype(BF16)
    for dst, src, width in _REGROUP:
        o_ref[:, dst:dst + width] = jnp.transpose(wt_ref[src:src + width, :]).astype(BF16)


def _prepare_weights(wt, w_out):
    n, d = wt.shape
    k_out, n_out = w_out.shape
    assert n == _SRC_ROWS and d % W_COL_BLOCK == 0 and k_out % (d // W_COL_BLOCK) == 0
    steps = d // W_COL_BLOCK
    out_rows = k_out // steps
    return pl.pallas_call(
        _prepare_weights_body,
        grid=(steps,),
        in_specs=[pl.BlockSpec((n, W_COL_BLOCK), lambda r: (0, r)),
                  pl.BlockSpec((out_rows, n_out), lambda r: (r, 0))],
        out_specs=[pl.BlockSpec((W_COL_BLOCK, W_COLS), lambda r: (r, 0)),
                   pl.BlockSpec((out_rows, n_out), lambda r: (r, 0))],
        out_shape=[jax.ShapeDtypeStruct((d, W_COLS), BF16), jax.ShapeDtypeStruct((k_out, n_out), BF16)],
        name="prepare_weights",
    )(wt, w_out)


_OPERAND_BUFFERS = (
    ("qs", (GLA_HEADS, SEQ_TILE, LANES), BF16),
    ("qb", (GLA_HEADS, SEQ_TILE, SEQ_TILE), BF16),
    ("kb", (GLA_HEADS, SEQ_TILE, SEQ_TILE), BF16),
    ("kd", (SEQ_TILE, GLA_QK), BF16),
    ("gv", (SEQ_TILE, GLA_WIDTH), BF16),
    ("dec", (GLA_QK, SEQ_TILE), F32),
    ("gg", (SEQ_TILE, GLA_WIDTH), BF16),
    ("sg", (SEQ_TILE, SWA_WIDTH), BF16),
    ("qt", (SWA_UNITS, SWA_HEAD_DIM, SWA_GROUP * SWA_BLOCK), BF16),
    ("kk", (SWA_KV_HEADS, SEQ_TILE + SWA_BLOCK, SWA_HEAD_DIM), BF16),
    ("vv", (SWA_KV_HEADS, SEQ_TILE + SWA_BLOCK, SWA_HEAD_DIM), BF16),
)


class _Operands:
    def __init__(self, refs):
        for (name, _, _), ref in zip(_OPERAND_BUFFERS, refs):
            setattr(self, name, ref)


def _layer_body(sinks_ref, x_ref, xn_ref, pos_ref, posn_ref, mod_ref, modn_ref, gnorm_ref, invf_ref, tri_ref,
                w_in_ref, w_dec_ref, b_dec_ref, g_gla_ref, w_out_ref, g_fin_ref,
                o_ref, state_ref, cat0_scr, cat1_scr, *operand_refs):
    ts = SEQ_TILE
    b_idx = pl.program_id(0)
    i = pl.program_id(1)
    last_step = i == pl.num_programs(1) - 1
    nbuf = len(_OPERAND_BUFFERS)
    ops0 = _Operands(operand_refs[:nbuf])
    ops1 = _Operands(operand_refs[nbuf:])

    gate = mod_ref[0, :, 2 * D_MODEL:3 * D_MODEL]

    def norm_modulation(m_ref):
        return (gnorm_ref[...] * (1.0 + m_ref[0, :, D_MODEL:2 * D_MODEL]), m_ref[0, :, 0:D_MODEL])

    n_chunks = ts // GLA_CHUNK
    n_blocks = ts // SWA_BLOCK
    units = [(n, kv) for n in range(n_blocks) for kv in range(SWA_KV_HEADS)]

    def proj(hb, c0, width):
        return _dot(hb, w_in_ref[:, c0:c0 + width])

    def prepare(x, pos, modulation, ops, prev_ops, starts_sequence):
        norm_gain, shift = modulation
        ms = jnp.mean(x * x, axis=-1, keepdims=True)
        hb = ((x * lax.rsqrt(ms + RMS_EPS)) * norm_gain + shift).astype(BF16)
        yield

        d0 = proj(hb, W_GA, LANES + 2 * SWA_KV_WIDTH)
        yield

        ang_t = invf_ref[...] * pos.astype(F32)
        reps = LANES // (SWA_HEAD_DIM // 2)
        cos_f = jnp.transpose(jnp.concatenate([jnp.cos(ang_t)] * reps, axis=0))
        sin_f = jnp.transpose(jnp.concatenate([jnp.sin(ang_t)] * reps, axis=0))
        lane = lax.broadcasted_iota(jnp.int32, (1, LANES), 1)
        first_half = (lane % SWA_HEAD_DIM) < (SWA_HEAD_DIM // 2)
        sin_s = jnp.where(first_half, -sin_f, sin_f)

        def rope(tv):
            rot = jnp.where(first_half,
                            pltpu.roll(tv, LANES - SWA_HEAD_DIM // 2, axis=1),
                            pltpu.roll(tv, SWA_HEAD_DIM // 2, axis=1))
            return tv * cos_f + rot * sin_s
        yield

        z = _dot(d0[:, 0:LANES].astype(BF16), w_dec_ref[...]) + b_dec_ref[...]
        k_rot = rope(d0[:, LANES:LANES + SWA_KV_WIDTH]).astype(BF16)
        for kv in range(SWA_KV_HEADS):
            hs = slice(kv * SWA_HEAD_DIM, (kv + 1) * SWA_HEAD_DIM)
            v0 = LANES + SWA_KV_WIDTH + kv * SWA_HEAD_DIM
            ops.kk[kv, SWA_BLOCK:SWA_BLOCK + ts, :] = k_rot[:, hs]
            ops.vv[kv, SWA_BLOCK:SWA_BLOCK + ts, :] = d0[:, v0:v0 + SWA_HEAD_DIM].astype(BF16)
            zeros = jnp.zeros((SWA_BLOCK, SWA_HEAD_DIM), BF16)
            if starts_sequence is True:
                k_carry, v_carry = zeros, zeros
            else:
                k_carry = prev_ops.kk[kv, ts:ts + SWA_BLOCK, :]
                v_carry = prev_ops.vv[kv, ts:ts + SWA_BLOCK, :]
                if starts_sequence is not False:
                    k_carry = jnp.where(starts_sequence, zeros, k_carry)
                    v_carry = jnp.where(starts_sequence, zeros, v_carry)
            ops.kk[kv, 0:SWA_BLOCK, :] = k_carry
            ops.vv[kv, 0:SWA_BLOCK, :] = v_carry
        yield

        log_a = _log_sigmoid(z) * (1.0 / GLA_GATE_TAU)
        la_hi = log_a.astype(BF16)
        la_lo = (log_a - la_hi.astype(F32)).astype(BF16)
        b2 = _dot(tri_ref[...], jnp.concatenate([la_hi, la_lo], axis=1))
        gqk = proj(hb, W_GQ, 2 * GLA_QK)
        gq = gqk[:, :GLA_QK]
        gk = gqk[:, GLA_QK:]
        yield
        b = b2[:, :GLA_QK] + b2[:, GLA_QK:]
        b_last = jnp.concatenate(
            [jnp.broadcast_to(b[(c + 1) * GLA_CHUNK - 1:(c + 1) * GLA_CHUNK, :], (GLA_CHUNK, GLA_QK))
             for c in range(n_chunks)], axis=0)
        ops.dec[...] = jnp.transpose(jnp.exp(b_last))
        q_d = gq * (GLA_DK ** -0.5) * jnp.exp(b)
        yield
        ops.kd[...] = (gk * jnp.exp(-b)).astype(BF16)
        k_tail = gk * jnp.exp(b_last - b)

        row_i = lax.broadcasted_iota(jnp.int32, (ts, ts), 0)
        col_i = lax.broadcasted_iota(jnp.int32, (ts, ts), 1)
        same_chunk = (row_i // GLA_CHUNK) == (col_i // GLA_CHUNK)
        pair_lane = lax.broadcasted_iota(jnp.int32, (1, LANES), 1)
        tile_reps = ts // LANES
        for hd in range(GLA_HEADS):
            pr = slice((hd // 2) * LANES, (hd // 2 + 1) * LANES)
            in_head = (pair_lane // GLA_DK) == (hd % 2)
            q_sel = jnp.where(in_head, q_d[:, pr], 0.0)
            k_sel = jnp.where(in_head, k_tail[:, pr], 0.0)
            q_dup = q_sel + pltpu.roll(q_sel, GLA_DK, axis=1)
            k_dup = k_sel + pltpu.roll(k_sel, GLA_DK, axis=1)
            ops.qs[hd] = q_sel.astype(BF16)
            ops.qb[hd] = jnp.where(same_chunk, jnp.concatenate([q_dup] * tile_reps, axis=1), 0.0).astype(BF16)
            ops.kb[hd] = jnp.where(same_chunk, jnp.concatenate([k_dup] * tile_reps, axis=1), 0.0).astype(BF16)
            if hd % 2 == 1:
                yield

        for c0 in range(0, GLA_WIDTH, PROJ_COLS):
            ops.gv[:, c0:c0 + PROJ_COLS] = proj(hb, W_GV + c0, PROJ_COLS).astype(BF16)
            yield

        sq_all = proj(hb, W_SQ, SWA_WIDTH)
        for kv in range(SWA_KV_HEADS):
            sq = sq_all[:, kv * MXU_COLS:(kv + 1) * MXU_COLS]
            q_rot = jnp.concatenate([rope(sq[:, j * LANES:(j + 1) * LANES]) for j in range(MXU_COLS // LANES)],
                                    axis=1) * (SWA_HEAD_DIM ** -0.5 * LOG2E)
            for n in range(n_blocks):
                q_t = jnp.transpose(q_rot[n * SWA_BLOCK:(n + 1) * SWA_BLOCK, :])
                ops.qt[n * SWA_KV_HEADS + kv] = jnp.concatenate(
                    [q_t[j * SWA_HEAD_DIM:(j + 1) * SWA_HEAD_DIM, :] for j in range(SWA_GROUP)],
                    axis=1).astype(BF16)
            yield

        for c0 in range(0, GLA_WIDTH, PROJ_COLS):
            ops.gg[:, c0:c0 + PROJ_COLS] = _silu(proj(hb, W_GZ + c0, PROJ_COLS)).astype(BF16)
            yield
        for c0 in range(0, SWA_WIDTH, PROJ_COLS):
            ops.sg[:, c0:c0 + PROJ_COLS] = _silu(proj(hb, W_SZ + c0, PROJ_COLS)).astype(BF16)
            yield

    def gla(ops, cat_scr):
        row_i = lax.broadcasted_iota(jnp.int32, (ts, ts), 0)
        col_i = lax.broadcasted_iota(jnp.int32, (ts, ts), 1)
        causal = ((row_i // GLA_CHUNK) == (col_i // GLA_CHUNK)) & (col_i <= row_i)
        heads = []
        for hd in range(GLA_HEADS):
            pr = slice((hd // 2) * LANES, (hd // 2 + 1) * LANES)
            v_h = ops.gv[:, hd * GLA_DV:(hd + 1) * GLA_DV]
            s = jnp.where(causal, _dot_nt(ops.qs[hd], ops.kd[:, pr]), 0.0).astype(BF16)
            u_all = _dot_tn(ops.kb[hd], v_h)
            heads.append((v_h, s, u_all))
            if hd % 2 == 1:
                yield

        staged = []
        for hd, (v_h, s, u_all) in enumerate(heads):
            st = state_ref[hd]
            entering = []
            for c in range(n_chunks):
                entering.append(st)
                last = c * GLA_CHUNK + GLA_CHUNK - 1
                d_col = ops.dec[hd * GLA_DK:(hd + 1) * GLA_DK, last:last + 1]
                st = st * d_col + u_all[c * GLA_DK:(c + 1) * GLA_DK, :]
            state_ref[hd] = st
            s_stack = jnp.concatenate(entering, axis=0).astype(BF16)
            staged.append((s, v_h, s_stack))
        yield

        outs = []
        for hd, (s, v_h, s_stack) in enumerate(staged):
            outs.append(_dot(s, v_h) + _dot(ops.qb[hd], s_stack))
            if hd % 2 == 1:
                yield

        g_gla = g_gla_ref[...]
        for hd, o_h in enumerate(outs):
            vs = slice(hd * GLA_DV, (hd + 1) * GLA_DV)
            o_ms = jnp.mean(o_h * o_h, axis=-1, keepdims=True)
            o_n = o_h * lax.rsqrt(o_ms + RMS_EPS) * g_gla[:, vs]
            cat_scr[:, vs] = (o_n * ops.gg[:, vs].astype(F32)).astype(BF16)

    def swa(ops, cat_scr, first_tile):
        gq_lanes = SWA_GROUP * SWA_BLOCK
        kj = lax.broadcasted_iota(jnp.int32, (2 * SWA_BLOCK, gq_lanes), 0)
        ql = lax.broadcasted_iota(jnp.int32, (2 * SWA_BLOCK, gq_lanes), 1)
        dist = (ql % SWA_BLOCK) + SWA_BLOCK - kj
        valid = (dist >= 0) & (dist < WINDOW)
        head_of_lane = lax.broadcasted_iota(jnp.int32, (1, gq_lanes), 1) // SWA_BLOCK

        scores = []
        for u, (n, kv) in enumerate(units):
            bs = slice(n * SWA_BLOCK, (n + 2) * SWA_BLOCK)
            scores.append(_dot(ops.kk[kv, bs, :], ops.qt[u]))
            if u % 2 == 1:
                yield

        probs = []
        for (n, kv), s_raw in zip(units, scores):
            if n == 0 and first_tile is not None:
                ok = valid & ((kj >= SWA_BLOCK) | jnp.logical_not(first_tile))
            else:
                ok = valid
            s_t = s_raw + jnp.where(ok, 0.0, NEG_BIG)
            sink = jnp.zeros((1, gq_lanes), F32)
            for j in range(SWA_GROUP):
                sink = jnp.where(head_of_lane == j, sinks_ref[kv * SWA_GROUP + j] * LOG2E, sink)
            m = jnp.maximum(jnp.max(s_t, axis=0, keepdims=True), sink)
            e = jnp.exp2(s_t - m)
            den = jnp.sum(e, axis=0, keepdims=True) + jnp.exp2(sink - m)
            probs.append((e.astype(BF16), 1.0 / den))
            yield

        outs = []
        for (n, kv), (e, inv_den) in zip(units, probs):
            bs = slice(n * SWA_BLOCK, (n + 2) * SWA_BLOCK)
            outs.append(_dot_tn(ops.vv[kv, bs, :], e) * inv_den)
        yield

        for n in range(n_blocks):
            rs = slice(n * SWA_BLOCK, (n + 1) * SWA_BLOCK)
            o_groups = []
            for kv in range(SWA_KV_HEADS):
                o_t = outs[n * SWA_KV_HEADS + kv]
                o_stack = jnp.concatenate([o_t[:, j * SWA_BLOCK:(j + 1) * SWA_BLOCK] for j in range(SWA_GROUP)],
                                          axis=0)
                o_groups.append(jnp.transpose(o_stack))
            o_swa = jnp.concatenate(o_groups, axis=1)
            cat_scr[rs, GLA_WIDTH:] = (o_swa * ops.sg[rs, :].astype(F32)).astype(BF16)

    def finish(cat_scr, rows):
        y = _dot(cat_scr[...], w_out_ref[...])
        yield
        x = x_ref[0, rows, :]
        xo = x + gate * y
        ms_o = jnp.mean(xo * xo, axis=-1, keepdims=True)
        o_ref[0, rows, :] = xo * lax.rsqrt(ms_o + RMS_EPS) * g_fin_ref[...]

    def interleave(primary, filler):
        live = list(primary)
        while live:
            for g in list(live):
                if next(g, StopIteration) is StopIteration:
                    live.remove(g)
                next(filler, None)
        for _ in filler:
            pass

    rows = [slice(k * ts, (k + 1) * ts) for k in range(STEP_TILES)]
    ops = (ops0, ops1)
    cats = (cat0_scr, cat1_scr)

    @pl.when(i == 0)
    def _():
        state_ref[...] = jnp.zeros_like(state_ref)

    @pl.when((i == 0) & (b_idx == 0))
    def _():
        for _ in prepare(x_ref[0, rows[0], :], pos_ref[0, :, rows[0]], norm_modulation(mod_ref), ops0, None, True):
            pass

    for k in range(STEP_TILES):
        cur, nxt = ops[k % 2], ops[(k + 1) % 2]
        primary = [gla(cur, cats[k % 2]), swa(cur, cats[k % 2], (i == 0) if k == 0 else None)]
        if k > 0:
            primary.insert(0, finish(cats[(k - 1) % 2], rows[k - 1]))
        if k + 1 < STEP_TILES:
            filler = prepare(x_ref[0, rows[k + 1], :], pos_ref[0, :, rows[k + 1]], norm_modulation(mod_ref),
                             nxt, cur, False)
        else:
            filler = prepare(xn_ref[0], posn_ref[0], norm_modulation(modn_ref), nxt, cur, last_step)
        interleave(primary, filler)
    for _ in finish(cats[(STEP_TILES - 1) % 2], rows[STEP_TILES - 1]):
        pass


def _chunk_cumsum_matrix(ts):
    r = np.arange(ts)
    same = (r[:, None] // GLA_CHUNK) == (r[None, :] // GLA_CHUNK)
    return jnp.asarray(same & (r[None, :] <= r[:, None]), BF16)


def _rope_inv_freq_column():
    inv = 1.0 / (ROPE_THETA ** (jnp.arange(0, SWA_HEAD_DIM, 2, dtype=F32) / SWA_HEAD_DIM))
    return inv.reshape(SWA_HEAD_DIM // 2, 1)


def kernel(x, c, positions, w_ada, b_ada, g_norm, w_in, w_decay, b_decay, g_gla_head, sinks, w_out, g_final):
    B, S, D = x.shape
    ts = SEQ_TILE
    assert D == D_MODEL and S % (STEP_TILES * ts) == 0 and ts % SWA_BLOCK == 0 and STEP_TILES % 2 == 0
    assert w_ada.shape[0] == 1, "one layer"
    n_tiles = S // ts

    mod = _adaln(c.astype(F32), w_ada[0], b_ada[0][None, :]).reshape(B, 1, 3 * D)

    w_in_b, w_out_b = _prepare_weights(jnp.transpose(w_in[0]), w_out[0])
    w_dec = jnp.pad(w_decay[0].astype(BF16), ((0, LANES - GLA_GATE_RANK), (0, 0)))
    tri = _chunk_cumsum_matrix(ts)
    pos3 = positions.reshape(B, 1, S)

    const2 = lambda b, i, s: (0, 0)
    full = lambda a: pl.BlockSpec(a.shape, const2)
    n_steps = n_tiles // STEP_TILES

    def next_row_tile(b, i):
        wraps = i == n_steps - 1
        at_end = wraps & (b == B - 1)
        row = jnp.where(wraps & jnp.logical_not(at_end), b + 1, b)
        tile = jnp.where(wraps, jnp.where(at_end, n_tiles - 1, 0), STEP_TILES * (i + 1))
        return row, tile

    operands = [
        x, x, pos3, pos3,
        mod, mod,
        g_norm[0][None, :],
        _rope_inv_freq_column(),
        tri,
        w_in_b,
        w_dec, b_decay[0][None, :], g_gla_head[0][None, :],
        w_out_b, g_final[None, :],
    ]
    in_specs = [
        pl.BlockSpec((1, STEP_TILES * ts, D), lambda b, i, s: (b, i, 0)),
        pl.BlockSpec((1, ts, D), lambda b, i, s: next_row_tile(b, i) + (0,)),
        pl.BlockSpec((1, 1, STEP_TILES * ts), lambda b, i, s: (b, 0, i)),
        pl.BlockSpec((1, 1, ts), lambda b, i, s: (next_row_tile(b, i)[0], 0, next_row_tile(b, i)[1])),
        pl.BlockSpec((1, 1, 3 * D), lambda b, i, s: (b, 0, 0)),
        pl.BlockSpec((1, 1, 3 * D), lambda b, i, s: (next_row_tile(b, i)[0], 0, 0)),
    ] + [full(a) for a in operands[6:]]

    operand_scratch = [pltpu.VMEM(shape, dtype) for _, shape, dtype in _OPERAND_BUFFERS]
    grid_spec = pltpu.PrefetchScalarGridSpec(
        num_scalar_prefetch=1,
        grid=(B, n_steps),
        in_specs=in_specs,
        out_specs=pl.BlockSpec((1, STEP_TILES * ts, D), lambda b, i, s: (b, i, 0)),
        scratch_shapes=[
            pltpu.VMEM((GLA_HEADS, GLA_DK, GLA_DV), F32),
            pltpu.VMEM((ts, GLA_WIDTH + SWA_WIDTH), BF16),
            pltpu.VMEM((ts, GLA_WIDTH + SWA_WIDTH), BF16),
        ] + operand_scratch + operand_scratch,
    )
    return pl.pallas_call(
        _layer_body,
        grid_spec=grid_spec,
        out_shape=jax.ShapeDtypeStruct((B, S, D), x.dtype),
        compiler_params=pltpu.CompilerParams(
            dimension_semantics=("arbitrary", "arbitrary"),
            vmem_limit_bytes=VMEM_LIMIT_BYTES,
        ),
        name="hymba_layer",
    )(sinks[0].astype(F32), *operands)
```

```python
import jax
import jax.numpy as jnp
import numpy as np
from jax import lax
from jax.experimental import pallas as pl
from jax.experimental.pallas import tpu as pltpu

D_MODEL = 1024
GLA_HEADS = 4
GLA_DK = 64
GLA_DV = 128
GLA_WIDTH = GLA_HEADS * GLA_DV
GLA_QK = GLA_HEADS * GLA_DK
GLA_GATE_RANK = 16
GLA_GATE_TAU = 16.0
GLA_CHUNK = 64
SWA_Q_HEADS = 8
SWA_KV_HEADS = 2
SWA_GROUP = SWA_Q_HEADS // SWA_KV_HEADS
SWA_HEAD_DIM = 64
SWA_WIDTH = SWA_Q_HEADS * SWA_HEAD_DIM
SWA_KV_WIDTH = SWA_KV_HEADS * SWA_HEAD_DIM
WINDOW = 128
SWA_BLOCK = 128
ROPE_THETA = 10000.0
RMS_EPS = 1e-6

LANES = 128
MXU_COLS = 256
PROJ_COLS = 2 * MXU_COLS
SEQ_TILE = 256
STEP_TILES = 2
LOG2E = 1.4426950408889634
NEG_BIG = -1e30
VMEM_LIMIT_BYTES = 56 * 1024 * 1024

W_GA = 0
W_SK = W_GA + LANES
W_SV = W_SK + SWA_KV_WIDTH
W_GQ = W_SV + SWA_KV_WIDTH
W_GK = W_GQ + GLA_QK
W_GV = W_GK + GLA_QK
W_SQ = W_GV + GLA_WIDTH
W_GZ = W_SQ + SWA_WIDTH
W_SZ = W_GZ + GLA_WIDTH
W_COLS = W_SZ + SWA_WIDTH

SWA_UNITS = (SEQ_TILE // SWA_BLOCK) * SWA_KV_HEADS

BF16 = jnp.bfloat16
F32 = jnp.float32


def _dot(a, b):
    return jnp.dot(a, b, preferred_element_type=F32)


def _dot_nt(a, b):
    return lax.dot_general(a, b, (((1,), (1,)), ((), ())), preferred_element_type=F32)


def _dot_tn(a, b):
    return lax.dot_general(a, b, (((0,), (0,)), ((), ())), preferred_element_type=F32)


def _silu(x):
    return x * (1.0 / (1.0 + jnp.exp2(x * (-LOG2E))))


def _log_sigmoid(z):
    return jnp.minimum(z, 0.0) - jnp.log(1.0 + jnp.exp2(jnp.abs(z) * (-LOG2E)))


def _adaln_body(c_ref, w_ref, b_ref, o_ref):
    a = _silu(c_ref[...]).astype(BF16)
    o_ref[...] = _dot(a, w_ref[...].astype(BF16)) + b_ref[...]


def _adaln(c, w_ada, b_ada):
    rows = c.shape[0]
    n_out = w_ada.shape[1]
    col_block = D_MODEL
    return pl.pallas_call(
        _adaln_body,
        grid=(n_out // col_block,),
        in_specs=[
            pl.BlockSpec((rows, D_MODEL), lambda j: (0, 0)),
            pl.BlockSpec((D_MODEL, col_block), lambda j: (0, j)),
            pl.BlockSpec((1, col_block), lambda j: (0, j)),
        ],
        out_specs=pl.BlockSpec((rows, col_block), lambda j: (0, j)),
        out_shape=jax.ShapeDtypeStruct((rows, n_out), F32),
        name="adaln_mod",
    )(c, w_ada, b_ada)


_SRC_GA = 2 * GLA_QK + GLA_WIDTH
_SRC_GZ = _SRC_GA + GLA_GATE_RANK
_SRC_SQ = _SRC_GZ + GLA_WIDTH
_SRC_SK = _SRC_SQ + SWA_WIDTH
_SRC_SV = _SRC_SK + SWA_KV_WIDTH
_SRC_SZ = _SRC_SV + SWA_KV_WIDTH
_SRC_ROWS = _SRC_SZ + SWA_WIDTH
_REGROUP = (
    (W_SK, _SRC_SK, SWA_KV_WIDTH), (W_SV, _SRC_SV, SWA_KV_WIDTH),
    (W_GQ, 0, GLA_QK), (W_GK, GLA_QK, GLA_QK), (W_GV, 2 * GLA_QK, GLA_WIDTH),
    (W_SQ, _SRC_SQ, SWA_WIDTH), (W_GZ, _SRC_GZ, GLA_WIDTH), (W_SZ, _SRC_SZ, SWA_WIDTH),
)
W_COL_BLOCK = 256


def _prepare_weights_body(wt_ref, w_out_ref, o_ref, w_out_o_ref):
    w_out_o_ref[...] = w_out_ref[...].astype(BF16)
    row = lax.broadcasted_iota(jnp.int32, (LANES, 1), 0)
    ga = jnp.where(row < GLA_GATE_RANK, wt_ref[_SRC_GA:_SRC_GA + LANES, :], 0.0)
    o_ref[:, W_GA:W_GA + LANES] = jnp.transpose(ga).astype(BF16)
    for dst, src, width in _REGROUP:
        o_ref[:, dst:dst + width] = jnp.transpose(wt_ref[src:src + width, :]).astype(BF16)


def _prepare_weights(wt, w_out):
    n, d = wt.shape
    k_out, n_out = w_out.shape
    assert n == _SRC_ROWS and d % W_COL_BLOCK == 0 and k_out % (d // W_COL_BLOCK) == 0
    steps = d // W_COL_BLOCK
    out_rows = k_out // steps
    return pl.pallas_call(
        _prepare_weights_body,
        grid=(steps,),
        in_specs=[pl.BlockSpec((n, W_COL_BLOCK), lambda r: (0, r)),
                  pl.BlockSpec((out_rows, n_out), lambda r: (r, 0))],
        out_specs=[pl.BlockSpec((W_COL_BLOCK, W_COLS), lambda r: (r, 0)),
                   pl.BlockSpec((out_rows, n_out), lambda r: (r, 0))],
        out_shape=[jax.ShapeDtypeStruct((d, W_COLS), BF16), jax.ShapeDtypeStruct((k_out, n_out), BF16)],
        name="prepare_weights",
    )(wt, w_out)


_OPERAND_BUFFERS = (
    ("qs", (GLA_HEADS // 2, SEQ_TILE, LANES), BF16),
    ("qb", (GLA_HEADS, SEQ_TILE, SEQ_TILE), BF16),
    ("kb", (GLA_HEADS, SEQ_TILE, SEQ_TILE), BF16),
    ("kd", (SEQ_TILE, GLA_QK), BF16),
    ("gv", (SEQ_TILE, GLA_WIDTH), BF16),
    ("dec", (GLA_QK, SEQ_TILE), F32),
    ("gg", (SEQ_TILE, GLA_WIDTH), BF16),
    ("sg", (SEQ_TILE, SWA_WIDTH), BF16),
    ("qt", (SWA_UNITS, SWA_HEAD_DIM, SWA_GROUP * SWA_BLOCK), BF16),
    ("kk", (SWA_KV_HEADS, SEQ_TILE + SWA_BLOCK, SWA_HEAD_DIM), BF16),
    ("vv", (SWA_KV_HEADS, SEQ_TILE + SWA_BLOCK, SWA_HEAD_DIM), BF16),
)


class _Operands:
    def __init__(self, refs):
        for (name, _, _), ref in zip(_OPERAND_BUFFERS, refs):
            setattr(self, name, ref)


def _layer_body(sinks_ref, x_ref, xn_ref, pos_ref, posn_ref, mod_ref, modn_ref, gnorm_ref, invf_ref, tri_ref, blk_ref,
                w_in_ref, w_dec_ref, b_dec_ref, g_gla_ref, w_out_ref, g_fin_ref,
                o_ref, state_ref, *operand_refs):
    ts = SEQ_TILE
    b_idx = pl.program_id(0)
    i = pl.program_id(1)
    last_step = i == pl.num_programs(1) - 1
    nbuf = len(_OPERAND_BUFFERS)
    ops0 = _Operands(operand_refs[:nbuf])
    ops1 = _Operands(operand_refs[nbuf:])

    gate = mod_ref[0, :, 2 * D_MODEL:3 * D_MODEL]

    def norm_modulation(m_ref):
        return (gnorm_ref[...] * (1.0 + m_ref[0, :, D_MODEL:2 * D_MODEL]), m_ref[0, :, 0:D_MODEL])

    n_chunks = ts // GLA_CHUNK
    n_blocks = ts // SWA_BLOCK
    units = [(n, kv) for n in range(n_blocks) for kv in range(SWA_KV_HEADS)]

    def proj(hb, c0, width):
        return _dot(hb, w_in_ref[:, c0:c0 + width])

    def prepare(x, pos, modulation, ops, prev_ops, starts_sequence):
        norm_gain, shift = modulation
        ms = jnp.mean(x * x, axis=-1, keepdims=True)
        hb = ((x * lax.rsqrt(ms + RMS_EPS)) * norm_gain + shift).astype(BF16)
        yield

        d0 = proj(hb, W_GA, LANES + 2 * SWA_KV_WIDTH)
        yield

        ang_t = invf_ref[...] * pos.astype(F32)
        reps = LANES // (SWA_HEAD_DIM // 2)
        cos_f = jnp.transpose(jnp.concatenate([jnp.cos(ang_t)] * reps, axis=0))
        sin_f = jnp.transpose(jnp.concatenate([jnp.sin(ang_t)] * reps, axis=0))
        lane = lax.broadcasted_iota(jnp.int32, (1, LANES), 1)
        first_half = (lane % SWA_HEAD_DIM) < (SWA_HEAD_DIM // 2)
        sin_s = jnp.where(first_half, -sin_f, sin_f)

        def rope(tv):
            rot = jnp.where(first_half,
                            pltpu.roll(tv, LANES - SWA_HEAD_DIM // 2, axis=1),
                            pltpu.roll(tv, SWA_HEAD_DIM // 2, axis=1))
            return tv * cos_f + rot * sin_s
        yield

        z = _dot(d0[:, 0:LANES].astype(BF16), w_dec_ref[...]) + b_dec_ref[...]
        k_rot = rope(d0[:, LANES:LANES + SWA_KV_WIDTH]).astype(BF16)
        for kv in range(SWA_KV_HEADS):
            hs = slice(kv * SWA_HEAD_DIM, (kv + 1) * SWA_HEAD_DIM)
            v0 = LANES + SWA_KV_WIDTH + kv * SWA_HEAD_DIM
            ops.kk[kv, SWA_BLOCK:SWA_BLOCK + ts, :] = k_rot[:, hs]
            ops.vv[kv, SWA_BLOCK:SWA_BLOCK + ts, :] = d0[:, v0:v0 + SWA_HEAD_DIM].astype(BF16)
            zeros = jnp.zeros((SWA_BLOCK, SWA_HEAD_DIM), BF16)
            if starts_sequence is True:
                k_carry, v_carry = zeros, zeros
            else:
                k_carry = prev_ops.kk[kv, ts:ts + SWA_BLOCK, :]
                v_carry = prev_ops.vv[kv, ts:ts + SWA_BLOCK, :]
                if starts_sequence is not False:
                    k_carry = jnp.where(starts_sequence, zeros, k_carry)
                    v_carry = jnp.where(starts_sequence, zeros, v_carry)
            ops.kk[kv, 0:SWA_BLOCK, :] = k_carry
            ops.vv[kv, 0:SWA_BLOCK, :] = v_carry
        yield

        log_a = _log_sigmoid(z) * (1.0 / GLA_GATE_TAU)
        la_hi = log_a.astype(BF16)
        la_lo = (log_a - la_hi.astype(F32)).astype(BF16)
        b2 = _dot(tri_ref[...], jnp.concatenate([la_hi, la_lo], axis=1))
        gqk = proj(hb, W_GQ, 2 * GLA_QK)
        gq = gqk[:, :GLA_QK]
        gk = gqk[:, GLA_QK:]
        yield
        b = b2[:, :GLA_QK] + b2[:, GLA_QK:]
        b_last = jnp.concatenate(
            [jnp.broadcast_to(b[(c + 1) * GLA_CHUNK - 1:(c + 1) * GLA_CHUNK, :], (GLA_CHUNK, GLA_QK))
             for c in range(n_chunks)], axis=0)
        ops.dec[...] = jnp.transpose(jnp.exp(b_last))
        q_d = gq * (GLA_DK ** -0.5) * jnp.exp(b)
        yield
        ops.kd[...] = (gk * jnp.exp(-b)).astype(BF16)
        k_tail = gk * jnp.exp(b_last - b)

        blk = blk_ref[...]
        low_half = lax.broadcasted_iota(jnp.int32, (1, LANES), 1) < GLA_DK
        tile_reps = ts // LANES
        for pair in range(GLA_HEADS // 2):
            pr = slice(pair * LANES, (pair + 1) * LANES)
            q_pair, k_pair = q_d[:, pr], k_tail[:, pr]
            q_swap = pltpu.roll(q_pair, GLA_DK, axis=1)
            k_swap = pltpu.roll(k_pair, GLA_DK, axis=1)
            ops.qs[pair] = q_pair.astype(BF16)
            for j in range(2):
                a, b_ = (q_pair, q_swap) if j == 0 else (q_swap, q_pair)
                c, d_ = (k_pair, k_swap) if j == 0 else (k_swap, k_pair)
                q_dup = jnp.where(low_half, a, b_).astype(BF16)
                k_dup = jnp.where(low_half, c, d_).astype(BF16)
                ops.qb[2 * pair + j] = jnp.concatenate([q_dup] * tile_reps, axis=1) * blk
                ops.kb[2 * pair + j] = jnp.concatenate([k_dup] * tile_reps, axis=1) * blk
            yield

        for c0 in range(0, GLA_WIDTH, PROJ_COLS):
            ops.gv[:, c0:c0 + PROJ_COLS] = proj(hb, W_GV + c0, PROJ_COLS).astype(BF16)
            yield

        sq_all = proj(hb, W_SQ, SWA_WIDTH)
        for kv in range(SWA_KV_HEADS):
            sq = sq_all[:, kv * MXU_COLS:(kv + 1) * MXU_COLS]
            q_rot = jnp.concatenate([rope(sq[:, j * LANES:(j + 1) * LANES]) for j in range(MXU_COLS // LANES)],
                                    axis=1) * (SWA_HEAD_DIM ** -0.5 * LOG2E)
            for n in range(n_blocks):
                q_t = jnp.transpose(q_rot[n * SWA_BLOCK:(n + 1) * SWA_BLOCK, :])
                ops.qt[n * SWA_KV_HEADS + kv] = jnp.concatenate(
                    [q_t[j * SWA_HEAD_DIM:(j + 1) * SWA_HEAD_DIM, :] for j in range(SWA_GROUP)],
                    axis=1).astype(BF16)
            yield

        for c0 in range(0, GLA_WIDTH, PROJ_COLS):
            ops.gg[:, c0:c0 + PROJ_COLS] = _silu(proj(hb, W_GZ + c0, PROJ_COLS)).astype(BF16)
            yield
        for c0 in range(0, SWA_WIDTH, PROJ_COLS):
            ops.sg[:, c0:c0 + PROJ_COLS] = _silu(proj(hb, W_SZ + c0, PROJ_COLS)).astype(BF16)
            yield

    def gla(ops, mixed):
        row_i = lax.broadcasted_iota(jnp.int32, (ts, ts), 0)
        col_i = lax.broadcasted_iota(jnp.int32, (ts, ts), 1)
        causal = ((row_i // GLA_CHUNK) == (col_i // GLA_CHUNK)) & (col_i <= row_i)
        pair_lane = lax.broadcasted_iota(jnp.int32, (1, LANES), 1)
        heads = []
        for hd in range(GLA_HEADS):
            pr = slice((hd // 2) * LANES, (hd // 2 + 1) * LANES)
            v_h = ops.gv[:, hd * GLA_DV:(hd + 1) * GLA_DV]
            k_h = jnp.where((pair_lane // GLA_DK) == (hd % 2), ops.kd[:, pr], jnp.zeros((), BF16))
            s = jnp.where(causal, _dot_nt(ops.qs[hd // 2], k_h), 0.0).astype(BF16)
            u_all = _dot_tn(ops.kb[hd], v_h)
            heads.append((v_h, s, u_all))
            if hd % 2 == 1:
                yield

        staged = []
        for hd, (v_h, s, u_all) in enumerate(heads):
            st = state_ref[hd]
            entering = []
            for c in range(n_chunks):
                entering.append(st)
                last = c * GLA_CHUNK + GLA_CHUNK - 1
                d_col = ops.dec[hd * GLA_DK:(hd + 1) * GLA_DK, last:last + 1]
                st = st * d_col + u_all[c * GLA_DK:(c + 1) * GLA_DK, :]
            state_ref[hd] = st
            s_stack = jnp.concatenate(entering, axis=0).astype(BF16)
            staged.append((s, v_h, s_stack))
        yield

        outs = []
        for hd, (s, v_h, s_stack) in enumerate(staged):
            outs.append(_dot(s, v_h) + _dot(ops.qb[hd], s_stack))
            if hd % 2 == 1:
                yield

        g_gla = g_gla_ref[...]
        for hd, o_h in enumerate(outs):
            vs = slice(hd * GLA_DV, (hd + 1) * GLA_DV)
            o_ms = jnp.mean(o_h * o_h, axis=-1, keepdims=True)
            o_n = o_h * lax.rsqrt(o_ms + RMS_EPS) * g_gla[:, vs]
            mixed.append((o_n * ops.gg[:, vs].astype(F32)).astype(BF16))

    def swa(ops, mixed, first_tile):
        gq_lanes = SWA_GROUP * SWA_BLOCK
        kj = lax.broadcasted_iota(jnp.int32, (2 * SWA_BLOCK, gq_lanes), 0)
        ql = lax.broadcasted_iota(jnp.int32, (2 * SWA_BLOCK, gq_lanes), 1)
        dist = (ql % SWA_BLOCK) + SWA_BLOCK - kj
        valid = (dist >= 0) & (dist < WINDOW)
        head_of_lane = lax.broadcasted_iota(jnp.int32, (1, gq_lanes), 1) // SWA_BLOCK

        scores = []
        for u, (n, kv) in enumerate(units):
            bs = slice(n * SWA_BLOCK, (n + 2) * SWA_BLOCK)
            scores.append(_dot(ops.kk[kv, bs, :], ops.qt[u]))
            if u % 2 == 1:
                yield

        probs = []
        for (n, kv), s_raw in zip(units, scores):
            if n == 0 and first_tile is not None:
                ok = valid & ((kj >= SWA_BLOCK) | jnp.logical_not(first_tile))
            else:
                ok = valid
            s_t = s_raw + jnp.where(ok, 0.0, NEG_BIG)
            sink = jnp.zeros((1, gq_lanes), F32)
            for j in range(SWA_GROUP):
                sink = jnp.where(head_of_lane == j, sinks_ref[kv * SWA_GROUP + j] * LOG2E, sink)
            m = jnp.maximum(jnp.max(s_t, axis=0, keepdims=True), sink)
            e = jnp.exp2(s_t - m)
            den = jnp.sum(e, axis=0, keepdims=True) + jnp.exp2(sink - m)
            probs.append((e.astype(BF16), 1.0 / den))
            yield

        outs = []
        for (n, kv), (e, inv_den) in zip(units, probs):
            bs = slice(n * SWA_BLOCK, (n + 2) * SWA_BLOCK)
            outs.append(_dot_tn(ops.vv[kv, bs, :], e) * inv_den)
        yield

        for n in range(n_blocks):
            rs = slice(n * SWA_BLOCK, (n + 1) * SWA_BLOCK)
            o_groups = []
            for kv in range(SWA_KV_HEADS):
                o_t = outs[n * SWA_KV_HEADS + kv]
                o_stack = jnp.concatenate([o_t[:, j * SWA_BLOCK:(j + 1) * SWA_BLOCK] for j in range(SWA_GROUP)],
                                          axis=0)
                o_groups.append(jnp.transpose(o_stack))
            o_swa = jnp.concatenate(o_groups, axis=1)
            mixed.append((o_swa * ops.sg[rs, :].astype(F32)).astype(BF16))

    def finish(gla_heads, swa_blocks, rows):
        cat = jnp.concatenate([jnp.concatenate(gla_heads, axis=1), jnp.concatenate(swa_blocks, axis=0)], axis=1)
        y = _dot(cat, w_out_ref[...])
        yield
        x = x_ref[0, rows, :]
        xo = x + gate * y
        ms_o = jnp.mean(xo * xo, axis=-1, keepdims=True)
        o_ref[0, rows, :] = xo * lax.rsqrt(ms_o + RMS_EPS) * g_fin_ref[...]

    def interleave(primary, filler):
        live = list(primary)
        while live:
            for g in list(live):
                if next(g, StopIteration) is StopIteration:
                    live.remove(g)
                next(filler, None)
        for _ in filler:
            pass

    rows = [slice(k * ts, (k + 1) * ts) for k in range(STEP_TILES)]
    ops = (ops0, ops1)
    mixed = [([], []) for _ in range(STEP_TILES)]

    @pl.when(i == 0)
    def _():
        state_ref[...] = jnp.zeros_like(state_ref)

    @pl.when((i == 0) & (b_idx == 0))
    def _():
        for _ in prepare(x_ref[0, rows[0], :], pos_ref[0, :, rows[0]], norm_modulation(mod_ref), ops0, None, True):
            pass

    for k in range(STEP_TILES):
        cur, nxt = ops[k % 2], ops[(k + 1) % 2]
        primary = [gla(cur, mixed[k][0]), swa(cur, mixed[k][1], (i == 0) if k == 0 else None)]
        if k > 0:
            primary.insert(0, finish(*mixed[k - 1], rows[k - 1]))
        if k + 1 < STEP_TILES:
            filler = prepare(x_ref[0, rows[k + 1], :], pos_ref[0, :, rows[k + 1]], norm_modulation(mod_ref),
                             nxt, cur, False)
        else:
            filler = prepare(xn_ref[0], posn_ref[0], norm_modulation(modn_ref), nxt, cur, last_step)
        interleave(primary, filler)
    for _ in finish(*mixed[STEP_TILES - 1], rows[STEP_TILES - 1]):
        pass


def _chunk_matrices(ts):
    r = np.arange(ts)
    same = (r[:, None] // GLA_CHUNK) == (r[None, :] // GLA_CHUNK)
    return jnp.asarray(same & (r[None, :] <= r[:, None]), BF16), jnp.asarray(same, BF16)


def _rope_inv_freq_column():
    inv = 1.0 / (ROPE_THETA ** (jnp.arange(0, SWA_HEAD_DIM, 2, dtype=F32) / SWA_HEAD_DIM))
    return inv.reshape(SWA_HEAD_DIM // 2, 1)


def kernel(x, c, positions, w_ada, b_ada, g_norm, w_in, w_decay, b_decay, g_gla_head, sinks, w_out, g_final):
    B, S, D = x.shape
    ts = SEQ_TILE
    assert D == D_MODEL and S % (STEP_TILES * ts) == 0 and ts % SWA_BLOCK == 0 and STEP_TILES % 2 == 0
    assert w_ada.shape[0] == 1, "one layer"
    n_tiles = S // ts

    mod = _adaln(c.astype(F32), w_ada[0], b_ada[0][None, :]).reshape(B, 1, 3 * D)

    w_in_b, w_out_b = _prepare_weights(jnp.transpose(w_in[0]), w_out[0])
    w_dec = jnp.pad(w_decay[0].astype(BF16), ((0, LANES - GLA_GATE_RANK), (0, 0)))
    tri, blk = _chunk_matrices(ts)
    pos3 = positions.reshape(B, 1, S)

    const2 = lambda b, i, s: (0, 0)
    full = lambda a: pl.BlockSpec(a.shape, const2)
    n_steps = n_tiles // STEP_TILES

    def next_row_tile(b, i):
        wraps = i == n_steps - 1
        at_end = wraps & (b == B - 1)
        row = jnp.where(wraps & jnp.logical_not(at_end), b + 1, b)
        tile = jnp.where(wraps, jnp.where(at_end, n_tiles - 1, 0), STEP_TILES * (i + 1))
        return row, tile

    operands = [
        x, x, pos3, pos3,
        mod, mod,
        g_norm[0][None, :],
        _rope_inv_freq_column(),
        tri, blk,
        w_in_b,
        w_dec, b_decay[0][None, :], g_gla_head[0][None, :],
        w_out_b, g_final[None, :],
    ]
    in_specs = [
        pl.BlockSpec((1, STEP_TILES * ts, D), lambda b, i, s: (b, i, 0)),
        pl.BlockSpec((1, ts, D), lambda b, i, s: next_row_tile(b, i) + (0,)),
        pl.BlockSpec((1, 1, STEP_TILES * ts), lambda b, i, s: (b, 0, i)),
        pl.BlockSpec((1, 1, ts), lambda b, i, s: (next_row_tile(b, i)[0], 0, next_row_tile(b, i)[1])),
        pl.BlockSpec((1, 1, 3 * D), lambda b, i, s: (b, 0, 0)),
        pl.BlockSpec((1, 1, 3 * D), lambda b, i, s: (next_row_tile(b, i)[0], 0, 0)),
    ] + [full(a) for a in operands[6:]]

    operand_scratch = [pltpu.VMEM(shape, dtype) for _, shape, dtype in _OPERAND_BUFFERS]
    grid_spec = pltpu.PrefetchScalarGridSpec(
        num_scalar_prefetch=1,
        grid=(B, n_steps),
        in_specs=in_specs,
        out_specs=pl.BlockSpec((1, STEP_TILES * ts, D), lambda b, i, s: (b, i, 0)),
        scratch_shapes=[
            pltpu.VMEM((GLA_HEADS, GLA_DK, GLA_DV), F32),
        ] + operand_scratch + operand_scratch,
    )
    return pl.pallas_call(
        _layer_body,
        grid_spec=grid_spec,
        out_shape=jax.ShapeDtypeStruct((B, S, D), x.dtype),
        compiler_params=pltpu.CompilerParams(
            dimension_semantics=("arbitrary", "arbitrary"),
            vmem_limit_bytes=VMEM_LIMIT_BYTES,
        ),
        name="hymba_layer",
    )(sinks[0].astype(F32), *operands)
```

```python
import jax
import jax.numpy as jnp
import numpy as np
from jax import lax
from jax.experimental import pallas as pl
from jax.experimental.pallas import tpu as pltpu

D_MODEL = 1024
GLA_HEADS = 4
GLA_DK = 64
GLA_DV = 128
GLA_WIDTH = GLA_HEADS * GLA_DV
GLA_QK = GLA_HEADS * GLA_DK
GLA_GATE_RANK = 16
GLA_GATE_TAU = 16.0
GLA_CHUNK = 64
SWA_Q_HEADS = 8
SWA_KV_HEADS = 2
SWA_GROUP = SWA_Q_HEADS // SWA_KV_HEADS
SWA_HEAD_DIM = 64
SWA_WIDTH = SWA_Q_HEADS * SWA_HEAD_DIM
SWA_KV_WIDTH = SWA_KV_HEADS * SWA_HEAD_DIM
WINDOW = 128
SWA_BLOCK = 128
ROPE_THETA = 10000.0
RMS_EPS = 1e-6

LANES = 128
MXU_COLS = 256
PROJ_COLS = 2 * MXU_COLS
SEQ_TILE = 256
STEP_TILES = 2
LOG2E = 1.4426950408889634
NEG_BIG = -1e30
VMEM_LIMIT_BYTES = 56 * 1024 * 1024

W_GA = 0
W_SK = W_GA + LANES
W_SV = W_SK + SWA_KV_WIDTH
W_GQ = W_SV + SWA_KV_WIDTH
W_GK = W_GQ + GLA_QK
W_GV = W_GK + GLA_QK
W_SQ = W_GV + GLA_WIDTH
W_GZ = W_SQ + SWA_WIDTH
W_SZ = W_GZ + GLA_WIDTH
W_COLS = W_SZ + SWA_WIDTH

SWA_UNITS = (SEQ_TILE // SWA_BLOCK) * SWA_KV_HEADS

BF16 = jnp.bfloat16
F32 = jnp.float32


def _dot(a, b):
    return jnp.dot(a, b, preferred_element_type=F32)


def _dot_nt(a, b):
    return lax.dot_general(a, b, (((1,), (1,)), ((), ())), preferred_element_type=F32)


def _dot_tn(a, b):
    return lax.dot_general(a, b, (((0,), (0,)), ((), ())), preferred_element_type=F32)


def _silu(x):
    return x * (1.0 / (1.0 + jnp.exp2(x * (-LOG2E))))


def _log_sigmoid(z):
    return jnp.minimum(z, 0.0) - jnp.log(1.0 + jnp.exp2(jnp.abs(z) * (-LOG2E)))


def _adaln_body(c_ref, w_ref, b_ref, o_ref):
    a = _silu(c_ref[...]).astype(BF16)
    o_ref[...] = _dot(a, w_ref[...].astype(BF16)) + b_ref[...]


def _adaln(c, w_ada, b_ada):
    rows = c.shape[0]
    n_out = w_ada.shape[1]
    col_block = D_MODEL
    return pl.pallas_call(
        _adaln_body,
        grid=(n_out // col_block,),
        in_specs=[
            pl.BlockSpec((rows, D_MODEL), lambda j: (0, 0)),
            pl.BlockSpec((D_MODEL, col_block), lambda j: (0, j)),
            pl.BlockSpec((1, col_block), lambda j: (0, j)),
        ],
        out_specs=pl.BlockSpec((rows, col_block), lambda j: (0, j)),
        out_shape=jax.ShapeDtypeStruct((rows, n_out), F32),
        name="adaln_mod",
    )(c, w_ada, b_ada)


_SRC_GA = 2 * GLA_QK + GLA_WIDTH
_SRC_GZ = _SRC_GA + GLA_GATE_RANK
_SRC_SQ = _SRC_GZ + GLA_WIDTH
_SRC_SK = _SRC_SQ + SWA_WIDTH
_SRC_SV = _SRC_SK + SWA_KV_WIDTH
_SRC_SZ = _SRC_SV + SWA_KV_WIDTH
_SRC_ROWS = _SRC_SZ + SWA_WIDTH
_REGROUP = (
    (W_SK, _SRC_SK, SWA_KV_WIDTH), (W_SV, _SRC_SV, SWA_KV_WIDTH),
    (W_GQ, 0, GLA_QK), (W_GK, GLA_QK, GLA_QK), (W_GV, 2 * GLA_QK, GLA_WIDTH),
    (W_SQ, _SRC_SQ, SWA_WIDTH), (W_GZ, _SRC_GZ, GLA_WIDTH), (W_SZ, _SRC_SZ, SWA_WIDTH),
)
W_COL_BLOCK = 256
W_OUT_ROW_PAD = LANES


def _prepare_weights_body(wt_ref, w_out_ref, o_ref, w_out_o_ref):
    n_out = w_out_ref.shape[1]
    w_out_o_ref[:, 0:n_out] = w_out_ref[...].astype(BF16)
    w_out_o_ref[:, n_out:] = jnp.zeros((w_out_o_ref.shape[0], w_out_o_ref.shape[1] - n_out), BF16)
    row = lax.broadcasted_iota(jnp.int32, (LANES, 1), 0)
    ga = jnp.where(row < GLA_GATE_RANK, wt_ref[_SRC_GA:_SRC_GA + LANES, :], 0.0)
    o_ref[:, W_GA:W_GA + LANES] = jnp.transpose(ga).astype(BF16)
    for dst, src, width in _REGROUP:
        o_ref[:, dst:dst + width] = jnp.transpose(wt_ref[src:src + width, :]).astype(BF16)


def _prepare_weights(wt, w_out):
    n, d = wt.shape
    k_out, n_out = w_out.shape
    assert n == _SRC_ROWS and d % W_COL_BLOCK == 0 and k_out % (d // W_COL_BLOCK) == 0
    steps = d // W_COL_BLOCK
    out_rows = k_out // steps
    return pl.pallas_call(
        _prepare_weights_body,
        grid=(steps,),
        in_specs=[pl.BlockSpec((n, W_COL_BLOCK), lambda r: (0, r)),
                  pl.BlockSpec((out_rows, n_out), lambda r: (r, 0))],
        out_specs=[pl.BlockSpec((W_COL_BLOCK, W_COLS), lambda r: (r, 0)),
                   pl.BlockSpec((out_rows, n_out + W_OUT_ROW_PAD), lambda r: (r, 0))],
        out_shape=[jax.ShapeDtypeStruct((d, W_COLS), BF16),
                   jax.ShapeDtypeStruct((k_out, n_out + W_OUT_ROW_PAD), BF16)],
        name="prepare_weights",
    )(wt, w_out)


_OPERAND_BUFFERS = (
    ("qs", (GLA_HEADS // 2, SEQ_TILE, LANES), BF16),
    ("qb", (GLA_HEADS, SEQ_TILE, SEQ_TILE), BF16),
    ("kb", (GLA_HEADS, SEQ_TILE, SEQ_TILE), BF16),
    ("kd", (SEQ_TILE, GLA_QK), BF16),
    ("gv", (SEQ_TILE, GLA_WIDTH), BF16),
    ("dec", (GLA_QK, SEQ_TILE), F32),
    ("gg", (SEQ_TILE, GLA_WIDTH), BF16),
    ("sg", (SEQ_TILE, SWA_WIDTH), BF16),
    ("qt", (SWA_UNITS, SWA_HEAD_DIM, SWA_GROUP * SWA_BLOCK), BF16),
    ("kk", (SWA_KV_HEADS, SEQ_TILE + SWA_BLOCK, SWA_HEAD_DIM), BF16),
    ("vv", (SWA_KV_HEADS, SEQ_TILE + SWA_BLOCK, SWA_HEAD_DIM), BF16),
)


class _Operands:
    def __init__(self, refs):
        for (name, _, _), ref in zip(_OPERAND_BUFFERS, refs):
            setattr(self, name, ref)


def _layer_body(sinks_ref, x_ref, xn_ref, pos_ref, posn_ref, mod_ref, modn_ref, gnorm_ref, invf_ref, tri_ref, blk_ref,
                w_in_ref, w_dec_ref, b_dec_ref, g_gla_ref, w_out_ref, g_fin_ref,
                o_ref, state_ref, *operand_refs):
    ts = SEQ_TILE
    b_idx = pl.program_id(0)
    i = pl.program_id(1)
    last_step = i == pl.num_programs(1) - 1
    nbuf = len(_OPERAND_BUFFERS)
    ops0 = _Operands(operand_refs[:nbuf])
    ops1 = _Operands(operand_refs[nbuf:])

    gate = mod_ref[0, :, 2 * D_MODEL:3 * D_MODEL]

    def norm_modulation(m_ref):
        return (gnorm_ref[...] * (1.0 + m_ref[0, :, D_MODEL:2 * D_MODEL]), m_ref[0, :, 0:D_MODEL])

    n_chunks = ts // GLA_CHUNK
    n_blocks = ts // SWA_BLOCK
    units = [(n, kv) for n in range(n_blocks) for kv in range(SWA_KV_HEADS)]

    def proj(hb, c0, width):
        return _dot(hb, w_in_ref[:, c0:c0 + width])

    def prepare(x, pos, modulation, ops, prev_ops, starts_sequence):
        norm_gain, shift = modulation
        ms = jnp.mean(x * x, axis=-1, keepdims=True)
        hb = ((x * lax.rsqrt(ms + RMS_EPS)) * norm_gain + shift).astype(BF16)
        yield

        d0 = proj(hb, W_GA, LANES + 2 * SWA_KV_WIDTH)
        yield

        ang_t = invf_ref[...] * pos.astype(F32)
        reps = LANES // (SWA_HEAD_DIM // 2)
        cos_f = jnp.transpose(jnp.concatenate([jnp.cos(ang_t)] * reps, axis=0))
        sin_f = jnp.transpose(jnp.concatenate([jnp.sin(ang_t)] * reps, axis=0))
        lane = lax.broadcasted_iota(jnp.int32, (1, LANES), 1)
        first_half = (lane % SWA_HEAD_DIM) < (SWA_HEAD_DIM // 2)
        sin_s = jnp.where(first_half, -sin_f, sin_f)

        def rope(tv):
            rot = jnp.where(first_half,
                            pltpu.roll(tv, LANES - SWA_HEAD_DIM // 2, axis=1),
                            pltpu.roll(tv, SWA_HEAD_DIM // 2, axis=1))
            return tv * cos_f + rot * sin_s
        yield

        z = _dot(d0[:, 0:LANES].astype(BF16), w_dec_ref[...]) + b_dec_ref[...]
        k_rot = rope(d0[:, LANES:LANES + SWA_KV_WIDTH]).astype(BF16)
        for kv in range(SWA_KV_HEADS):
            hs = slice(kv * SWA_HEAD_DIM, (kv + 1) * SWA_HEAD_DIM)
            v0 = LANES + SWA_KV_WIDTH + kv * SWA_HEAD_DIM
            ops.kk[kv, SWA_BLOCK:SWA_BLOCK + ts, :] = k_rot[:, hs]
            ops.vv[kv, SWA_BLOCK:SWA_BLOCK + ts, :] = d0[:, v0:v0 + SWA_HEAD_DIM].astype(BF16)
            zeros = jnp.zeros((SWA_BLOCK, SWA_HEAD_DIM), BF16)
            if starts_sequence is True:
                k_carry, v_carry = zeros, zeros
            else:
                k_carry = prev_ops.kk[kv, ts:ts + SWA_BLOCK, :]
                v_carry = prev_ops.vv[kv, ts:ts + SWA_BLOCK, :]
                if starts_sequence is not False:
                    k_carry = jnp.where(starts_sequence, zeros, k_carry)
                    v_carry = jnp.where(starts_sequence, zeros, v_carry)
            ops.kk[kv, 0:SWA_BLOCK, :] = k_carry
            ops.vv[kv, 0:SWA_BLOCK, :] = v_carry
        yield

        log_a = _log_sigmoid(z) * (1.0 / GLA_GATE_TAU)
        la_hi = log_a.astype(BF16)
        la_lo = (log_a - la_hi.astype(F32)).astype(BF16)
        b2 = _dot(tri_ref[...], jnp.concatenate([la_hi, la_lo], axis=1))
        gqk = proj(hb, W_GQ, 2 * GLA_QK)
        gq = gqk[:, :GLA_QK]
        gk = gqk[:, GLA_QK:]
        yield
        b = b2[:, :GLA_QK] + b2[:, GLA_QK:]
        b_last = jnp.concatenate(
            [jnp.broadcast_to(b[(c + 1) * GLA_CHUNK - 1:(c + 1) * GLA_CHUNK, :], (GLA_CHUNK, GLA_QK))
             for c in range(n_chunks)], axis=0)
        ops.dec[...] = jnp.transpose(jnp.exp(b_last))
        q_d = gq * (GLA_DK ** -0.5) * jnp.exp(b)
        yield
        ops.kd[...] = (gk * jnp.exp(-b)).astype(BF16)
        k_tail = gk * jnp.exp(b_last - b)

        blk = blk_ref[...]
        low_half = lax.broadcasted_iota(jnp.int32, (1, LANES), 1) < GLA_DK
        tile_reps = ts // LANES
        for pair in range(GLA_HEADS // 2):
            pr = slice(pair * LANES, (pair + 1) * LANES)
            q_pair, k_pair = q_d[:, pr], k_tail[:, pr]
            q_swap = pltpu.roll(q_pair, GLA_DK, axis=1)
            k_swap = pltpu.roll(k_pair, GLA_DK, axis=1)
            ops.qs[pair] = q_pair.astype(BF16)
            for j in range(2):
                a, b_ = (q_pair, q_swap) if j == 0 else (q_swap, q_pair)
                c, d_ = (k_pair, k_swap) if j == 0 else (k_swap, k_pair)
                q_dup = jnp.where(low_half, a, b_).astype(BF16)
                k_dup = jnp.where(low_half, c, d_).astype(BF16)
                ops.qb[2 * pair + j] = jnp.concatenate([q_dup] * tile_reps, axis=1) * blk
                ops.kb[2 * pair + j] = jnp.concatenate([k_dup] * tile_reps, axis=1) * blk
            yield

        for c0 in range(0, GLA_WIDTH, PROJ_COLS):
            ops.gv[:, c0:c0 + PROJ_COLS] = proj(hb, W_GV + c0, PROJ_COLS).astype(BF16)
            yield

        sq_all = proj(hb, W_SQ, SWA_WIDTH)
        for kv in range(SWA_KV_HEADS):
            sq = sq_all[:, kv * MXU_COLS:(kv + 1) * MXU_COLS]
            q_rot = jnp.concatenate([rope(sq[:, j * LANES:(j + 1) * LANES]) for j in range(MXU_COLS // LANES)],
                                    axis=1) * (SWA_HEAD_DIM ** -0.5 * LOG2E)
            for n in range(n_blocks):
                q_t = jnp.transpose(q_rot[n * SWA_BLOCK:(n + 1) * SWA_BLOCK, :])
                ops.qt[n * SWA_KV_HEADS + kv] = jnp.concatenate(
                    [q_t[j * SWA_HEAD_DIM:(j + 1) * SWA_HEAD_DIM, :] for j in range(SWA_GROUP)],
                    axis=1).astype(BF16)
            yield

        for c0 in range(0, GLA_WIDTH, PROJ_COLS):
            ops.gg[:, c0:c0 + PROJ_COLS] = _silu(proj(hb, W_GZ + c0, PROJ_COLS)).astype(BF16)
            yield
        for c0 in range(0, SWA_WIDTH, PROJ_COLS):
            ops.sg[:, c0:c0 + PROJ_COLS] = _silu(proj(hb, W_SZ + c0, PROJ_COLS)).astype(BF16)
            yield

    def gla(ops, mixed):
        row_i = lax.broadcasted_iota(jnp.int32, (ts, ts), 0)
        col_i = lax.broadcasted_iota(jnp.int32, (ts, ts), 1)
        causal = ((row_i // GLA_CHUNK) == (col_i // GLA_CHUNK)) & (col_i <= row_i)
        pair_lane = lax.broadcasted_iota(jnp.int32, (1, LANES), 1)
        heads = []
        for hd in range(GLA_HEADS):
            pr = slice((hd // 2) * LANES, (hd // 2 + 1) * LANES)
            v_h = ops.gv[:, hd * GLA_DV:(hd + 1) * GLA_DV]
            k_h = jnp.where((pair_lane // GLA_DK) == (hd % 2), ops.kd[:, pr], jnp.zeros((), BF16))
            s = jnp.where(causal, _dot_nt(ops.qs[hd // 2], k_h), 0.0).astype(BF16)
            u_all = _dot_tn(ops.kb[hd], v_h)
            heads.append((v_h, s, u_all))
            if hd % 2 == 1:
                yield

        staged = []
        for hd, (v_h, s, u_all) in enumerate(heads):
            st = state_ref[hd]
            entering = []
            for c in range(n_chunks):
                entering.append(st)
                last = c * GLA_CHUNK + GLA_CHUNK - 1
                d_col = ops.dec[hd * GLA_DK:(hd + 1) * GLA_DK, last:last + 1]
                st = st * d_col + u_all[c * GLA_DK:(c + 1) * GLA_DK, :]
            state_ref[hd] = st
            s_stack = jnp.concatenate(entering, axis=0).astype(BF16)
            staged.append((s, v_h, s_stack))
        yield

        outs = []
        for hd, (s, v_h, s_stack) in enumerate(staged):
            outs.append(_dot(s, v_h) + _dot(ops.qb[hd], s_stack))
            if hd % 2 == 1:
                yield

        g_gla = g_gla_ref[...]
        for hd, o_h in enumerate(outs):
            vs = slice(hd * GLA_DV, (hd + 1) * GLA_DV)
            o_ms = jnp.mean(o_h * o_h, axis=-1, keepdims=True)
            o_n = o_h * lax.rsqrt(o_ms + RMS_EPS) * g_gla[:, vs]
            mixed.append((o_n * ops.gg[:, vs].astype(F32)).astype(BF16))

    def swa(ops, mixed, first_tile):
        gq_lanes = SWA_GROUP * SWA_BLOCK
        kj = lax.broadcasted_iota(jnp.int32, (2 * SWA_BLOCK, gq_lanes), 0)
        ql = lax.broadcasted_iota(jnp.int32, (2 * SWA_BLOCK, gq_lanes), 1)
        dist = (ql % SWA_BLOCK) + SWA_BLOCK - kj
        valid = (dist >= 0) & (dist < WINDOW)
        head_of_lane = lax.broadcasted_iota(jnp.int32, (1, gq_lanes), 1) // SWA_BLOCK

        scores = []
        for u, (n, kv) in enumerate(units):
            bs = slice(n * SWA_BLOCK, (n + 2) * SWA_BLOCK)
            scores.append(_dot(ops.kk[kv, bs, :], ops.qt[u]))
            if u % 2 == 1:
                yield

        probs = []
        for (n, kv), s_raw in zip(units, scores):
            if n == 0 and first_tile is not None:
                ok = valid & ((kj >= SWA_BLOCK) | jnp.logical_not(first_tile))
            else:
                ok = valid
            s_t = s_raw + jnp.where(ok, 0.0, NEG_BIG)
            sink = jnp.zeros((1, gq_lanes), F32)
            for j in range(SWA_GROUP):
                sink = jnp.where(head_of_lane == j, sinks_ref[kv * SWA_GROUP + j] * LOG2E, sink)
            m = jnp.maximum(jnp.max(s_t, axis=0, keepdims=True), sink)
            e = jnp.exp2(s_t - m)
            den = jnp.sum(e, axis=0, keepdims=True) + jnp.exp2(sink - m)
            probs.append((e.astype(BF16), 1.0 / den))
            yield

        outs = []
        for (n, kv), (e, inv_den) in zip(units, probs):
            bs = slice(n * SWA_BLOCK, (n + 2) * SWA_BLOCK)
            outs.append(_dot_tn(ops.vv[kv, bs, :], e) * inv_den)
        yield

        for n in range(n_blocks):
            rs = slice(n * SWA_BLOCK, (n + 1) * SWA_BLOCK)
            o_groups = []
            for kv in range(SWA_KV_HEADS):
                o_t = outs[n * SWA_KV_HEADS + kv]
                o_stack = jnp.concatenate([o_t[:, j * SWA_BLOCK:(j + 1) * SWA_BLOCK] for j in range(SWA_GROUP)],
                                          axis=0)
                o_groups.append(jnp.transpose(o_stack))
            o_swa = jnp.concatenate(o_groups, axis=1)
            mixed.append((o_swa * ops.sg[rs, :].astype(F32)).astype(BF16))

    def finish(gla_heads, swa_blocks, rows):
        cat = jnp.concatenate([jnp.concatenate(gla_heads, axis=1), jnp.concatenate(swa_blocks, axis=0)], axis=1)
        y = _dot(cat, w_out_ref[:, 0:D_MODEL])
        yield
        x = x_ref[0, rows, :]
        xo = x + gate * y
        ms_o = jnp.mean(xo * xo, axis=-1, keepdims=True)
        o_ref[0, rows, :] = xo * lax.rsqrt(ms_o + RMS_EPS) * g_fin_ref[...]

    def interleave(primary, filler):
        live = list(primary)
        while live:
            for g in list(live):
                if next(g, StopIteration) is StopIteration:
                    live.remove(g)
                next(filler, None)
        for _ in filler:
            pass

    rows = [slice(k * ts, (k + 1) * ts) for k in range(STEP_TILES)]
    ops = (ops0, ops1)
    mixed = [([], []) for _ in range(STEP_TILES)]

    @pl.when(i == 0)
    def _():
        state_ref[...] = jnp.zeros_like(state_ref)

    @pl.when((i == 0) & (b_idx == 0))
    def _():
        for _ in prepare(x_ref[0, rows[0], :], pos_ref[0, :, rows[0]], norm_modulation(mod_ref), ops0, None, True):
            pass

    for k in range(STEP_TILES):
        cur, nxt = ops[k % 2], ops[(k + 1) % 2]
        primary = [gla(cur, mixed[k][0]), swa(cur, mixed[k][1], (i == 0) if k == 0 else None)]
        if k > 0:
            primary.insert(0, finish(*mixed[k - 1], rows[k - 1]))
        if k + 1 < STEP_TILES:
            filler = prepare(x_ref[0, rows[k + 1], :], pos_ref[0, :, rows[k + 1]], norm_modulation(mod_ref),
                             nxt, cur, False)
        else:
            filler = prepare(xn_ref[0], posn_ref[0], norm_modulation(modn_ref), nxt, cur, last_step)
        interleave(primary, filler)
    for _ in finish(*mixed[STEP_TILES - 1], rows[STEP_TILES - 1]):
        pass


def _chunk_matrices(ts):
    r = np.arange(ts)
    same = (r[:, None] // GLA_CHUNK) == (r[None, :] // GLA_CHUNK)
    return jnp.asarray(same & (r[None, :] <= r[:, None]), BF16), jnp.asarray(same, BF16)


def _rope_inv_freq_column():
    inv = 1.0 / (ROPE_THETA ** (jnp.arange(0, SWA_HEAD_DIM, 2, dtype=F32) / SWA_HEAD_DIM))
    return inv.reshape(SWA_HEAD_DIM // 2, 1)


def kernel(x, c, positions, w_ada, b_ada, g_norm, w_in, w_decay, b_decay, g_gla_head, sinks, w_out, g_final):
    B, S, D = x.shape
    ts = SEQ_TILE
    assert D == D_MODEL and S % (STEP_TILES * ts) == 0 and ts % SWA_BLOCK == 0 and STEP_TILES % 2 == 0
    assert w_ada.shape[0] == 1, "one layer"
    n_tiles = S // ts

    mod = _adaln(c.astype(F32), w_ada[0], b_ada[0][None, :]).reshape(B, 1, 3 * D)

    w_in_b, w_out_b = _prepare_weights(jnp.transpose(w_in[0]), w_out[0])
    w_dec = jnp.pad(w_decay[0].astype(BF16), ((0, LANES - GLA_GATE_RANK), (0, 0)))
    tri, blk = _chunk_matrices(ts)
    pos3 = positions.reshape(B, 1, S)

    const2 = lambda b, i, s: (0, 0)
    full = lambda a: pl.BlockSpec(a.shape, const2)
    n_steps = n_tiles // STEP_TILES

    def next_row_tile(b, i):
        wraps = i == n_steps - 1
        at_end = wraps & (b == B - 1)
        row = jnp.where(wraps & jnp.logical_not(at_end), b + 1, b)
        tile = jnp.where(wraps, jnp.where(at_end, n_tiles - 1, 0), STEP_TILES * (i + 1))
        return row, tile

    operands = [
        x, x, pos3, pos3,
        mod, mod,
        g_norm[0][None, :],
        _rope_inv_freq_column(),
        tri, blk,
        w_in_b,
        w_dec, b_decay[0][None, :], g_gla_head[0][None, :],
        w_out_b, g_final[None, :],
    ]
    in_specs = [
        pl.BlockSpec((1, STEP_TILES * ts, D), lambda b, i, s: (b, i, 0)),
        pl.BlockSpec((1, ts, D), lambda b, i, s: next_row_tile(b, i) + (0,)),
        pl.BlockSpec((1, 1, STEP_TILES * ts), lambda b, i, s: (b, 0, i)),
        pl.BlockSpec((1, 1, ts), lambda b, i, s: (next_row_tile(b, i)[0], 0, next_row_tile(b, i)[1])),
        pl.BlockSpec((1, 1, 3 * D), lambda b, i, s: (b, 0, 0)),
        pl.BlockSpec((1, 1, 3 * D), lambda b, i, s: (next_row_tile(b, i)[0], 0, 0)),
    ] + [full(a) for a in operands[6:]]

    operand_scratch = [pltpu.VMEM(shape, dtype) for _, shape, dtype in _OPERAND_BUFFERS]
    grid_spec = pltpu.PrefetchScalarGridSpec(
        num_scalar_prefetch=1,
        grid=(B, n_steps),
        in_specs=in_specs,
        out_specs=pl.BlockSpec((1, STEP_TILES * ts, D), lambda b, i, s: (b, i, 0)),
        scratch_shapes=[
            pltpu.VMEM((GLA_HEADS, GLA_DK, GLA_DV), F32),
        ] + operand_scratch + operand_scratch,
    )
    return pl.pallas_call(
        _layer_body,
        grid_spec=grid_spec,
        out_shape=jax.ShapeDtypeStruct((B, S, D), x.dtype),
        compiler_params=pltpu.CompilerParams(
            dimension_semantics=("arbitrary", "arbitrary"),
            vmem_limit_bytes=VMEM_LIMIT_BYTES,
        ),
        name="hymba_layer",
    )(sinks[0].astype(F32), *operands)
```

```python
import jax
import jax.numpy as jnp
import numpy as np
from jax import lax
from jax.experimental import pallas as pl
from jax.experimental.pallas import tpu as pltpu

D_MODEL = 1024
GLA_HEADS = 4
GLA_DK = 64
GLA_DV = 128
GLA_WIDTH = GLA_HEADS * GLA_DV
GLA_QK = GLA_HEADS * GLA_DK
GLA_GATE_RANK = 16
GLA_GATE_TAU = 16.0
GLA_CHUNK = 64
SWA_Q_HEADS = 8
SWA_KV_HEADS = 2
SWA_GROUP = SWA_Q_HEADS // SWA_KV_HEADS
SWA_HEAD_DIM = 64
SWA_WIDTH = SWA_Q_HEADS * SWA_HEAD_DIM
SWA_KV_WIDTH = SWA_KV_HEADS * SWA_HEAD_DIM
WINDOW = 128
SWA_BLOCK = 128
ROPE_THETA = 10000.0
RMS_EPS = 1e-6

LANES = 128
MXU_COLS = 256
PROJ_COLS = 2 * MXU_COLS
SEQ_TILE = 256
STEP_TILES = 2
LOG2E = 1.4426950408889634
NEG_BIG = -1e30
VMEM_LIMIT_BYTES = 56 * 1024 * 1024

W_GA = 0
W_SK = W_GA + LANES
W_SV = W_SK + SWA_KV_WIDTH
W_GQ = W_SV + SWA_KV_WIDTH
W_GK = W_GQ + GLA_QK
W_GV = W_GK + GLA_QK
W_SQ = W_GV + GLA_WIDTH
W_GZ = W_SQ + SWA_WIDTH
W_SZ = W_GZ + GLA_WIDTH
W_COLS = W_SZ + SWA_WIDTH

SWA_UNITS = (SEQ_TILE // SWA_BLOCK) * SWA_KV_HEADS

BF16 = jnp.bfloat16
F32 = jnp.float32


def _dot(a, b):
    return jnp.dot(a, b, preferred_element_type=F32)


def _dot_nt(a, b):
    return lax.dot_general(a, b, (((1,), (1,)), ((), ())), preferred_element_type=F32)


def _dot_tn(a, b):
    return lax.dot_general(a, b, (((0,), (0,)), ((), ())), preferred_element_type=F32)


def _silu(x):
    return x * (1.0 / (1.0 + jnp.exp2(x * (-LOG2E))))


def _log_sigmoid(z):
    return jnp.minimum(z, 0.0) - jnp.log(1.0 + jnp.exp2(jnp.abs(z) * (-LOG2E)))


def _adaln_body(c_ref, w_ref, b_ref, o_ref):
    a = _silu(c_ref[...]).astype(BF16)
    o_ref[...] = _dot(a, w_ref[...].astype(BF16)) + b_ref[...]


def _adaln(c, w_ada, b_ada):
    rows = c.shape[0]
    n_out = w_ada.shape[1]
    col_block = D_MODEL
    return pl.pallas_call(
        _adaln_body,
        grid=(n_out // col_block,),
        in_specs=[
            pl.BlockSpec((rows, D_MODEL), lambda j: (0, 0)),
            pl.BlockSpec((D_MODEL, col_block), lambda j: (0, j)),
            pl.BlockSpec((1, col_block), lambda j: (0, j)),
        ],
        out_specs=pl.BlockSpec((rows, col_block), lambda j: (0, j)),
        out_shape=jax.ShapeDtypeStruct((rows, n_out), F32),
        name="adaln_mod",
    )(c, w_ada, b_ada)


_SRC_GA = 2 * GLA_QK + GLA_WIDTH
_SRC_GZ = _SRC_GA + GLA_GATE_RANK
_SRC_SQ = _SRC_GZ + GLA_WIDTH
_SRC_SK = _SRC_SQ + SWA_WIDTH
_SRC_SV = _SRC_SK + SWA_KV_WIDTH
_SRC_SZ = _SRC_SV + SWA_KV_WIDTH
_SRC_ROWS = _SRC_SZ + SWA_WIDTH
_REGROUP = (
    (W_SK, _SRC_SK, SWA_KV_WIDTH), (W_SV, _SRC_SV, SWA_KV_WIDTH),
    (W_GQ, 0, GLA_QK), (W_GK, GLA_QK, GLA_QK), (W_GV, 2 * GLA_QK, GLA_WIDTH),
    (W_SQ, _SRC_SQ, SWA_WIDTH), (W_GZ, _SRC_GZ, GLA_WIDTH), (W_SZ, _SRC_SZ, SWA_WIDTH),
)
W_COL_BLOCK = 256


def _prepare_weights_body(wt_ref, w_out_ref, o_ref, w_out_o_ref):
    w_out_o_ref[...] = w_out_ref[...].astype(BF16)
    row = lax.broadcasted_iota(jnp.int32, (LANES, 1), 0)
    ga = jnp.where(row < GLA_GATE_RANK, wt_ref[_SRC_GA:_SRC_GA + LANES, :], 0.0)
    o_ref[:, W_GA:W_GA + LANES] = jnp.transpose(ga).astype(BF16)
    for dst, src, width in _REGROUP:
        o_ref[:, dst:dst + width] = jnp.transpose(wt_ref[src:src + width, :]).astype(BF16)


def _prepare_weights(wt, w_out):
    n, d = wt.shape
    k_out, n_out = w_out.shape
    assert n == _SRC_ROWS and d % W_COL_BLOCK == 0 and k_out % (d // W_COL_BLOCK) == 0
    steps = d // W_COL_BLOCK
    out_rows = k_out // steps
    return pl.pallas_call(
        _prepare_weights_body,
        grid=(steps,),
        in_specs=[pl.BlockSpec((n, W_COL_BLOCK), lambda r: (0, r)),
                  pl.BlockSpec((out_rows, n_out), lambda r: (r, 0))],
        out_specs=[pl.BlockSpec((W_COL_BLOCK, W_COLS), lambda r: (r, 0)),
                   pl.BlockSpec((out_rows, n_out), lambda r: (r, 0))],
        out_shape=[jax.ShapeDtypeStruct((d, W_COLS), BF16), jax.ShapeDtypeStruct((k_out, n_out), BF16)],
        name="prepare_weights",
    )(wt, w_out)


_OPERAND_BUFFERS = (
    ("qs", (GLA_HEADS // 2, SEQ_TILE, LANES), BF16),
    ("kb", (GLA_HEADS, SEQ_TILE, SEQ_TILE), BF16),
    ("kd", (SEQ_TILE, GLA_QK), BF16),
    ("gv", (SEQ_TILE, GLA_WIDTH), BF16),
    ("dec", (GLA_QK, SEQ_TILE), F32),
    ("gg", (SEQ_TILE, GLA_WIDTH), BF16),
    ("sg", (SEQ_TILE, SWA_WIDTH), BF16),
    ("qt", (SWA_UNITS, SWA_HEAD_DIM, SWA_GROUP * SWA_BLOCK), BF16),
    ("kk", (SWA_KV_HEADS, SEQ_TILE + SWA_BLOCK, SWA_HEAD_DIM), BF16),
    ("vv", (SWA_KV_HEADS, SEQ_TILE + SWA_BLOCK, SWA_HEAD_DIM), BF16),
)


class _Operands:
    def __init__(self, refs):
        for (name, _, _), ref in zip(_OPERAND_BUFFERS, refs):
            setattr(self, name, ref)


def _layer_body(sinks_ref, x_ref, xn_ref, pos_ref, posn_ref, mod_ref, modn_ref, gnorm_ref, invf_ref, tri_ref, blk_ref,
                w_in_ref, w_dec_ref, b_dec_ref, g_gla_ref, w_out_ref, g_fin_ref,
                o_ref, state_ref, *operand_refs):
    ts = SEQ_TILE
    b_idx = pl.program_id(0)
    i = pl.program_id(1)
    last_step = i == pl.num_programs(1) - 1
    nbuf = len(_OPERAND_BUFFERS)
    ops0 = _Operands(operand_refs[:nbuf])
    ops1 = _Operands(operand_refs[nbuf:])

    gate = mod_ref[0, :, 2 * D_MODEL:3 * D_MODEL]

    def norm_modulation(m_ref):
        return (gnorm_ref[...] * (1.0 + m_ref[0, :, D_MODEL:2 * D_MODEL]), m_ref[0, :, 0:D_MODEL])

    n_chunks = ts // GLA_CHUNK
    n_blocks = ts // SWA_BLOCK
    units = [(n, kv) for n in range(n_blocks) for kv in range(SWA_KV_HEADS)]

    def proj(hb, c0, width):
        return _dot(hb, w_in_ref[:, c0:c0 + width])

    def prepare(x, pos, modulation, ops, prev_ops, starts_sequence):
        norm_gain, shift = modulation
        ms = jnp.mean(x * x, axis=-1, keepdims=True)
        hb = ((x * lax.rsqrt(ms + RMS_EPS)) * norm_gain + shift).astype(BF16)
        yield

        d0 = proj(hb, W_GA, LANES + 2 * SWA_KV_WIDTH)
        yield

        ang_t = invf_ref[...] * pos.astype(F32)
        reps = LANES // (SWA_HEAD_DIM // 2)
        cos_f = jnp.transpose(jnp.concatenate([jnp.cos(ang_t)] * reps, axis=0))
        sin_f = jnp.transpose(jnp.concatenate([jnp.sin(ang_t)] * reps, axis=0))
        lane = lax.broadcasted_iota(jnp.int32, (1, LANES), 1)
        first_half = (lane % SWA_HEAD_DIM) < (SWA_HEAD_DIM // 2)
        sin_s = jnp.where(first_half, -sin_f, sin_f)

        def rope(tv):
            rot = jnp.where(first_half,
                            pltpu.roll(tv, LANES - SWA_HEAD_DIM // 2, axis=1),
                            pltpu.roll(tv, SWA_HEAD_DIM // 2, axis=1))
            return tv * cos_f + rot * sin_s
        yield

        z = _dot(d0[:, 0:LANES].astype(BF16), w_dec_ref[...]) + b_dec_ref[...]
        k_rot = rope(d0[:, LANES:LANES + SWA_KV_WIDTH]).astype(BF16)
        for kv in range(SWA_KV_HEADS):
            hs = slice(kv * SWA_HEAD_DIM, (kv + 1) * SWA_HEAD_DIM)
            v0 = LANES + SWA_KV_WIDTH + kv * SWA_HEAD_DIM
            ops.kk[kv, SWA_BLOCK:SWA_BLOCK + ts, :] = k_rot[:, hs]
            ops.vv[kv, SWA_BLOCK:SWA_BLOCK + ts, :] = d0[:, v0:v0 + SWA_HEAD_DIM].astype(BF16)
            zeros = jnp.zeros((SWA_BLOCK, SWA_HEAD_DIM), BF16)
            if starts_sequence is True:
                k_carry, v_carry = zeros, zeros
            else:
                k_carry = prev_ops.kk[kv, ts:ts + SWA_BLOCK, :]
                v_carry = prev_ops.vv[kv, ts:ts + SWA_BLOCK, :]
                if starts_sequence is not False:
                    k_carry = jnp.where(starts_sequence, zeros, k_carry)
                    v_carry = jnp.where(starts_sequence, zeros, v_carry)
            ops.kk[kv, 0:SWA_BLOCK, :] = k_carry
            ops.vv[kv, 0:SWA_BLOCK, :] = v_carry
        yield

        log_a = _log_sigmoid(z) * (1.0 / GLA_GATE_TAU)
        la_hi = log_a.astype(BF16)
        la_lo = (log_a - la_hi.astype(F32)).astype(BF16)
        b2 = _dot(tri_ref[...], jnp.concatenate([la_hi, la_lo], axis=1))
        gqk = proj(hb, W_GQ, 2 * GLA_QK)
        gq = gqk[:, :GLA_QK]
        gk = gqk[:, GLA_QK:]
        yield
        b = b2[:, :GLA_QK] + b2[:, GLA_QK:]
        b_last = jnp.concatenate(
            [jnp.broadcast_to(b[(c + 1) * GLA_CHUNK - 1:(c + 1) * GLA_CHUNK, :], (GLA_CHUNK, GLA_QK))
             for c in range(n_chunks)], axis=0)
        ops.dec[...] = jnp.transpose(jnp.exp(b_last))
        q_d = gq * (GLA_DK ** -0.5) * jnp.exp(b)
        yield
        ops.kd[...] = (gk * jnp.exp(-b)).astype(BF16)
        k_tail = gk * jnp.exp(b_last - b)

        blk = blk_ref[...]
        low_half = lax.broadcasted_iota(jnp.int32, (1, LANES), 1) < GLA_DK
        tile_reps = ts // LANES
        for pair in range(GLA_HEADS // 2):
            pr = slice(pair * LANES, (pair + 1) * LANES)
            q_pair, k_pair = q_d[:, pr], k_tail[:, pr]
            k_swap = pltpu.roll(k_pair, GLA_DK, axis=1)
            ops.qs[pair] = q_pair.astype(BF16)
            for j in range(2):
                c, d_ = (k_pair, k_swap) if j == 0 else (k_swap, k_pair)
                k_dup = jnp.where(low_half, c, d_).astype(BF16)
                ops.kb[2 * pair + j] = jnp.concatenate([k_dup] * tile_reps, axis=1) * blk
            yield

        for c0 in range(0, GLA_WIDTH, PROJ_COLS):
            ops.gv[:, c0:c0 + PROJ_COLS] = proj(hb, W_GV + c0, PROJ_COLS).astype(BF16)
            yield

        sq_all = proj(hb, W_SQ, SWA_WIDTH)
        for kv in range(SWA_KV_HEADS):
            sq = sq_all[:, kv * MXU_COLS:(kv + 1) * MXU_COLS]
            q_rot = jnp.concatenate([rope(sq[:, j * LANES:(j + 1) * LANES]) for j in range(MXU_COLS // LANES)],
                                    axis=1) * (SWA_HEAD_DIM ** -0.5 * LOG2E)
            for n in range(n_blocks):
                q_t = jnp.transpose(q_rot[n * SWA_BLOCK:(n + 1) * SWA_BLOCK, :])
                ops.qt[n * SWA_KV_HEADS + kv] = jnp.concatenate(
                    [q_t[j * SWA_HEAD_DIM:(j + 1) * SWA_HEAD_DIM, :] for j in range(SWA_GROUP)],
                    axis=1).astype(BF16)
            yield

        for c0 in range(0, GLA_WIDTH, PROJ_COLS):
            ops.gg[:, c0:c0 + PROJ_COLS] = _silu(proj(hb, W_GZ + c0, PROJ_COLS)).astype(BF16)
            yield
        for c0 in range(0, SWA_WIDTH, PROJ_COLS):
            ops.sg[:, c0:c0 + PROJ_COLS] = _silu(proj(hb, W_SZ + c0, PROJ_COLS)).astype(BF16)
            yield

    def gla(ops, mixed):
        row_i = lax.broadcasted_iota(jnp.int32, (ts, ts), 0)
        col_i = lax.broadcasted_iota(jnp.int32, (ts, ts), 1)
        causal = ((row_i // GLA_CHUNK) == (col_i // GLA_CHUNK)) & (col_i <= row_i)
        pair_lane = lax.broadcasted_iota(jnp.int32, (1, LANES), 1)
        heads = []
        for hd in range(GLA_HEADS):
            pr = slice((hd // 2) * LANES, (hd // 2 + 1) * LANES)
            v_h = ops.gv[:, hd * GLA_DV:(hd + 1) * GLA_DV]
            k_h = jnp.where((pair_lane // GLA_DK) == (hd % 2), ops.kd[:, pr], jnp.zeros((), BF16))
            s = jnp.where(causal, _dot_nt(ops.qs[hd // 2], k_h), 0.0).astype(BF16)
            u_all = _dot_tn(ops.kb[hd], v_h)
            heads.append((v_h, s, u_all))
            if hd % 2 == 1:
                yield

        zero_blk = jnp.zeros((GLA_DK, GLA_DV), BF16)
        states = [state_ref[hd] for hd in range(GLA_HEADS)]
        inter = []
        for c in range(n_chunks):
            rs = slice(c * GLA_CHUNK, (c + 1) * GLA_CHUNK)
            state_bd = jnp.concatenate(
                [jnp.concatenate([states[hd].astype(BF16) if j == hd else zero_blk for j in range(GLA_HEADS)], axis=1)
                 for hd in range(GLA_HEADS)], axis=0)
            q_c = jnp.concatenate([ops.qs[pair, rs, :] for pair in range(GLA_HEADS // 2)], axis=1)
            inter.append(_dot(q_c, state_bd))
            last = c * GLA_CHUNK + GLA_CHUNK - 1
            for hd, (v_h, s, u_all) in enumerate(heads):
                d_col = ops.dec[hd * GLA_DK:(hd + 1) * GLA_DK, last:last + 1]
                states[hd] = states[hd] * d_col + u_all[c * GLA_DK:(c + 1) * GLA_DK, :]
            if c % 2 == 1:
                yield
        for hd in range(GLA_HEADS):
            state_ref[hd] = states[hd]
        inter = jnp.concatenate(inter, axis=0)
        outs = []
        for hd, (v_h, s, u_all) in enumerate(heads):
            outs.append(_dot(s, v_h) + inter[:, hd * GLA_DV:(hd + 1) * GLA_DV])
            if hd % 2 == 1:
                yield

        g_gla = g_gla_ref[...]
        for hd, o_h in enumerate(outs):
            vs = slice(hd * GLA_DV, (hd + 1) * GLA_DV)
            o_ms = jnp.mean(o_h * o_h, axis=-1, keepdims=True)
            o_n = o_h * lax.rsqrt(o_ms + RMS_EPS) * g_gla[:, vs]
            mixed.append((o_n * ops.gg[:, vs].astype(F32)).astype(BF16))

    def swa(ops, mixed, first_tile):
        gq_lanes = SWA_GROUP * SWA_BLOCK
        kj = lax.broadcasted_iota(jnp.int32, (2 * SWA_BLOCK, gq_lanes), 0)
        ql = lax.broadcasted_iota(jnp.int32, (2 * SWA_BLOCK, gq_lanes), 1)
        dist = (ql % SWA_BLOCK) + SWA_BLOCK - kj
        valid = (dist >= 0) & (dist < WINDOW)
        head_of_lane = lax.broadcasted_iota(jnp.int32, (1, gq_lanes), 1) // SWA_BLOCK

        scores = []
        for u, (n, kv) in enumerate(units):
            bs = slice(n * SWA_BLOCK, (n + 2) * SWA_BLOCK)
            scores.append(_dot(ops.kk[kv, bs, :], ops.qt[u]))
            if u % 2 == 1:
                yield

        probs = []
        for (n, kv), s_raw in zip(units, scores):
            if n == 0 and first_tile is not None:
                ok = valid & ((kj >= SWA_BLOCK) | jnp.logical_not(first_tile))
            else:
                ok = valid
            s_t = s_raw + jnp.where(ok, 0.0, NEG_BIG)
            sink = jnp.zeros((1, gq_lanes), F32)
            for j in range(SWA_GROUP):
                sink = jnp.where(head_of_lane == j, sinks_ref[kv * SWA_GROUP + j] * LOG2E, sink)
            m = jnp.maximum(jnp.max(s_t, axis=0, keepdims=True), sink)
            e = jnp.exp2(s_t - m)
            den = jnp.sum(e, axis=0, keepdims=True) + jnp.exp2(sink - m)
            probs.append((e.astype(BF16), 1.0 / den))
            yield

        outs = []
        for (n, kv), (e, inv_den) in zip(units, probs):
            bs = slice(n * SWA_BLOCK, (n + 2) * SWA_BLOCK)
            outs.append(_dot_tn(ops.vv[kv, bs, :], e) * inv_den)
        yield

        for n in range(n_blocks):
            rs = slice(n * SWA_BLOCK, (n + 1) * SWA_BLOCK)
            o_groups = []
            for kv in range(SWA_KV_HEADS):
                o_t = outs[n * SWA_KV_HEADS + kv]
                o_stack = jnp.concatenate([o_t[:, j * SWA_BLOCK:(j + 1) * SWA_BLOCK] for j in range(SWA_GROUP)],
                                          axis=0)
                o_groups.append(jnp.transpose(o_stack))
            o_swa = jnp.concatenate(o_groups, axis=1)
            mixed.append((o_swa * ops.sg[rs, :].astype(F32)).astype(BF16))

    def finish(gla_heads, swa_blocks, rows):
        cat = jnp.concatenate([jnp.concatenate(gla_heads, axis=1), jnp.concatenate(swa_blocks, axis=0)], axis=1)
        y = _dot(cat, w_out_ref[...])
        yield
        x = x_ref[0, rows, :]
        xo = x + gate * y
        ms_o = jnp.mean(xo * xo, axis=-1, keepdims=True)
        o_ref[0, rows, :] = xo * lax.rsqrt(ms_o + RMS_EPS) * g_fin_ref[...]

    def interleave(primary, filler):
        live = list(primary)
        while live:
            for g in list(live):
                if next(g, StopIteration) is StopIteration:
                    live.remove(g)
                next(filler, None)
        for _ in filler:
            pass

    rows = [slice(k * ts, (k + 1) * ts) for k in range(STEP_TILES)]
    ops = (ops0, ops1)
    mixed = [([], []) for _ in range(STEP_TILES)]

    @pl.when(i == 0)
    def _():
        state_ref[...] = jnp.zeros_like(state_ref)

    @pl.when((i == 0) & (b_idx == 0))
    def _():
        for _ in prepare(x_ref[0, rows[0], :], pos_ref[0, :, rows[0]], norm_modulation(mod_ref), ops0, None, True):
            pass

    for k in range(STEP_TILES):
        cur, nxt = ops[k % 2], ops[(k + 1) % 2]
        primary = [gla(cur, mixed[k][0]), swa(cur, mixed[k][1], (i == 0) if k == 0 else None)]
        if k > 0:
            primary.insert(0, finish(*mixed[k - 1], rows[k - 1]))
        if k + 1 < STEP_TILES:
            filler = prepare(x_ref[0, rows[k + 1], :], pos_ref[0, :, rows[k + 1]], norm_modulation(mod_ref),
                             nxt, cur, False)
        else:
            filler = prepare(xn_ref[0], posn_ref[0], norm_modulation(modn_ref), nxt, cur, last_step)
        interleave(primary, filler)
    for _ in finish(*mixed[STEP_TILES - 1], rows[STEP_TILES - 1]):
        pass


def _chunk_matrices(ts):
    r = np.arange(ts)
    same = (r[:, None] // GLA_CHUNK) == (r[None, :] // GLA_CHUNK)
    return jnp.asarray(same & (r[None, :] <= r[:, None]), BF16), jnp.asarray(same, BF16)


def _rope_inv_freq_column():
    inv = 1.0 / (ROPE_THETA ** (jnp.arange(0, SWA_HEAD_DIM, 2, dtype=F32) / SWA_HEAD_DIM))
    return inv.reshape(SWA_HEAD_DIM // 2, 1)


def kernel(x, c, positions, w_ada, b_ada, g_norm, w_in, w_decay, b_decay, g_gla_head, sinks, w_out, g_final):
    B, S, D = x.shape
    ts = SEQ_TILE
    assert D == D_MODEL and S % (STEP_TILES * ts) == 0 and ts % SWA_BLOCK == 0 and STEP_TILES % 2 == 0
    assert w_ada.shape[0] == 1, "one layer"
    n_tiles = S // ts

    mod = _adaln(c.astype(F32), w_ada[0], b_ada[0][None, :]).reshape(B, 1, 3 * D)

    w_in_b, w_out_b = _prepare_weights(jnp.transpose(w_in[0]), w_out[0])
    w_dec = jnp.pad(w_decay[0].astype(BF16), ((0, LANES - GLA_GATE_RANK), (0, 0)))
    tri, blk = _chunk_matrices(ts)
    pos3 = positions.reshape(B, 1, S)

    const2 = lambda b, i, s: (0, 0)
    full = lambda a: pl.BlockSpec(a.shape, const2)
    n_steps = n_tiles // STEP_TILES

    def next_row_tile(b, i):
        wraps = i == n_steps - 1
        at_end = wraps & (b == B - 1)
        row = jnp.where(wraps & jnp.logical_not(at_end), b + 1, b)
        tile = jnp.where(wraps, jnp.where(at_end, n_tiles - 1, 0), STEP_TILES * (i + 1))
        return row, tile

    operands = [
        x, x, pos3, pos3,
        mod, mod,
        g_norm[0][None, :],
        _rope_inv_freq_column(),
        tri, blk,
        w_in_b,
        w_dec, b_decay[0][None, :], g_gla_head[0][None, :],
        w_out_b, g_final[None, :],
    ]
    in_specs = [
        pl.BlockSpec((1, STEP_TILES * ts, D), lambda b, i, s: (b, i, 0)),
        pl.BlockSpec((1, ts, D), lambda b, i, s: next_row_tile(b, i) + (0,)),
        pl.BlockSpec((1, 1, STEP_TILES * ts), lambda b, i, s: (b, 0, i)),
        pl.BlockSpec((1, 1, ts), lambda b, i, s: (next_row_tile(b, i)[0], 0, next_row_tile(b, i)[1])),
        pl.BlockSpec((1, 1, 3 * D), lambda b, i, s: (b, 0, 0)),
        pl.BlockSpec((1, 1, 3 * D), lambda b, i, s: (next_row_tile(b, i)[0], 0, 0)),
    ] + [full(a) for a in operands[6:]]

    operand_scratch = [pltpu.VMEM(shape, dtype) for _, shape, dtype in _OPERAND_BUFFERS]
    grid_spec = pltpu.PrefetchScalarGridSpec(
        num_scalar_prefetch=1,
        grid=(B, n_steps),
        in_specs=in_specs,
        out_specs=pl.BlockSpec((1, STEP_TILES * ts, D), lambda b, i, s: (b, i, 0)),
        scratch_shapes=[
            pltpu.VMEM((GLA_HEADS, GLA_DK, GLA_DV), F32),
        ] + operand_scratch + operand_scratch,
    )
    return pl.pallas_call(
        _layer_body,
        grid_spec=grid_spec,
        out_shape=jax.ShapeDtypeStruct((B, S, D), x.dtype),
        compiler_params=pltpu.CompilerParams(
            dimension_semantics=("arbitrary", "arbitrary"),
            vmem_limit_bytes=VMEM_LIMIT_BYTES,
        ),
        name="hymba_layer",
    )(sinks[0].astype(F32), *operands)
```

```python
import jax
import jax.numpy as jnp
import numpy as np
from jax import lax
from jax.experimental import pallas as pl
from jax.experimental.pallas import tpu as pltpu

D_MODEL = 1024
GLA_HEADS = 4
GLA_DK = 64
GLA_DV = 128
GLA_WIDTH = GLA_HEADS * GLA_DV
GLA_QK = GLA_HEADS * GLA_DK
GLA_GATE_RANK = 16
GLA_GATE_TAU = 16.0
GLA_CHUNK = 64
SWA_Q_HEADS = 8
SWA_KV_HEADS = 2
SWA_GROUP = SWA_Q_HEADS // SWA_KV_HEADS
SWA_HEAD_DIM = 64
SWA_WIDTH = SWA_Q_HEADS * SWA_HEAD_DIM
SWA_KV_WIDTH = SWA_KV_HEADS * SWA_HEAD_DIM
WINDOW = 128
SWA_BLOCK = 128
ROPE_THETA = 10000.0
RMS_EPS = 1e-6

LANES = 128
MXU_COLS = 256
PROJ_COLS = 2 * MXU_COLS
SEQ_TILE = 256
STEP_TILES = 2
LOG2E = 1.4426950408889634
NEG_BIG = -1e30
VMEM_LIMIT_BYTES = 56 * 1024 * 1024

W_GA = 0
W_SK = W_GA + LANES
W_SV = W_SK + SWA_KV_WIDTH
W_GQ = W_SV + SWA_KV_WIDTH
W_GK = W_GQ + GLA_QK
W_GV = W_GK + GLA_QK
W_SQ = W_GV + GLA_WIDTH
W_GZ = W_SQ + SWA_WIDTH
W_SZ = W_GZ + GLA_WIDTH
W_COLS = W_SZ + SWA_WIDTH

SWA_UNITS = (SEQ_TILE // SWA_BLOCK) * SWA_KV_HEADS

BF16 = jnp.bfloat16
F32 = jnp.float32


def _dot(a, b):
    return jnp.dot(a, b, preferred_element_type=F32)


def _dot_nt(a, b):
    return lax.dot_general(a, b, (((1,), (1,)), ((), ())), preferred_element_type=F32)


def _dot_tn(a, b):
    return lax.dot_general(a, b, (((0,), (0,)), ((), ())), preferred_element_type=F32)


def _silu(x):
    return x * (1.0 / (1.0 + jnp.exp2(x * (-LOG2E))))


def _log_sigmoid(z):
    return jnp.minimum(z, 0.0) - jnp.log(1.0 + jnp.exp2(jnp.abs(z) * (-LOG2E)))


def _adaln_body(c_ref, w_ref, b_ref, o_ref):
    a = _silu(c_ref[...]).astype(BF16)
    o_ref[...] = _dot(a, w_ref[...].astype(BF16)) + b_ref[...]


def _adaln(c, w_ada, b_ada):
    rows = c.shape[0]
    n_out = w_ada.shape[1]
    col_block = D_MODEL
    return pl.pallas_call(
        _adaln_body,
        grid=(n_out // col_block,),
        in_specs=[
            pl.BlockSpec((rows, D_MODEL), lambda j: (0, 0)),
            pl.BlockSpec((D_MODEL, col_block), lambda j: (0, j)),
            pl.BlockSpec((1, col_block), lambda j: (0, j)),
        ],
        out_specs=pl.BlockSpec((rows, col_block), lambda j: (0, j)),
        out_shape=jax.ShapeDtypeStruct((rows, n_out), F32),
        name="adaln_mod",
    )(c, w_ada, b_ada)


_SRC_GA = 2 * GLA_QK + GLA_WIDTH
_SRC_GZ = _SRC_GA + GLA_GATE_RANK
_SRC_SQ = _SRC_GZ + GLA_WIDTH
_SRC_SK = _SRC_SQ + SWA_WIDTH
_SRC_SV = _SRC_SK + SWA_KV_WIDTH
_SRC_SZ = _SRC_SV + SWA_KV_WIDTH
_SRC_ROWS = _SRC_SZ + SWA_WIDTH
_REGROUP = (
    (W_SK, _SRC_SK, SWA_KV_WIDTH), (W_SV, _SRC_SV, SWA_KV_WIDTH),
    (W_GQ, 0, GLA_QK), (W_GK, GLA_QK, GLA_QK), (W_GV, 2 * GLA_QK, GLA_WIDTH),
    (W_SQ, _SRC_SQ, SWA_WIDTH), (W_GZ, _SRC_GZ, GLA_WIDTH), (W_SZ, _SRC_SZ, SWA_WIDTH),
)
W_COL_BLOCK = 256


def _prepare_weights_body(wt_ref, w_out_ref, o_ref, w_out_o_ref):
    w_out_o_ref[...] = w_out_ref[...].astype(BF16)
    row = lax.broadcasted_iota(jnp.int32, (LANES, 1), 0)
    ga = jnp.where(row < GLA_GATE_RANK, wt_ref[_SRC_GA:_SRC_GA + LANES, :], 0.0)
    o_ref[:, W_GA:W_GA + LANES] = jnp.transpose(ga).astype(BF16)
    for dst, src, width in _REGROUP:
        o_ref[:, dst:dst + width] = jnp.transpose(wt_ref[src:src + width, :]).astype(BF16)


def _prepare_weights(wt, w_out):
    n, d = wt.shape
    k_out, n_out = w_out.shape
    assert n == _SRC_ROWS and d % W_COL_BLOCK == 0 and k_out % (d // W_COL_BLOCK) == 0
    steps = d // W_COL_BLOCK
    out_rows = k_out // steps
    return pl.pallas_call(
        _prepare_weights_body,
        grid=(steps,),
        in_specs=[pl.BlockSpec((n, W_COL_BLOCK), lambda r: (0, r)),
                  pl.BlockSpec((out_rows, n_out), lambda r: (r, 0))],
        out_specs=[pl.BlockSpec((W_COL_BLOCK, W_COLS), lambda r: (r, 0)),
                   pl.BlockSpec((out_rows, n_out), lambda r: (r, 0))],
        out_shape=[jax.ShapeDtypeStruct((d, W_COLS), BF16), jax.ShapeDtypeStruct((k_out, n_out), BF16)],
        name="prepare_weights",
    )(wt, w_out)


_OPERAND_BUFFERS = (
    ("qs", (GLA_HEADS // 2, SEQ_TILE, LANES), BF16),
    ("qb", (GLA_HEADS, SEQ_TILE, SEQ_TILE), BF16),
    ("kb", (GLA_HEADS, SEQ_TILE, SEQ_TILE), BF16),
    ("kd", (SEQ_TILE, GLA_QK), BF16),
    ("gv", (SEQ_TILE, GLA_WIDTH), BF16),
    ("dec", (GLA_QK, SEQ_TILE), F32),
    ("gg", (SEQ_TILE, GLA_WIDTH), BF16),
    ("sg", (SEQ_TILE, SWA_WIDTH), BF16),
    ("qt", (SWA_UNITS, SWA_HEAD_DIM, SWA_GROUP * SWA_BLOCK), BF16),
    ("kk", (SWA_KV_HEADS, SEQ_TILE + SWA_BLOCK, SWA_HEAD_DIM), BF16),
    ("vv", (SWA_KV_HEADS, SEQ_TILE + SWA_BLOCK, SWA_HEAD_DIM), BF16),
)


class _Operands:
    def __init__(self, refs):
        for (name, _, _), ref in zip(_OPERAND_BUFFERS, refs):
            setattr(self, name, ref)


def _layer_body(sinks_ref, x_ref, xn_ref, pos_ref, posn_ref, mod_ref, modn_ref, gnorm_ref, invf_ref, tri_ref, blk_ref,
                w_in_ref, w_dec_ref, b_dec_ref, g_gla_ref, w_out_ref, g_fin_ref,
                o_ref, state_ref, *operand_refs):
    ts = SEQ_TILE
    b_idx = pl.program_id(0)
    i = pl.program_id(1)
    last_step = i == pl.num_programs(1) - 1
    nbuf = len(_OPERAND_BUFFERS)
    ops0 = _Operands(operand_refs[:nbuf])
    ops1 = _Operands(operand_refs[nbuf:])

    gate = mod_ref[0, :, 2 * D_MODEL:3 * D_MODEL]

    def norm_modulation(m_ref):
        return (gnorm_ref[...] * (1.0 + m_ref[0, :, D_MODEL:2 * D_MODEL]), m_ref[0, :, 0:D_MODEL])

    n_chunks = ts // GLA_CHUNK
    n_blocks = ts // SWA_BLOCK
    units = [(n, kv) for n in range(n_blocks) for kv in range(SWA_KV_HEADS)]

    def proj(hb, c0, width):
        return _dot(hb, w_in_ref[:, c0:c0 + width])

    def prepare(x, pos, modulation, ops, prev_ops, starts_sequence):
        norm_gain, shift = modulation
        ms = jnp.mean(x * x, axis=-1, keepdims=True)
        hb = ((x * lax.rsqrt(ms + RMS_EPS)) * norm_gain + shift).astype(BF16)
        yield

        d0 = proj(hb, W_GA, LANES + 2 * SWA_KV_WIDTH)
        yield

        ang_t = invf_ref[...] * pos.astype(F32)
        reps = LANES // (SWA_HEAD_DIM // 2)
        cos_f = jnp.transpose(jnp.concatenate([jnp.cos(ang_t)] * reps, axis=0))
        sin_f = jnp.transpose(jnp.concatenate([jnp.sin(ang_t)] * reps, axis=0))
        lane = lax.broadcasted_iota(jnp.int32, (1, LANES), 1)
        first_half = (lane % SWA_HEAD_DIM) < (SWA_HEAD_DIM // 2)
        sin_s = jnp.where(first_half, -sin_f, sin_f)

        def rope(tv):
            rot = jnp.where(first_half,
                            pltpu.roll(tv, LANES - SWA_HEAD_DIM // 2, axis=1),
                            pltpu.roll(tv, SWA_HEAD_DIM // 2, axis=1))
            return tv * cos_f + rot * sin_s
        yield

        z = _dot(d0[:, 0:LANES].astype(BF16), w_dec_ref[...]) + b_dec_ref[...]
        k_rot = rope(d0[:, LANES:LANES + SWA_KV_WIDTH]).astype(BF16)
        for kv in range(SWA_KV_HEADS):
            hs = slice(kv * SWA_HEAD_DIM, (kv + 1) * SWA_HEAD_DIM)
            v0 = LANES + SWA_KV_WIDTH + kv * SWA_HEAD_DIM
            ops.kk[kv, SWA_BLOCK:SWA_BLOCK + ts, :] = k_rot[:, hs]
            ops.vv[kv, SWA_BLOCK:SWA_BLOCK + ts, :] = d0[:, v0:v0 + SWA_HEAD_DIM].astype(BF16)
            zeros = jnp.zeros((SWA_BLOCK, SWA_HEAD_DIM), BF16)
            if starts_sequence is True:
                k_carry, v_carry = zeros, zeros
            else:
                k_carry = prev_ops.kk[kv, ts:ts + SWA_BLOCK, :]
                v_carry = prev_ops.vv[kv, ts:ts + SWA_BLOCK, :]
                if starts_sequence is not False:
                    k_carry = jnp.where(starts_sequence, zeros, k_carry)
                    v_carry = jnp.where(starts_sequence, zeros, v_carry)
            ops.kk[kv, 0:SWA_BLOCK, :] = k_carry
            ops.vv[kv, 0:SWA_BLOCK, :] = v_carry
        yield

        log_a = _log_sigmoid(z) * (1.0 / GLA_GATE_TAU)
        la_hi = log_a.astype(BF16)
        la_lo = (log_a - la_hi.astype(F32)).astype(BF16)
        b2 = _dot(tri_ref[...], jnp.concatenate([la_hi, la_lo], axis=1))
        gqk = proj(hb, W_GQ, 2 * GLA_QK)
        gq = gqk[:, :GLA_QK]
        gk = gqk[:, GLA_QK:]
        yield
        b = b2[:, :GLA_QK] + b2[:, GLA_QK:]
        e_b = jnp.exp(b)
        e_last = jnp.concatenate(
            [jnp.broadcast_to(e_b[(c + 1) * GLA_CHUNK - 1:(c + 1) * GLA_CHUNK, :], (GLA_CHUNK, GLA_QK))
             for c in range(n_chunks)], axis=0)
        ops.dec[...] = jnp.transpose(e_last)
        q_d = gq * (GLA_DK ** -0.5) * e_b
        yield
        k_d = gk * (1.0 / e_b)
        ops.kd[...] = k_d.astype(BF16)
        k_tail = k_d * e_last

        blk = blk_ref[...]
        low_half = lax.broadcasted_iota(jnp.int32, (1, LANES), 1) < GLA_DK
        tile_reps = ts // LANES
        for pair in range(GLA_HEADS // 2):
            pr = slice(pair * LANES, (pair + 1) * LANES)
            q_pair, k_pair = q_d[:, pr], k_tail[:, pr]
            q_swap = pltpu.roll(q_pair, GLA_DK, axis=1)
            k_swap = pltpu.roll(k_pair, GLA_DK, axis=1)
            ops.qs[pair] = q_pair.astype(BF16)
            for j in range(2):
                a, b_ = (q_pair, q_swap) if j == 0 else (q_swap, q_pair)
                c, d_ = (k_pair, k_swap) if j == 0 else (k_swap, k_pair)
                q_dup = jnp.where(low_half, a, b_).astype(BF16)
                k_dup = jnp.where(low_half, c, d_).astype(BF16)
                ops.qb[2 * pair + j] = jnp.concatenate([q_dup] * tile_reps, axis=1) * blk
                ops.kb[2 * pair + j] = jnp.concatenate([k_dup] * tile_reps, axis=1) * blk
            yield

        for c0 in range(0, GLA_WIDTH, PROJ_COLS):
            ops.gv[:, c0:c0 + PROJ_COLS] = proj(hb, W_GV + c0, PROJ_COLS).astype(BF16)
            yield

        sq_all = proj(hb, W_SQ, SWA_WIDTH)
        for kv in range(SWA_KV_HEADS):
            sq = sq_all[:, kv * MXU_COLS:(kv + 1) * MXU_COLS]
            q_rot = jnp.concatenate([rope(sq[:, j * LANES:(j + 1) * LANES]) for j in range(MXU_COLS // LANES)],
                                    axis=1) * (SWA_HEAD_DIM ** -0.5 * LOG2E)
            for n in range(n_blocks):
                q_t = jnp.transpose(q_rot[n * SWA_BLOCK:(n + 1) * SWA_BLOCK, :])
                ops.qt[n * SWA_KV_HEADS + kv] = jnp.concatenate(
                    [q_t[j * SWA_HEAD_DIM:(j + 1) * SWA_HEAD_DIM, :] for j in range(SWA_GROUP)],
                    axis=1).astype(BF16)
            yield

        for c0 in range(0, GLA_WIDTH, PROJ_COLS):
            ops.gg[:, c0:c0 + PROJ_COLS] = _silu(proj(hb, W_GZ + c0, PROJ_COLS)).astype(BF16)
            yield
        for c0 in range(0, SWA_WIDTH, PROJ_COLS):
            ops.sg[:, c0:c0 + PROJ_COLS] = _silu(proj(hb, W_SZ + c0, PROJ_COLS)).astype(BF16)
            yield

    def gla(ops, mixed):
        row_i = lax.broadcasted_iota(jnp.int32, (ts, ts), 0)
        col_i = lax.broadcasted_iota(jnp.int32, (ts, ts), 1)
        causal = ((row_i // GLA_CHUNK) == (col_i // GLA_CHUNK)) & (col_i <= row_i)
        pair_lane = lax.broadcasted_iota(jnp.int32, (1, LANES), 1)
        heads = []
        for hd in range(GLA_HEADS):
            pr = slice((hd // 2) * LANES, (hd // 2 + 1) * LANES)
            v_h = ops.gv[:, hd * GLA_DV:(hd + 1) * GLA_DV]
            k_h = jnp.where((pair_lane // GLA_DK) == (hd % 2), ops.kd[:, pr], jnp.zeros((), BF16))
            s = jnp.where(causal, _dot_nt(ops.qs[hd // 2], k_h), 0.0).astype(BF16)
            u_all = _dot_tn(ops.kb[hd], v_h)
            heads.append((v_h, s, u_all))
            if hd % 2 == 1:
                yield

        staged = []
        for hd, (v_h, s, u_all) in enumerate(heads):
            st = state_ref[hd]
            entering = []
            for c in range(n_chunks):
                entering.append(st)
                last = c * GLA_CHUNK + GLA_CHUNK - 1
                d_col = ops.dec[hd * GLA_DK:(hd + 1) * GLA_DK, last:last + 1]
                st = st * d_col + u_all[c * GLA_DK:(c + 1) * GLA_DK, :]
            state_ref[hd] = st
            s_stack = jnp.concatenate(entering, axis=0).astype(BF16)
            staged.append((s, v_h, s_stack))
        yield

        outs = []
        for hd, (s, v_h, s_stack) in enumerate(staged):
            outs.append(_dot(s, v_h) + _dot(ops.qb[hd], s_stack))
            if hd % 2 == 1:
                yield

        g_gla = g_gla_ref[...]
        for hd, o_h in enumerate(outs):
            vs = slice(hd * GLA_DV, (hd + 1) * GLA_DV)
            o_ms = jnp.mean(o_h * o_h, axis=-1, keepdims=True)
            o_n = o_h * lax.rsqrt(o_ms + RMS_EPS) * g_gla[:, vs]
            mixed.append((o_n * ops.gg[:, vs].astype(F32)).astype(BF16))

    def swa(ops, mixed, first_tile):
        gq_lanes = SWA_GROUP * SWA_BLOCK
        kj = lax.broadcasted_iota(jnp.int32, (2 * SWA_BLOCK, gq_lanes), 0)
        ql = lax.broadcasted_iota(jnp.int32, (2 * SWA_BLOCK, gq_lanes), 1)
        dist = (ql % SWA_BLOCK) + SWA_BLOCK - kj
        valid = (dist >= 0) & (dist < WINDOW)
        head_of_lane = lax.broadcasted_iota(jnp.int32, (1, gq_lanes), 1) // SWA_BLOCK

        scores = []
        for u, (n, kv) in enumerate(units):
            bs = slice(n * SWA_BLOCK, (n + 2) * SWA_BLOCK)
            scores.append(_dot(ops.kk[kv, bs, :], ops.qt[u]))
            if u % 2 == 1:
                yield

        probs = []
        for (n, kv), s_raw in zip(units, scores):
            if n == 0 and first_tile is not None:
                ok = valid & ((kj >= SWA_BLOCK) | jnp.logical_not(first_tile))
            else:
                ok = valid
            s_t = s_raw + jnp.where(ok, 0.0, NEG_BIG)
            sink = jnp.zeros((1, gq_lanes), F32)
            for j in range(SWA_GROUP):
                sink = jnp.where(head_of_lane == j, sinks_ref[kv * SWA_GROUP + j] * LOG2E, sink)
            m = jnp.maximum(jnp.max(s_t, axis=0, keepdims=True), sink)
            e = jnp.exp2(s_t - m)
            den = jnp.sum(e, axis=0, keepdims=True) + jnp.exp2(sink - m)
            probs.append((e.astype(BF16), 1.0 / den))
            yield

        outs = []
        for (n, kv), (e, inv_den) in zip(units, probs):
            bs = slice(n * SWA_BLOCK, (n + 2) * SWA_BLOCK)
            outs.append(_dot_tn(ops.vv[kv, bs, :], e) * inv_den)
        yield

        for n in range(n_blocks):
            rs = slice(n * SWA_BLOCK, (n + 1) * SWA_BLOCK)
            o_groups = []
            for kv in range(SWA_KV_HEADS):
                o_t = outs[n * SWA_KV_HEADS + kv]
                o_stack = jnp.concatenate([o_t[:, j * SWA_BLOCK:(j + 1) * SWA_BLOCK] for j in range(SWA_GROUP)],
                                          axis=0)
                o_groups.append(jnp.transpose(o_stack))
            o_swa = jnp.concatenate(o_groups, axis=1)
            mixed.append((o_swa * ops.sg[rs, :].astype(F32)).astype(BF16))

    def finish(gla_heads, swa_blocks, rows):
        cat = jnp.concatenate([jnp.concatenate(gla_heads, axis=1), jnp.concatenate(swa_blocks, axis=0)], axis=1)
        y = _dot(cat, w_out_ref[...])
        yield
        x = x_ref[0, rows, :]
        xo = x + gate * y
        ms_o = jnp.mean(xo * xo, axis=-1, keepdims=True)
        o_ref[0, rows, :] = xo * lax.rsqrt(ms_o + RMS_EPS) * g_fin_ref[...]

    def interleave(primary, filler):
        live = list(primary)
        while live:
            for g in list(live):
                if next(g, StopIteration) is StopIteration:
                    live.remove(g)
                next(filler, None)
        for _ in filler:
            pass

    rows = [slice(k * ts, (k + 1) * ts) for k in range(STEP_TILES)]
    ops = (ops0, ops1)
    mixed = [([], []) for _ in range(STEP_TILES)]

    @pl.when(i == 0)
    def _():
        state_ref[...] = jnp.zeros_like(state_ref)

    @pl.when((i == 0) & (b_idx == 0))
    def _():
        for _ in prepare(x_ref[0, rows[0], :], pos_ref[0, :, rows[0]], norm_modulation(mod_ref), ops0, None, True):
            pass

    for k in range(STEP_TILES):
        cur, nxt = ops[k % 2], ops[(k + 1) % 2]
        primary = [gla(cur, mixed[k][0]), swa(cur, mixed[k][1], (i == 0) if k == 0 else None)]
        if k > 0:
            primary.insert(0, finish(*mixed[k - 1], rows[k - 1]))
        if k + 1 < STEP_TILES:
            filler = prepare(x_ref[0, rows[k + 1], :], pos_ref[0, :, rows[k + 1]], norm_modulation(mod_ref),
                             nxt, cur, False)
        else:
            filler = prepare(xn_ref[0], posn_ref[0], norm_modulation(modn_ref), nxt, cur, last_step)
        interleave(primary, filler)
    for _ in finish(*mixed[STEP_TILES - 1], rows[STEP_TILES - 1]):
        pass


def _chunk_matrices(ts):
    r = np.arange(ts)
    same = (r[:, None] // GLA_CHUNK) == (r[None, :] // GLA_CHUNK)
    return jnp.asarray(same & (r[None, :] <= r[:, None]), BF16), jnp.asarray(same, BF16)


def _rope_inv_freq_column():
    inv = 1.0 / (ROPE_THETA ** (jnp.arange(0, SWA_HEAD_DIM, 2, dtype=F32) / SWA_HEAD_DIM))
    return inv.reshape(SWA_HEAD_DIM // 2, 1)


def kernel(x, c, positions, w_ada, b_ada, g_norm, w_in, w_decay, b_decay, g_gla_head, sinks, w_out, g_final):
    B, S, D = x.shape
    ts = SEQ_TILE
    assert D == D_MODEL and S % (STEP_TILES * ts) == 0 and ts % SWA_BLOCK == 0 and STEP_TILES % 2 == 0
    assert w_ada.shape[0] == 1, "one layer"
    n_tiles = S // ts

    mod = _adaln(c.astype(F32), w_ada[0], b_ada[0][None, :]).reshape(B, 1, 3 * D)

    w_in_b, w_out_b = _prepare_weights(jnp.transpose(w_in[0]), w_out[0])
    w_dec = jnp.pad(w_decay[0].astype(BF16), ((0, LANES - GLA_GATE_RANK), (0, 0)))
    tri, blk = _chunk_matrices(ts)
    pos3 = positions.reshape(B, 1, S)

    const2 = lambda b, i, s: (0, 0)
    full = lambda a: pl.BlockSpec(a.shape, const2)
    n_steps = n_tiles // STEP_TILES

    def next_row_tile(b, i):
        wraps = i == n_steps - 1
        at_end = wraps & (b == B - 1)
        row = jnp.where(wraps & jnp.logical_not(at_end), b + 1, b)
        tile = jnp.where(wraps, jnp.where(at_end, n_tiles - 1, 0), STEP_TILES * (i + 1))
        return row, tile

    operands = [
        x, x, pos3, pos3,
        mod, mod,
        g_norm[0][None, :],
        _rope_inv_freq_column(),
        tri, blk,
        w_in_b,
        w_dec, b_decay[0][None, :], g_gla_head[0][None, :],
        w_out_b, g_final[None, :],
    ]
    in_specs = [
        pl.BlockSpec((1, STEP_TILES * ts, D), lambda b, i, s: (b, i, 0)),
        pl.BlockSpec((1, ts, D), lambda b, i, s: next_row_tile(b, i) + (0,)),
        pl.BlockSpec((1, 1, STEP_TILES * ts), lambda b, i, s: (b, 0, i)),
        pl.BlockSpec((1, 1, ts), lambda b, i, s: (next_row_tile(b, i)[0], 0, next_row_tile(b, i)[1])),
        pl.BlockSpec((1, 1, 3 * D), lambda b, i, s: (b, 0, 0)),
        pl.BlockSpec((1, 1, 3 * D), lambda b, i, s: (next_row_tile(b, i)[0], 0, 0)),
    ] + [full(a) for a in operands[6:]]

    operand_scratch = [pltpu.VMEM(shape, dtype) for _, shape, dtype in _OPERAND_BUFFERS]
    grid_spec = pltpu.PrefetchScalarGridSpec(
        num_scalar_prefetch=1,
        grid=(B, n_steps),
        in_specs=in_specs,
        out_specs=pl.BlockSpec((1, STEP_TILES * ts, D), lambda b, i, s: (b, i, 0)),
        scratch_shapes=[
            pltpu.VMEM((GLA_HEADS, GLA_DK, GLA_DV), F32),
        ] + operand_scratch + operand_scratch,
    )
    return pl.pallas_call(
        _layer_body,
        grid_spec=grid_spec,
        out_shape=jax.ShapeDtypeStruct((B, S, D), x.dtype),
        compiler_params=pltpu.CompilerParams(
            dimension_semantics=("arbitrary", "arbitrary"),
            vmem_limit_bytes=VMEM_LIMIT_BYTES,
        ),
        name="hymba_layer",
    )(sinks[0].astype(F32), *operands)
```

```python
import jax
import jax.numpy as jnp
import numpy as np
from jax import lax
from jax.experimental import pallas as pl
from jax.experimental.pallas import tpu as pltpu

D_MODEL = 1024
GLA_HEADS = 4
GLA_DK = 64
GLA_DV = 128
GLA_WIDTH = GLA_HEADS * GLA_DV
GLA_QK = GLA_HEADS * GLA_DK
GLA_GATE_RANK = 16
GLA_GATE_TAU = 16.0
GLA_CHUNK = 64
SWA_Q_HEADS = 8
SWA_KV_HEADS = 2
SWA_GROUP = SWA_Q_HEADS // SWA_KV_HEADS
SWA_HEAD_DIM = 64
SWA_WIDTH = SWA_Q_HEADS * SWA_HEAD_DIM
SWA_KV_WIDTH = SWA_KV_HEADS * SWA_HEAD_DIM
WINDOW = 128
SWA_BLOCK = 128
ROPE_THETA = 10000.0
RMS_EPS = 1e-6

LANES = 128
MXU_COLS = 256
PROJ_COLS = 2 * MXU_COLS
SEQ_TILE = 256
STEP_TILES = 2
LOG2E = 1.4426950408889634
NEG_BIG = -1e30
VMEM_LIMIT_BYTES = 40 * 1024 * 1024

W_GA = 0
W_SK = W_GA + LANES
W_SV = W_SK + SWA_KV_WIDTH
W_GQ = W_SV + SWA_KV_WIDTH
W_GK = W_GQ + GLA_QK
W_GV = W_GK + GLA_QK
W_SQ = W_GV + GLA_WIDTH
W_GZ = W_SQ + SWA_WIDTH
W_SZ = W_GZ + GLA_WIDTH
W_COLS = W_SZ + SWA_WIDTH

SWA_UNITS = (SEQ_TILE // SWA_BLOCK) * SWA_KV_HEADS

BF16 = jnp.bfloat16
F32 = jnp.float32


def _dot(a, b):
    return jnp.dot(a, b, preferred_element_type=F32)


def _dot_nt(a, b):
    return lax.dot_general(a, b, (((1,), (1,)), ((), ())), preferred_element_type=F32)


def _dot_tn(a, b):
    return lax.dot_general(a, b, (((0,), (0,)), ((), ())), preferred_element_type=F32)


def _silu(x):
    return x * (1.0 / (1.0 + jnp.exp2(x * (-LOG2E))))


def _log_sigmoid(z):
    return jnp.minimum(z, 0.0) - jnp.log(1.0 + jnp.exp2(jnp.abs(z) * (-LOG2E)))


def _adaln_body(c_ref, w_ref, b_ref, o_ref):
    a = _silu(c_ref[...]).astype(BF16)
    o_ref[...] = _dot(a, w_ref[...].astype(BF16)) + b_ref[...]


def _adaln(c, w_ada, b_ada):
    rows = c.shape[0]
    n_out = w_ada.shape[1]
    col_block = D_MODEL
    return pl.pallas_call(
        _adaln_body,
        grid=(n_out // col_block,),
        in_specs=[
            pl.BlockSpec((rows, D_MODEL), lambda j: (0, 0)),
            pl.BlockSpec((D_MODEL, col_block), lambda j: (0, j)),
            pl.BlockSpec((1, col_block), lambda j: (0, j)),
        ],
        out_specs=pl.BlockSpec((rows, col_block), lambda j: (0, j)),
        out_shape=jax.ShapeDtypeStruct((rows, n_out), F32),
        name="adaln_mod",
    )(c, w_ada, b_ada)


_SRC_GA = 2 * GLA_QK + GLA_WIDTH
_SRC_GZ = _SRC_GA + GLA_GATE_RANK
_SRC_SQ = _SRC_GZ + GLA_WIDTH
_SRC_SK = _SRC_SQ + SWA_WIDTH
_SRC_SV = _SRC_SK + SWA_KV_WIDTH
_SRC_SZ = _SRC_SV + SWA_KV_WIDTH
_SRC_ROWS = _SRC_SZ + SWA_WIDTH
_REGROUP = (
    (W_SK, _SRC_SK, SWA_KV_WIDTH), (W_SV, _SRC_SV, SWA_KV_WIDTH),
    (W_GQ, 0, GLA_QK), (W_GK, GLA_QK, GLA_QK), (W_GV, 2 * GLA_QK, GLA_WIDTH),
    (W_SQ, _SRC_SQ, SWA_WIDTH), (W_GZ, _SRC_GZ, GLA_WIDTH), (W_SZ, _SRC_SZ, SWA_WIDTH),
)
W_COL_BLOCK = 256


def _prepare_weights_body(wt_ref, w_out_ref, o_ref, w_out_o_ref):
    w_out_o_ref[...] = w_out_ref[...].astype(BF16)
    row = lax.broadcasted_iota(jnp.int32, (LANES, 1), 0)
    ga = jnp.where(row < GLA_GATE_RANK, wt_ref[_SRC_GA:_SRC_GA + LANES, :], 0.0)
    o_ref[:, W_GA:W_GA + LANES] = jnp.transpose(ga).astype(BF16)
    for dst, src, width in _REGROUP:
        o_ref[:, dst:dst + width] = jnp.transpose(wt_ref[src:src + width, :]).astype(BF16)


def _prepare_weights(wt, w_out):
    n, d = wt.shape
    k_out, n_out = w_out.shape
    assert n == _SRC_ROWS and d % W_COL_BLOCK == 0 and k_out % (d // W_COL_BLOCK) == 0
    steps = d // W_COL_BLOCK
    out_rows = k_out // steps
    return pl.pallas_call(
        _prepare_weights_body,
        grid=(steps,),
        in_specs=[pl.BlockSpec((n, W_COL_BLOCK), lambda r: (0, r)),
                  pl.BlockSpec((out_rows, n_out), lambda r: (r, 0))],
        out_specs=[pl.BlockSpec((W_COL_BLOCK, W_COLS), lambda r: (r, 0)),
                   pl.BlockSpec((out_rows, n_out), lambda r: (r, 0))],
        out_shape=[jax.ShapeDtypeStruct((d, W_COLS), BF16), jax.ShapeDtypeStruct((k_out, n_out), BF16)],
        name="prepare_weights",
    )(wt, w_out)


_OPERAND_BUFFERS = (
    ("qs", (GLA_HEADS // 2, SEQ_TILE, LANES), BF16),
    ("qb", (GLA_HEADS, SEQ_TILE, SEQ_TILE), BF16),
    ("kb", (GLA_HEADS, SEQ_TILE, SEQ_TILE), BF16),
    ("kd", (SEQ_TILE, GLA_QK), BF16),
    ("gv", (SEQ_TILE, GLA_WIDTH), BF16),
    ("dec", (GLA_QK, SEQ_TILE), F32),
    ("gg", (SEQ_TILE, GLA_WIDTH), BF16),
    ("sg", (SEQ_TILE, SWA_WIDTH), BF16),
    ("qt", (SWA_UNITS, SWA_HEAD_DIM, SWA_GROUP * SWA_BLOCK), BF16),
    ("kk", (SWA_KV_HEADS, SEQ_TILE + SWA_BLOCK, SWA_HEAD_DIM), BF16),
    ("vv", (SWA_KV_HEADS, SEQ_TILE + SWA_BLOCK, SWA_HEAD_DIM), BF16),
)


class _Operands:
    def __init__(self, refs):
        for (name, _, _), ref in zip(_OPERAND_BUFFERS, refs):
            setattr(self, name, ref)


def _layer_body(sinks_ref, x_ref, xn_ref, pos_ref, posn_ref, mod_ref, modn_ref, gnorm_ref, invf_ref, tri_ref, blk_ref,
                w_in_ref, w_dec_ref, b_dec_ref, g_gla_ref, w_out_ref, g_fin_ref,
                o_ref, state_ref, *operand_refs):
    ts = SEQ_TILE
    b_idx = pl.program_id(0)
    i = pl.program_id(1)
    last_step = i == pl.num_programs(1) - 1
    nbuf = len(_OPERAND_BUFFERS)
    ops0 = _Operands(operand_refs[:nbuf])
    ops1 = _Operands(operand_refs[nbuf:])

    gate = mod_ref[0, :, 2 * D_MODEL:3 * D_MODEL]

    def norm_modulation(m_ref):
        return (gnorm_ref[...] * (1.0 + m_ref[0, :, D_MODEL:2 * D_MODEL]), m_ref[0, :, 0:D_MODEL])

    n_chunks = ts // GLA_CHUNK
    n_blocks = ts // SWA_BLOCK
    units = [(n, kv) for n in range(n_blocks) for kv in range(SWA_KV_HEADS)]

    def proj(hb, c0, width):
        return _dot(hb, w_in_ref[:, c0:c0 + width])

    def prepare(x, pos, modulation, ops, prev_ops, starts_sequence):
        norm_gain, shift = modulation
        ms = jnp.mean(x * x, axis=-1, keepdims=True)
        hb = ((x * lax.rsqrt(ms + RMS_EPS)) * norm_gain + shift).astype(BF16)
        yield

        d0 = proj(hb, W_GA, LANES + 2 * SWA_KV_WIDTH)
        yield

        ang_t = invf_ref[...] * pos.astype(F32)
        reps = LANES // (SWA_HEAD_DIM // 2)
        cos_f = jnp.transpose(jnp.concatenate([jnp.cos(ang_t)] * reps, axis=0))
        sin_f = jnp.transpose(jnp.concatenate([jnp.sin(ang_t)] * reps, axis=0))
        lane = lax.broadcasted_iota(jnp.int32, (1, LANES), 1)
        first_half = (lane % SWA_HEAD_DIM) < (SWA_HEAD_DIM // 2)
        sin_s = jnp.where(first_half, -sin_f, sin_f)

        def rope(tv):
            rot = jnp.where(first_half,
                            pltpu.roll(tv, LANES - SWA_HEAD_DIM // 2, axis=1),
                            pltpu.roll(tv, SWA_HEAD_DIM // 2, axis=1))
            return tv * cos_f + rot * sin_s
        yield

        z = _dot(d0[:, 0:LANES].astype(BF16), w_dec_ref[...]) + b_dec_ref[...]
        k_rot = rope(d0[:, LANES:LANES + SWA_KV_WIDTH]).astype(BF16)
        for kv in range(SWA_KV_HEADS):
            hs = slice(kv * SWA_HEAD_DIM, (kv + 1) * SWA_HEAD_DIM)
            v0 = LANES + SWA_KV_WIDTH + kv * SWA_HEAD_DIM
            ops.kk[kv, SWA_BLOCK:SWA_BLOCK + ts, :] = k_rot[:, hs]
            ops.vv[kv, SWA_BLOCK:SWA_BLOCK + ts, :] = d0[:, v0:v0 + SWA_HEAD_DIM].astype(BF16)
            zeros = jnp.zeros((SWA_BLOCK, SWA_HEAD_DIM), BF16)
            if starts_sequence is True:
                k_carry, v_carry = zeros, zeros
            else:
                k_carry = prev_ops.kk[kv, ts:ts + SWA_BLOCK, :]
                v_carry = prev_ops.vv[kv, ts:ts + SWA_BLOCK, :]
                if starts_sequence is not False:
                    k_carry = jnp.where(starts_sequence, zeros, k_carry)
                    v_carry = jnp.where(starts_sequence, zeros, v_carry)
            ops.kk[kv, 0:SWA_BLOCK, :] = k_carry
            ops.vv[kv, 0:SWA_BLOCK, :] = v_carry
        yield

        log_a = _log_sigmoid(z) * (1.0 / GLA_GATE_TAU)
        la_hi = log_a.astype(BF16)
        la_lo = (log_a - la_hi.astype(F32)).astype(BF16)
        b2 = _dot(tri_ref[...], jnp.concatenate([la_hi, la_lo], axis=1))
        gqk = proj(hb, W_GQ, 2 * GLA_QK)
        gq = gqk[:, :GLA_QK]
        gk = gqk[:, GLA_QK:]
        yield
        b = b2[:, :GLA_QK] + b2[:, GLA_QK:]
        e_b = jnp.exp(b)
        e_last = jnp.concatenate(
            [jnp.broadcast_to(e_b[(c + 1) * GLA_CHUNK - 1:(c + 1) * GLA_CHUNK, :], (GLA_CHUNK, GLA_QK))
             for c in range(n_chunks)], axis=0)
        ops.dec[...] = jnp.transpose(e_last)
        q_d = gq * (GLA_DK ** -0.5) * e_b
        yield
        k_d = gk * (1.0 / e_b)
        ops.kd[...] = k_d.astype(BF16)
        k_tail = k_d * e_last

        blk = blk_ref[...]
        low_half = lax.broadcasted_iota(jnp.int32, (1, LANES), 1) < GLA_DK
        tile_reps = ts // LANES
        for pair in range(GLA_HEADS // 2):
            pr = slice(pair * LANES, (pair + 1) * LANES)
            q_pair, k_pair = q_d[:, pr], k_tail[:, pr]
            q_swap = pltpu.roll(q_pair, GLA_DK, axis=1)
            k_swap = pltpu.roll(k_pair, GLA_DK, axis=1)
            ops.qs[pair] = q_pair.astype(BF16)
            for j in range(2):
                a, b_ = (q_pair, q_swap) if j == 0 else (q_swap, q_pair)
                c, d_ = (k_pair, k_swap) if j == 0 else (k_swap, k_pair)
                q_dup = jnp.where(low_half, a, b_).astype(BF16)
                k_dup = jnp.where(low_half, c, d_).astype(BF16)
                ops.qb[2 * pair + j] = jnp.concatenate([q_dup] * tile_reps, axis=1) * blk
                ops.kb[2 * pair + j] = jnp.concatenate([k_dup] * tile_reps, axis=1) * blk
            yield

        for c0 in range(0, GLA_WIDTH, PROJ_COLS):
            ops.gv[:, c0:c0 + PROJ_COLS] = proj(hb, W_GV + c0, PROJ_COLS).astype(BF16)
            yield

        sq_all = proj(hb, W_SQ, SWA_WIDTH)
        for kv in range(SWA_KV_HEADS):
            sq = sq_all[:, kv * MXU_COLS:(kv + 1) * MXU_COLS]
            q_rot = jnp.concatenate([rope(sq[:, j * LANES:(j + 1) * LANES]) for j in range(MXU_COLS // LANES)],
                                    axis=1) * (SWA_HEAD_DIM ** -0.5 * LOG2E)
            for n in range(n_blocks):
                q_t = jnp.transpose(q_rot[n * SWA_BLOCK:(n + 1) * SWA_BLOCK, :])
                ops.qt[n * SWA_KV_HEADS + kv] = jnp.concatenate(
                    [q_t[j * SWA_HEAD_DIM:(j + 1) * SWA_HEAD_DIM, :] for j in range(SWA_GROUP)],
                    axis=1).astype(BF16)
            yield

        for c0 in range(0, GLA_WIDTH, PROJ_COLS):
            ops.gg[:, c0:c0 + PROJ_COLS] = _silu(proj(hb, W_GZ + c0, PROJ_COLS)).astype(BF16)
            yield
        for c0 in range(0, SWA_WIDTH, PROJ_COLS):
            ops.sg[:, c0:c0 + PROJ_COLS] = _silu(proj(hb, W_SZ + c0, PROJ_COLS)).astype(BF16)
            yield

    def gla(ops, mixed):
        row_i = lax.broadcasted_iota(jnp.int32, (ts, ts), 0)
        col_i = lax.broadcasted_iota(jnp.int32, (ts, ts), 1)
        causal = ((row_i // GLA_CHUNK) == (col_i // GLA_CHUNK)) & (col_i <= row_i)
        pair_lane = lax.broadcasted_iota(jnp.int32, (1, LANES), 1)
        heads = []
        for hd in range(GLA_HEADS):
            pr = slice((hd // 2) * LANES, (hd // 2 + 1) * LANES)
            v_h = ops.gv[:, hd * GLA_DV:(hd + 1) * GLA_DV]
            k_h = jnp.where((pair_lane // GLA_DK) == (hd % 2), ops.kd[:, pr], jnp.zeros((), BF16))
            s = jnp.where(causal, _dot_nt(ops.qs[hd // 2], k_h), 0.0).astype(BF16)
            u_all = _dot_tn(ops.kb[hd], v_h)
            heads.append((v_h, s, u_all))
            if hd % 2 == 1:
                yield

        staged = []
        for hd, (v_h, s, u_all) in enumerate(heads):
            st = state_ref[hd]
            entering = []
            for c in range(n_chunks):
                entering.append(st)
                last = c * GLA_CHUNK + GLA_CHUNK - 1
                d_col = ops.dec[hd * GLA_DK:(hd + 1) * GLA_DK, last:last + 1]
                st = st * d_col + u_all[c * GLA_DK:(c + 1) * GLA_DK, :]
            state_ref[hd] = st
            s_stack = jnp.concatenate(entering, axis=0).astype(BF16)
            staged.append((s, v_h, s_stack))
        yield

        outs = []
        for hd, (s, v_h, s_stack) in enumerate(staged):
            outs.append(_dot(s, v_h) + _dot(ops.qb[hd], s_stack))
            if hd % 2 == 1:
                yield

        g_gla = g_gla_ref[...]
        for hd, o_h in enumerate(outs):
            vs = slice(hd * GLA_DV, (hd + 1) * GLA_DV)
            o_ms = jnp.mean(o_h * o_h, axis=-1, keepdims=True)
            o_n = o_h * lax.rsqrt(o_ms + RMS_EPS) * g_gla[:, vs]
            mixed.append((o_n * ops.gg[:, vs].astype(F32)).astype(BF16))

    def swa(ops, mixed, first_tile):
        gq_lanes = SWA_GROUP * SWA_BLOCK
        kj = lax.broadcasted_iota(jnp.int32, (2 * SWA_BLOCK, gq_lanes), 0)
        ql = lax.broadcasted_iota(jnp.int32, (2 * SWA_BLOCK, gq_lanes), 1)
        dist = (ql % SWA_BLOCK) + SWA_BLOCK - kj
        valid = (dist >= 0) & (dist < WINDOW)
        head_of_lane = lax.broadcasted_iota(jnp.int32, (1, gq_lanes), 1) // SWA_BLOCK

        scores = []
        for u, (n, kv) in enumerate(units):
            bs = slice(n * SWA_BLOCK, (n + 2) * SWA_BLOCK)
            scores.append(_dot(ops.kk[kv, bs, :], ops.qt[u]))
            if u % 2 == 1:
                yield

        probs = []
        for (n, kv), s_raw in zip(units, scores):
            if n == 0 and first_tile is not None:
                ok = valid & ((kj >= SWA_BLOCK) | jnp.logical_not(first_tile))
            else:
                ok = valid
            s_t = s_raw + jnp.where(ok, 0.0, NEG_BIG)
            sink = jnp.zeros((1, gq_lanes), F32)
            for j in range(SWA_GROUP):
                sink = jnp.where(head_of_lane == j, sinks_ref[kv * SWA_GROUP + j] * LOG2E, sink)
            m = jnp.maximum(jnp.max(s_t, axis=0, keepdims=True), sink)
            e = jnp.exp2(s_t - m)
            den = jnp.sum(e, axis=0, keepdims=True) + jnp.exp2(sink - m)
            probs.append((e.astype(BF16), 1.0 / den))
            yield

        outs = []
        for (n, kv), (e, inv_den) in zip(units, probs):
            bs = slice(n * SWA_BLOCK, (n + 2) * SWA_BLOCK)
            outs.append(_dot_tn(ops.vv[kv, bs, :], e) * inv_den)
        yield

        for n in range(n_blocks):
            rs = slice(n * SWA_BLOCK, (n + 1) * SWA_BLOCK)
            o_groups = []
            for kv in range(SWA_KV_HEADS):
                o_t = outs[n * SWA_KV_HEADS + kv]
                o_stack = jnp.concatenate([o_t[:, j * SWA_BLOCK:(j + 1) * SWA_BLOCK] for j in range(SWA_GROUP)],
                                          axis=0)
                o_groups.append(jnp.transpose(o_stack))
            o_swa = jnp.concatenate(o_groups, axis=1)
            mixed.append((o_swa * ops.sg[rs, :].astype(F32)).astype(BF16))

    def finish(gla_heads, swa_blocks, rows):
        cat = jnp.concatenate([jnp.concatenate(gla_heads, axis=1), jnp.concatenate(swa_blocks, axis=0)], axis=1)
        y = _dot(cat, w_out_ref[...])
        yield
        x = x_ref[0, rows, :]
        xo = x + gate * y
        ms_o = jnp.mean(xo * xo, axis=-1, keepdims=True)
        o_ref[0, rows, :] = xo * lax.rsqrt(ms_o + RMS_EPS) * g_fin_ref[...]

    def interleave(primary, filler):
        live = list(primary)
        while live:
            for g in list(live):
                if next(g, StopIteration) is StopIteration:
                    live.remove(g)
                next(filler, None)
        for _ in filler:
            pass

    rows = [slice(k * ts, (k + 1) * ts) for k in range(STEP_TILES)]
    ops = (ops0, ops1)
    mixed = [([], []) for _ in range(STEP_TILES)]

    @pl.when(i == 0)
    def _():
        state_ref[...] = jnp.zeros_like(state_ref)

    @pl.when((i == 0) & (b_idx == 0))
    def _():
        for _ in prepare(x_ref[0, rows[0], :], pos_ref[0, :, rows[0]], norm_modulation(mod_ref), ops0, None, True):
            pass

    for k in range(STEP_TILES):
        cur, nxt = ops[k % 2], ops[(k + 1) % 2]
        primary = [gla(cur, mixed[k][0]), swa(cur, mixed[k][1], (i == 0) if k == 0 else None)]
        if k > 0:
            primary.insert(0, finish(*mixed[k - 1], rows[k - 1]))
        if k + 1 < STEP_TILES:
            filler = prepare(x_ref[0, rows[k + 1], :], pos_ref[0, :, rows[k + 1]], norm_modulation(mod_ref),
                             nxt, cur, False)
        else:
            filler = prepare(xn_ref[0], posn_ref[0], norm_modulation(modn_ref), nxt, cur, last_step)
        interleave(primary, filler)
    for _ in finish(*mixed[STEP_TILES - 1], rows[STEP_TILES - 1]):
        pass


def _chunk_matrices(ts):
    r = np.arange(ts)
    same = (r[:, None] // GLA_CHUNK) == (r[None, :] // GLA_CHUNK)
    return jnp.asarray(same & (r[None, :] <= r[:, None]), BF16), jnp.asarray(same, BF16)


def _rope_inv_freq_column():
    inv = 1.0 / (ROPE_THETA ** (jnp.arange(0, SWA_HEAD_DIM, 2, dtype=F32) / SWA_HEAD_DIM))
    return inv.reshape(SWA_HEAD_DIM // 2, 1)


def kernel(x, c, positions, w_ada, b_ada, g_norm, w_in, w_decay, b_decay, g_gla_head, sinks, w_out, g_final):
    B, S, D = x.shape
    ts = SEQ_TILE
    assert D == D_MODEL and S % (STEP_TILES * ts) == 0 and ts % SWA_BLOCK == 0 and STEP_TILES % 2 == 0
    assert w_ada.shape[0] == 1, "one layer"
    n_tiles = S // ts

    mod = _adaln(c.astype(F32), w_ada[0], b_ada[0][None, :]).reshape(B, 1, 3 * D)

    w_in_b, w_out_b = _prepare_weights(jnp.transpose(w_in[0]), w_out[0])
    w_dec = jnp.pad(w_decay[0].astype(BF16), ((0, LANES - GLA_GATE_RANK), (0, 0)))
    tri, blk = _chunk_matrices(ts)
    pos3 = positions.reshape(B, 1, S)

    const2 = lambda b, i, s: (0, 0)
    full = lambda a: pl.BlockSpec(a.shape, const2)
    n_steps = n_tiles // STEP_TILES

    def next_row_tile(b, i):
        wraps = i == n_steps - 1
        at_end = wraps & (b == B - 1)
        row = jnp.where(wraps & jnp.logical_not(at_end), b + 1, b)
        tile = jnp.where(wraps, jnp.where(at_end, n_tiles - 1, 0), STEP_TILES * (i + 1))
        return row, tile

    operands = [
        x, x, pos3, pos3,
        mod, mod,
        g_norm[0][None, :],
        _rope_inv_freq_column(),
        tri, blk,
        w_in_b,
        w_dec, b_decay[0][None, :], g_gla_head[0][None, :],
        w_out_b, g_final[None, :],
    ]
    in_specs = [
        pl.BlockSpec((1, STEP_TILES * ts, D), lambda b, i, s: (b, i, 0)),
        pl.BlockSpec((1, ts, D), lambda b, i, s: next_row_tile(b, i) + (0,)),
        pl.BlockSpec((1, 1, STEP_TILES * ts), lambda b, i, s: (b, 0, i)),
        pl.BlockSpec((1, 1, ts), lambda b, i, s: (next_row_tile(b, i)[0], 0, next_row_tile(b, i)[1])),
        pl.BlockSpec((1, 1, 3 * D), lambda b, i, s: (b, 0, 0)),
        pl.BlockSpec((1, 1, 3 * D), lambda b, i, s: (next_row_tile(b, i)[0], 0, 0)),
    ] + [full(a) for a in operands[6:]]

    operand_scratch = [pltpu.VMEM(shape, dtype) for _, shape, dtype in _OPERAND_BUFFERS]
    grid_spec = pltpu.PrefetchScalarGridSpec(
        num_scalar_prefetch=1,
        grid=(B, n_steps),
        in_specs=in_specs,
        out_specs=pl.BlockSpec((1, STEP_TILES * ts, D), lambda b, i, s: (b, i, 0)),
        scratch_shapes=[
            pltpu.VMEM((GLA_HEADS, GLA_DK, GLA_DV), F32),
        ] + operand_scratch + operand_scratch,
    )
    return pl.pallas_call(
        _layer_body,
        grid_spec=grid_spec,
        out_shape=jax.ShapeDtypeStruct((B, S, D), x.dtype),
        compiler_params=pltpu.CompilerParams(
            dimension_semantics=("arbitrary", "arbitrary"),
            vmem_limit_bytes=VMEM_LIMIT_BYTES,
        ),
        name="hymba_layer",
    )(sinks[0].astype(F32), *operands)
```

```python
import jax
import jax.numpy as jnp
import numpy as np
from jax import lax
from jax.experimental import pallas as pl
from jax.experimental.pallas import tpu as pltpu

D_MODEL = 1024
GLA_HEADS = 4
GLA_DK = 64
GLA_DV = 128
GLA_WIDTH = GLA_HEADS * GLA_DV
GLA_QK = GLA_HEADS * GLA_DK
GLA_GATE_RANK = 16
GLA_GATE_TAU = 16.0
GLA_CHUNK = 64
SWA_Q_HEADS = 8
SWA_KV_HEADS = 2
SWA_GROUP = SWA_Q_HEADS // SWA_KV_HEADS
SWA_HEAD_DIM = 64
SWA_WIDTH = SWA_Q_HEADS * SWA_HEAD_DIM
SWA_KV_WIDTH = SWA_KV_HEADS * SWA_HEAD_DIM
WINDOW = 128
SWA_BLOCK = 128
ROPE_THETA = 10000.0
RMS_EPS = 1e-6

LANES = 128
MXU_COLS = 256
PROJ_COLS = 2 * MXU_COLS
SEQ_TILE = 256
STEP_TILES = 2
LOG2E = 1.4426950408889634
NEG_BIG = -1e30
VMEM_LIMIT_BYTES = 40 * 1024 * 1024

W_GA = 0
W_SK = W_GA + LANES
W_SV = W_SK + SWA_KV_WIDTH
W_GQ = W_SV + SWA_KV_WIDTH
W_GK = W_GQ + GLA_QK
W_GV = W_GK + GLA_QK
W_SQ = W_GV + GLA_WIDTH
W_GZ = W_SQ + SWA_WIDTH
W_SZ = W_GZ + GLA_WIDTH
W_COLS = W_SZ + SWA_WIDTH

SWA_UNITS = (SEQ_TILE // SWA_BLOCK) * SWA_KV_HEADS
BF16_ROWS = 16
SWA_V_COLS = SWA_HEAD_DIM + BF16_ROWS

BF16 = jnp.bfloat16
F32 = jnp.float32


def _dot(a, b):
    return jnp.dot(a, b, preferred_element_type=F32)


def _dot_nt(a, b):
    return lax.dot_general(a, b, (((1,), (1,)), ((), ())), preferred_element_type=F32)


def _dot_tn(a, b):
    return lax.dot_general(a, b, (((0,), (0,)), ((), ())), preferred_element_type=F32)


def _silu(x):
    return x * (1.0 / (1.0 + jnp.exp2(x * (-LOG2E))))


def _log_sigmoid(z):
    return jnp.minimum(z, 0.0) - jnp.log(1.0 + jnp.exp2(jnp.abs(z) * (-LOG2E)))


def _adaln_body(c_ref, w_ref, b_ref, o_ref):
    a = _silu(c_ref[...]).astype(BF16)
    o_ref[...] = _dot(a, w_ref[...].astype(BF16)) + b_ref[...]


def _adaln(c, w_ada, b_ada):
    rows = c.shape[0]
    n_out = w_ada.shape[1]
    col_block = D_MODEL
    return pl.pallas_call(
        _adaln_body,
        grid=(n_out // col_block,),
        in_specs=[
            pl.BlockSpec((rows, D_MODEL), lambda j: (0, 0)),
            pl.BlockSpec((D_MODEL, col_block), lambda j: (0, j)),
            pl.BlockSpec((1, col_block), lambda j: (0, j)),
        ],
        out_specs=pl.BlockSpec((rows, col_block), lambda j: (0, j)),
        out_shape=jax.ShapeDtypeStruct((rows, n_out), F32),
        name="adaln_mod",
    )(c, w_ada, b_ada)


_SRC_GA = 2 * GLA_QK + GLA_WIDTH
_SRC_GZ = _SRC_GA + GLA_GATE_RANK
_SRC_SQ = _SRC_GZ + GLA_WIDTH
_SRC_SK = _SRC_SQ + SWA_WIDTH
_SRC_SV = _SRC_SK + SWA_KV_WIDTH
_SRC_SZ = _SRC_SV + SWA_KV_WIDTH
_SRC_ROWS = _SRC_SZ + SWA_WIDTH
_REGROUP = (
    (W_SK, _SRC_SK, SWA_KV_WIDTH), (W_SV, _SRC_SV, SWA_KV_WIDTH),
    (W_GQ, 0, GLA_QK), (W_GK, GLA_QK, GLA_QK), (W_GV, 2 * GLA_QK, GLA_WIDTH),
    (W_SQ, _SRC_SQ, SWA_WIDTH), (W_GZ, _SRC_GZ, GLA_WIDTH), (W_SZ, _SRC_SZ, SWA_WIDTH),
)
W_COL_BLOCK = 256


def _prepare_weights_body(wt_ref, w_out_ref, o_ref, w_out_o_ref):
    w_out_o_ref[...] = w_out_ref[...].astype(BF16)
    row = lax.broadcasted_iota(jnp.int32, (LANES, 1), 0)
    ga = jnp.where(row < GLA_GATE_RANK, wt_ref[_SRC_GA:_SRC_GA + LANES, :], 0.0)
    o_ref[:, W_GA:W_GA + LANES] = jnp.transpose(ga).astype(BF16)
    for dst, src, width in _REGROUP:
        o_ref[:, dst:dst + width] = jnp.transpose(wt_ref[src:src + width, :]).astype(BF16)


def _prepare_weights(wt, w_out):
    n, d = wt.shape
    k_out, n_out = w_out.shape
    assert n == _SRC_ROWS and d % W_COL_BLOCK == 0 and k_out % (d // W_COL_BLOCK) == 0
    steps = d // W_COL_BLOCK
    out_rows = k_out // steps
    return pl.pallas_call(
        _prepare_weights_body,
        grid=(steps,),
        in_specs=[pl.BlockSpec((n, W_COL_BLOCK), lambda r: (0, r)),
                  pl.BlockSpec((out_rows, n_out), lambda r: (r, 0))],
        out_specs=[pl.BlockSpec((W_COL_BLOCK, W_COLS), lambda r: (r, 0)),
                   pl.BlockSpec((out_rows, n_out), lambda r: (r, 0))],
        out_shape=[jax.ShapeDtypeStruct((d, W_COLS), BF16), jax.ShapeDtypeStruct((k_out, n_out), BF16)],
        name="prepare_weights",
    )(wt, w_out)


_OPERAND_BUFFERS = (
    ("qs", (GLA_HEADS // 2, SEQ_TILE, LANES), BF16),
    ("qb", (GLA_HEADS, SEQ_TILE, SEQ_TILE), BF16),
    ("kb", (GLA_HEADS, SEQ_TILE, SEQ_TILE), BF16),
    ("kd", (SEQ_TILE, GLA_QK), BF16),
    ("gv", (SEQ_TILE, GLA_WIDTH), BF16),
    ("dec", (GLA_QK, SEQ_TILE), F32),
    ("gg", (SEQ_TILE, GLA_WIDTH), BF16),
    ("sg", (SEQ_TILE, SWA_WIDTH), BF16),
    ("qt", (SWA_UNITS, SWA_HEAD_DIM, SWA_GROUP * SWA_BLOCK), BF16),
    ("kk", (SWA_KV_HEADS, SEQ_TILE + SWA_BLOCK, SWA_HEAD_DIM), BF16),
    ("vv", (SWA_KV_HEADS, SEQ_TILE + SWA_BLOCK, SWA_V_COLS), BF16),
)


class _Operands:
    def __init__(self, refs):
        for (name, _, _), ref in zip(_OPERAND_BUFFERS, refs):
            setattr(self, name, ref)


def _layer_body(sinks_ref, x_ref, xn_ref, pos_ref, posn_ref, mod_ref, modn_ref, gnorm_ref, invf_ref, tri_ref, blk_ref,
                w_in_ref, w_dec_ref, b_dec_ref, g_gla_ref, w_out_ref, g_fin_ref,
                o_ref, state_ref, *operand_refs):
    ts = SEQ_TILE
    b_idx = pl.program_id(0)
    i = pl.program_id(1)
    last_step = i == pl.num_programs(1) - 1
    nbuf = len(_OPERAND_BUFFERS)
    ops0 = _Operands(operand_refs[:nbuf])
    ops1 = _Operands(operand_refs[nbuf:])

    gate = mod_ref[0, :, 2 * D_MODEL:3 * D_MODEL]

    def norm_modulation(m_ref):
        return (gnorm_ref[...] * (1.0 + m_ref[0, :, D_MODEL:2 * D_MODEL]), m_ref[0, :, 0:D_MODEL])

    n_chunks = ts // GLA_CHUNK
    n_blocks = ts // SWA_BLOCK
    units = [(n, kv) for n in range(n_blocks) for kv in range(SWA_KV_HEADS)]

    def proj(hb, c0, width):
        return _dot(hb, w_in_ref[:, c0:c0 + width])

    def prepare(x, pos, modulation, ops, prev_ops, starts_sequence):
        norm_gain, shift = modulation
        ms = jnp.mean(x * x, axis=-1, keepdims=True)
        hb = ((x * lax.rsqrt(ms + RMS_EPS)) * norm_gain + shift).astype(BF16)
        yield

        d0 = proj(hb, W_GA, LANES + 2 * SWA_KV_WIDTH)
        yield

        ang_t = invf_ref[...] * pos.astype(F32)
        reps = LANES // (SWA_HEAD_DIM // 2)
        cos_f = jnp.transpose(jnp.concatenate([jnp.cos(ang_t)] * reps, axis=0))
        sin_f = jnp.transpose(jnp.concatenate([jnp.sin(ang_t)] * reps, axis=0))
        lane = lax.broadcasted_iota(jnp.int32, (1, LANES), 1)
        first_half = (lane % SWA_HEAD_DIM) < (SWA_HEAD_DIM // 2)
        sin_s = jnp.where(first_half, -sin_f, sin_f)

        def rope(tv):
            rot = jnp.where(first_half,
                            pltpu.roll(tv, LANES - SWA_HEAD_DIM // 2, axis=1),
                            pltpu.roll(tv, SWA_HEAD_DIM // 2, axis=1))
            return tv * cos_f + rot * sin_s
        yield

        z = _dot(d0[:, 0:LANES].astype(BF16), w_dec_ref[...]) + b_dec_ref[...]
        k_rot = rope(d0[:, LANES:LANES + SWA_KV_WIDTH]).astype(BF16)
        ones_col = (lax.broadcasted_iota(jnp.int32, (ts, BF16_ROWS), 1) == 0).astype(BF16)
        for kv in range(SWA_KV_HEADS):
            hs = slice(kv * SWA_HEAD_DIM, (kv + 1) * SWA_HEAD_DIM)
            v0 = LANES + SWA_KV_WIDTH + kv * SWA_HEAD_DIM
            ops.kk[kv, SWA_BLOCK:SWA_BLOCK + ts, :] = k_rot[:, hs]
            ops.vv[kv, SWA_BLOCK:SWA_BLOCK + ts, 0:SWA_HEAD_DIM] = d0[:, v0:v0 + SWA_HEAD_DIM].astype(BF16)
            ops.vv[kv, SWA_BLOCK:SWA_BLOCK + ts, SWA_HEAD_DIM:] = ones_col
            zeros = jnp.zeros((SWA_BLOCK, SWA_HEAD_DIM), BF16)
            v_zeros = jnp.zeros((SWA_BLOCK, SWA_V_COLS), BF16)
            if starts_sequence is True:
                k_carry, v_carry = zeros, v_zeros
            else:
                k_carry = prev_ops.kk[kv, ts:ts + SWA_BLOCK, :]
                v_carry = prev_ops.vv[kv, ts:ts + SWA_BLOCK, :]
                if starts_sequence is not False:
                    k_carry = jnp.where(starts_sequence, zeros, k_carry)
                    v_carry = jnp.where(starts_sequence, v_zeros, v_carry)
            ops.kk[kv, 0:SWA_BLOCK, :] = k_carry
            ops.vv[kv, 0:SWA_BLOCK, :] = v_carry
        yield

        log_a = _log_sigmoid(z) * (1.0 / GLA_GATE_TAU)
        la_hi = log_a.astype(BF16)
        la_lo = (log_a - la_hi.astype(F32)).astype(BF16)
        b2 = _dot(tri_ref[...], jnp.concatenate([la_hi, la_lo], axis=1))
        gqk = proj(hb, W_GQ, 2 * GLA_QK)
        gq = gqk[:, :GLA_QK]
        gk = gqk[:, GLA_QK:]
        yield
        b = b2[:, :GLA_QK] + b2[:, GLA_QK:]
        e_b = jnp.exp(b)
        e_last = jnp.concatenate(
            [jnp.broadcast_to(e_b[(c + 1) * GLA_CHUNK - 1:(c + 1) * GLA_CHUNK, :], (GLA_CHUNK, GLA_QK))
             for c in range(n_chunks)], axis=0)
        ops.dec[...] = jnp.transpose(e_last)
        q_d = gq * (GLA_DK ** -0.5) * e_b
        yield
        k_d = gk * (1.0 / e_b)
        ops.kd[...] = k_d.astype(BF16)
        k_tail = k_d * e_last

        blk = blk_ref[...]
        low_half = lax.broadcasted_iota(jnp.int32, (1, LANES), 1) < GLA_DK
        tile_reps = ts // LANES
        for pair in range(GLA_HEADS // 2):
            pr = slice(pair * LANES, (pair + 1) * LANES)
            q_pair, k_pair = q_d[:, pr], k_tail[:, pr]
            q_swap = pltpu.roll(q_pair, GLA_DK, axis=1)
            k_swap = pltpu.roll(k_pair, GLA_DK, axis=1)
            ops.qs[pair] = q_pair.astype(BF16)
            for j in range(2):
                a, b_ = (q_pair, q_swap) if j == 0 else (q_swap, q_pair)
                c, d_ = (k_pair, k_swap) if j == 0 else (k_swap, k_pair)
                q_dup = jnp.where(low_half, a, b_).astype(BF16)
                k_dup = jnp.where(low_half, c, d_).astype(BF16)
                ops.qb[2 * pair + j] = jnp.concatenate([q_dup] * tile_reps, axis=1) * blk
                ops.kb[2 * pair + j] = jnp.concatenate([k_dup] * tile_reps, axis=1) * blk
            yield

        for c0 in range(0, GLA_WIDTH, PROJ_COLS):
            ops.gv[:, c0:c0 + PROJ_COLS] = proj(hb, W_GV + c0, PROJ_COLS).astype(BF16)
            yield

        sq_all = proj(hb, W_SQ, SWA_WIDTH)
        for kv in range(SWA_KV_HEADS):
            sq = sq_all[:, kv * MXU_COLS:(kv + 1) * MXU_COLS]
            q_rot = jnp.concatenate([rope(sq[:, j * LANES:(j + 1) * LANES]) for j in range(MXU_COLS // LANES)],
                                    axis=1) * (SWA_HEAD_DIM ** -0.5 * LOG2E)
            for n in range(n_blocks):
                q_t = jnp.transpose(q_rot[n * SWA_BLOCK:(n + 1) * SWA_BLOCK, :])
                ops.qt[n * SWA_KV_HEADS + kv] = jnp.concatenate(
                    [q_t[j * SWA_HEAD_DIM:(j + 1) * SWA_HEAD_DIM, :] for j in range(SWA_GROUP)],
                    axis=1).astype(BF16)
            yield

        for c0 in range(0, GLA_WIDTH, PROJ_COLS):
            ops.gg[:, c0:c0 + PROJ_COLS] = _silu(proj(hb, W_GZ + c0, PROJ_COLS)).astype(BF16)
            yield
        for c0 in range(0, SWA_WIDTH, PROJ_COLS):
            ops.sg[:, c0:c0 + PROJ_COLS] = _silu(proj(hb, W_SZ + c0, PROJ_COLS)).astype(BF16)
            yield

    def gla(ops, mixed):
        row_i = lax.broadcasted_iota(jnp.int32, (ts, ts), 0)
        col_i = lax.broadcasted_iota(jnp.int32, (ts, ts), 1)
        causal = ((row_i // GLA_CHUNK) == (col_i // GLA_CHUNK)) & (col_i <= row_i)
        pair_lane = lax.broadcasted_iota(jnp.int32, (1, LANES), 1)
        heads = []
        for hd in range(GLA_HEADS):
            pr = slice((hd // 2) * LANES, (hd // 2 + 1) * LANES)
            v_h = ops.gv[:, hd * GLA_DV:(hd + 1) * GLA_DV]
            k_h = jnp.where((pair_lane // GLA_DK) == (hd % 2), ops.kd[:, pr], jnp.zeros((), BF16))
            s = jnp.where(causal, _dot_nt(ops.qs[hd // 2], k_h), 0.0).astype(BF16)
            u_all = _dot_tn(ops.kb[hd], v_h)
            heads.append((v_h, s, u_all))
            if hd % 2 == 1:
                yield

        staged = []
        for hd, (v_h, s, u_all) in enumerate(heads):
            st = state_ref[hd]
            entering = []
            for c in range(n_chunks):
                entering.append(st)
                last = c * GLA_CHUNK + GLA_CHUNK - 1
                d_col = ops.dec[hd * GLA_DK:(hd + 1) * GLA_DK, last:last + 1]
                st = st * d_col + u_all[c * GLA_DK:(c + 1) * GLA_DK, :]
            state_ref[hd] = st
            s_stack = jnp.concatenate(entering, axis=0).astype(BF16)
            staged.append((s, v_h, s_stack))
        yield

        outs = []
        for hd, (s, v_h, s_stack) in enumerate(staged):
            outs.append(_dot(s, v_h) + _dot(ops.qb[hd], s_stack))
            if hd % 2 == 1:
                yield

        g_gla = g_gla_ref[...]
        for hd, o_h in enumerate(outs):
            vs = slice(hd * GLA_DV, (hd + 1) * GLA_DV)
            o_ms = jnp.mean(o_h * o_h, axis=-1, keepdims=True)
            o_n = o_h * lax.rsqrt(o_ms + RMS_EPS) * g_gla[:, vs]
            mixed.append((o_n * ops.gg[:, vs].astype(F32)).astype(BF16))

    def swa(ops, mixed, first_tile):
        gq_lanes = SWA_GROUP * SWA_BLOCK
        kj = lax.broadcasted_iota(jnp.int32, (2 * SWA_BLOCK, gq_lanes), 0)
        ql = lax.broadcasted_iota(jnp.int32, (2 * SWA_BLOCK, gq_lanes), 1)
        dist = (ql % SWA_BLOCK) + SWA_BLOCK - kj
        valid = (dist >= 0) & (dist < WINDOW)
        head_of_lane = lax.broadcasted_iota(jnp.int32, (1, gq_lanes), 1) // SWA_BLOCK

        scores = []
        for u, (n, kv) in enumerate(units):
            bs = slice(n * SWA_BLOCK, (n + 2) * SWA_BLOCK)
            scores.append(_dot(ops.kk[kv, bs, :], ops.qt[u]))
            if u % 2 == 1:
                yield

        probs = []
        for (n, kv), s_raw in zip(units, scores):
            if n == 0 and first_tile is not None:
                ok = valid & ((kj >= SWA_BLOCK) | jnp.logical_not(first_tile))
            else:
                ok = valid
            s_t = s_raw + jnp.where(ok, 0.0, NEG_BIG)
            sink = jnp.zeros((1, gq_lanes), F32)
            for j in range(SWA_GROUP):
                sink = jnp.where(head_of_lane == j, sinks_ref[kv * SWA_GROUP + j] * LOG2E, sink)
            m = jnp.maximum(jnp.max(s_t, axis=0, keepdims=True), sink)
            e = jnp.exp2(s_t - m)
            probs.append((e.astype(BF16), jnp.exp2(sink - m)))
            yield

        outs = []
        for (n, kv), (e, e_sink) in zip(units, probs):
            bs = slice(n * SWA_BLOCK, (n + 2) * SWA_BLOCK)
            o_ext = _dot_tn(ops.vv[kv, bs, :], e)
            den = o_ext[SWA_HEAD_DIM:SWA_HEAD_DIM + 1, :] + e_sink
            outs.append(o_ext[0:SWA_HEAD_DIM, :] * (1.0 / den))
        yield

        for n in range(n_blocks):
            rs = slice(n * SWA_BLOCK, (n + 1) * SWA_BLOCK)
            o_groups = []
            for kv in range(SWA_KV_HEADS):
                o_t = outs[n * SWA_KV_HEADS + kv]
                o_stack = jnp.concatenate([o_t[:, j * SWA_BLOCK:(j + 1) * SWA_BLOCK] for j in range(SWA_GROUP)],
                                          axis=0)
                o_groups.append(jnp.transpose(o_stack))
            o_swa = jnp.concatenate(o_groups, axis=1)
            mixed.append((o_swa * ops.sg[rs, :].astype(F32)).astype(BF16))

    def finish(gla_heads, swa_blocks, rows):
        cat = jnp.concatenate([jnp.concatenate(gla_heads, axis=1), jnp.concatenate(swa_blocks, axis=0)], axis=1)
        y = _dot(cat, w_out_ref[...])
        yield
        x = x_ref[0, rows, :]
        xo = x + gate * y
        ms_o = jnp.mean(xo * xo, axis=-1, keepdims=True)
        o_ref[0, rows, :] = xo * lax.rsqrt(ms_o + RMS_EPS) * g_fin_ref[...]

    def interleave(primary, filler):
        live = list(primary)
        while live:
            for g in list(live):
                if next(g, StopIteration) is StopIteration:
                    live.remove(g)
                next(filler, None)
        for _ in filler:
            pass

    rows = [slice(k * ts, (k + 1) * ts) for k in range(STEP_TILES)]
    ops = (ops0, ops1)
    mixed = [([], []) for _ in range(STEP_TILES)]

    @pl.when(i == 0)
    def _():
        state_ref[...] = jnp.zeros_like(state_ref)

    @pl.when((i == 0) & (b_idx == 0))
    def _():
        for _ in prepare(x_ref[0, rows[0], :], pos_ref[0, :, rows[0]], norm_modulation(mod_ref), ops0, None, True):
            pass

    for k in range(STEP_TILES):
        cur, nxt = ops[k % 2], ops[(k + 1) % 2]
        primary = [gla(cur, mixed[k][0]), swa(cur, mixed[k][1], (i == 0) if k == 0 else None)]
        if k > 0:
            primary.insert(0, finish(*mixed[k - 1], rows[k - 1]))
        if k + 1 < STEP_TILES:
            filler = prepare(x_ref[0, rows[k + 1], :], pos_ref[0, :, rows[k + 1]], norm_modulation(mod_ref),
                             nxt, cur, False)
        else:
            filler = prepare(xn_ref[0], posn_ref[0], norm_modulation(modn_ref), nxt, cur, last_step)
        interleave(primary, filler)
    for _ in finish(*mixed[STEP_TILES - 1], rows[STEP_TILES - 1]):
        pass


def _chunk_matrices(ts):
    r = np.arange(ts)
    same = (r[:, None] // GLA_CHUNK) == (r[None, :] // GLA_CHUNK)
    return jnp.asarray(same & (r[None, :] <= r[:, None]), BF16), jnp.asarray(same, BF16)


def _rope_inv_freq_column():
    inv = 1.0 / (ROPE_THETA ** (jnp.arange(0, SWA_HEAD_DIM, 2, dtype=F32) / SWA_HEAD_DIM))
    return inv.reshape(SWA_HEAD_DIM // 2, 1)


def kernel(x, c, positions, w_ada, b_ada, g_norm, w_in, w_decay, b_decay, g_gla_head, sinks, w_out, g_final):
    B, S, D = x.shape
    ts = SEQ_TILE
    assert D == D_MODEL and S % (STEP_TILES * ts) == 0 and ts % SWA_BLOCK == 0 and STEP_TILES % 2 == 0
    assert w_ada.shape[0] == 1, "one layer"
    n_tiles = S // ts

    mod = _adaln(c.astype(F32), w_ada[0], b_ada[0][None, :]).reshape(B, 1, 3 * D)

    w_in_b, w_out_b = _prepare_weights(jnp.transpose(w_in[0]), w_out[0])
    w_dec = jnp.pad(w_decay[0].astype(BF16), ((0, LANES - GLA_GATE_RANK), (0, 0)))
    tri, blk = _chunk_matrices(ts)
    pos3 = positions.reshape(B, 1, S)

    const2 = lambda b, i, s: (0, 0)
    full = lambda a: pl.BlockSpec(a.shape, const2)
    n_steps = n_tiles // STEP_TILES

    def next_row_tile(b, i):
        wraps = i == n_steps - 1
        at_end = wraps & (b == B - 1)
        row = jnp.where(wraps & jnp.logical_not(at_end), b + 1, b)
        tile = jnp.where(wraps, jnp.where(at_end, n_tiles - 1, 0), STEP_TILES * (i + 1))
        return row, tile

    operands = [
        x, x, pos3, pos3,
        mod, mod,
        g_norm[0][None, :],
        _rope_inv_freq_column(),
        tri, blk,
        w_in_b,
        w_dec, b_decay[0][None, :], g_gla_head[0][None, :],
        w_out_b, g_final[None, :],
    ]
    in_specs = [
        pl.BlockSpec((1, STEP_TILES * ts, D), lambda b, i, s: (b, i, 0)),
        pl.BlockSpec((1, ts, D), lambda b, i, s: next_row_tile(b, i) + (0,)),
        pl.BlockSpec((1, 1, STEP_TILES * ts), lambda b, i, s: (b, 0, i)),
        pl.BlockSpec((1, 1, ts), lambda b, i, s: (next_row_tile(b, i)[0], 0, next_row_tile(b, i)[1])),
        pl.BlockSpec((1, 1, 3 * D), lambda b, i, s: (b, 0, 0)),
        pl.BlockSpec((1, 1, 3 * D), lambda b, i, s: (next_row_tile(b, i)[0], 0, 0)),
    ] + [full(a) for a in operands[6:]]

    operand_scratch = [pltpu.VMEM(shape, dtype) for _, shape, dtype in _OPERAND_BUFFERS]
    grid_spec = pltpu.PrefetchScalarGridSpec(
        num_scalar_prefetch=1,
        grid=(B, n_steps),
        in_specs=in_specs,
        out_specs=pl.BlockSpec((1, STEP_TILES * ts, D), lambda b, i, s: (b, i, 0)),
        scratch_shapes=[
            pltpu.VMEM((GLA_HEADS, GLA_DK, GLA_DV), F32),
        ] + operand_scratch + operand_scratch,
    )
    return pl.pallas_call(
        _layer_body,
        grid_spec=grid_spec,
        out_shape=jax.ShapeDtypeStruct((B, S, D), x.dtype),
        compiler_params=pltpu.CompilerParams(
            dimension_semantics=("arbitrary", "arbitrary"),
            vmem_limit_bytes=VMEM_LIMIT_BYTES,
        ),
        name="hymba_layer",
    )(sinks[0].astype(F32), *operands)
```

```python
import jax
import jax.numpy as jnp
import numpy as np
from jax import lax
from jax.experimental import pallas as pl
from jax.experimental.pallas import tpu as pltpu

D_MODEL = 1024
GLA_HEADS = 4
GLA_DK = 64
GLA_DV = 128
GLA_WIDTH = GLA_HEADS * GLA_DV
GLA_QK = GLA_HEADS * GLA_DK
GLA_GATE_RANK = 16
GLA_GATE_TAU = 16.0
GLA_CHUNK = 64
SWA_Q_HEADS = 8
SWA_KV_HEADS = 2
SWA_GROUP = SWA_Q_HEADS // SWA_KV_HEADS
SWA_HEAD_DIM = 64
SWA_WIDTH = SWA_Q_HEADS * SWA_HEAD_DIM
SWA_KV_WIDTH = SWA_KV_HEADS * SWA_HEAD_DIM
WINDOW = 128
SWA_BLOCK = 128
ROPE_THETA = 10000.0
RMS_EPS = 1e-6

LANES = 128
MXU_COLS = 256
PROJ_COLS = 2 * MXU_COLS
SEQ_TILE = 256
STEP_TILES = 2
LOG2E = 1.4426950408889634
NEG_BIG = -1e30
VMEM_LIMIT_BYTES = 40 * 1024 * 1024

W_GA = 0
W_SK = W_GA + LANES
W_SV = W_SK + SWA_KV_WIDTH
W_GQ = W_SV + SWA_KV_WIDTH
W_GK = W_GQ + GLA_QK
W_GV = W_GK + GLA_QK
W_SQ = W_GV + GLA_WIDTH
W_GZ = W_SQ + SWA_WIDTH
W_SZ = W_GZ + GLA_WIDTH
W_COLS = W_SZ + SWA_WIDTH

SWA_UNITS = (SEQ_TILE // SWA_BLOCK) * SWA_KV_HEADS

BF16 = jnp.bfloat16
F32 = jnp.float32


def _dot(a, b):
    return jnp.dot(a, b, preferred_element_type=F32)


def _dot_nt(a, b):
    return lax.dot_general(a, b, (((1,), (1,)), ((), ())), preferred_element_type=F32)


def _dot_tn(a, b):
    return lax.dot_general(a, b, (((0,), (0,)), ((), ())), preferred_element_type=F32)


def _silu(x):
    return x * (1.0 / (1.0 + jnp.exp2(x * (-LOG2E))))


def _log_sigmoid(z):
    return jnp.minimum(z, 0.0) - jnp.log(1.0 + jnp.exp2(jnp.abs(z) * (-LOG2E)))


def _adaln_body(c_ref, w_ref, b_ref, o_ref):
    a = _silu(c_ref[...]).astype(BF16)
    o_ref[...] = _dot(a, w_ref[...].astype(BF16)) + b_ref[...]


def _adaln(c, w_ada, b_ada):
    rows = c.shape[0]
    n_out = w_ada.shape[1]
    col_block = D_MODEL
    return pl.pallas_call(
        _adaln_body,
        grid=(n_out // col_block,),
        in_specs=[
            pl.BlockSpec((rows, D_MODEL), lambda j: (0, 0)),
            pl.BlockSpec((D_MODEL, col_block), lambda j: (0, j)),
            pl.BlockSpec((1, col_block), lambda j: (0, j)),
        ],
        out_specs=pl.BlockSpec((rows, col_block), lambda j: (0, j)),
        out_shape=jax.ShapeDtypeStruct((rows, n_out), F32),
        name="adaln_mod",
    )(c, w_ada, b_ada)


_SRC_GA = 2 * GLA_QK + GLA_WIDTH
_SRC_GZ = _SRC_GA + GLA_GATE_RANK
_SRC_SQ = _SRC_GZ + GLA_WIDTH
_SRC_SK = _SRC_SQ + SWA_WIDTH
_SRC_SV = _SRC_SK + SWA_KV_WIDTH
_SRC_SZ = _SRC_SV + SWA_KV_WIDTH
_SRC_ROWS = _SRC_SZ + SWA_WIDTH
_REGROUP = (
    (W_SK, _SRC_SK, SWA_KV_WIDTH), (W_SV, _SRC_SV, SWA_KV_WIDTH),
    (W_GQ, 0, GLA_QK), (W_GK, GLA_QK, GLA_QK), (W_GV, 2 * GLA_QK, GLA_WIDTH),
    (W_SQ, _SRC_SQ, SWA_WIDTH), (W_GZ, _SRC_GZ, GLA_WIDTH), (W_SZ, _SRC_SZ, SWA_WIDTH),
)
W_COL_BLOCK = 256


def _prepare_weights_body(wt_ref, w_out_ref, o_ref, w_out_o_ref):
    w_out_o_ref[...] = w_out_ref[...].astype(BF16)
    row = lax.broadcasted_iota(jnp.int32, (LANES, 1), 0)
    ga = jnp.where(row < GLA_GATE_RANK, wt_ref[_SRC_GA:_SRC_GA + LANES, :], 0.0)
    o_ref[:, W_GA:W_GA + LANES] = jnp.transpose(ga).astype(BF16)
    for dst, src, width in _REGROUP:
        o_ref[:, dst:dst + width] = jnp.transpose(wt_ref[src:src + width, :]).astype(BF16)


def _prepare_weights(wt, w_out):
    n, d = wt.shape
    k_out, n_out = w_out.shape
    assert n == _SRC_ROWS and d % W_COL_BLOCK == 0 and k_out % (d // W_COL_BLOCK) == 0
    steps = d // W_COL_BLOCK
    out_rows = k_out // steps
    return pl.pallas_call(
        _prepare_weights_body,
        grid=(steps,),
        in_specs=[pl.BlockSpec((n, W_COL_BLOCK), lambda r: (0, r)),
                  pl.BlockSpec((out_rows, n_out), lambda r: (r, 0))],
        out_specs=[pl.BlockSpec((W_COL_BLOCK, W_COLS), lambda r: (r, 0)),
                   pl.BlockSpec((out_rows, n_out), lambda r: (r, 0))],
        out_shape=[jax.ShapeDtypeStruct((d, W_COLS), BF16), jax.ShapeDtypeStruct((k_out, n_out), BF16)],
        name="prepare_weights",
    )(wt, w_out)


_OPERAND_BUFFERS = (
    ("qs", (GLA_HEADS // 2, SEQ_TILE, LANES), BF16),
    ("qb", (GLA_HEADS, SEQ_TILE, SEQ_TILE), BF16),
    ("kb", (GLA_HEADS, SEQ_TILE, SEQ_TILE), BF16),
    ("kd", (SEQ_TILE, GLA_QK), BF16),
    ("gv", (SEQ_TILE, GLA_WIDTH), BF16),
    ("dec", (GLA_QK, SEQ_TILE), F32),
    ("gg", (SEQ_TILE, GLA_WIDTH), BF16),
    ("sg", (SEQ_TILE, SWA_WIDTH), BF16),
    ("qt", (SWA_UNITS, SWA_HEAD_DIM, SWA_GROUP * SWA_BLOCK), BF16),
    ("kk", (SWA_KV_HEADS, SEQ_TILE + SWA_BLOCK, SWA_HEAD_DIM), BF16),
    ("vv", (SWA_KV_HEADS, SEQ_TILE + SWA_BLOCK, SWA_HEAD_DIM), BF16),
)


class _Operands:
    def __init__(self, refs):
        for (name, _, _), ref in zip(_OPERAND_BUFFERS, refs):
            setattr(self, name, ref)


def _layer_body(sinks_ref, x_ref, xn_ref, pos_ref, posn_ref, mod_ref, modn_ref, gnorm_ref, invf_ref, tri_ref, blk_ref,
                w_in_ref, w_dec_ref, b_dec_ref, g_gla_ref, w_out_ref, g_fin_ref,
                o_ref, state_ref, *operand_refs):
    ts = SEQ_TILE
    b_idx = pl.program_id(0)
    i = pl.program_id(1)
    last_step = i == pl.num_programs(1) - 1
    nbuf = len(_OPERAND_BUFFERS)
    ops0 = _Operands(operand_refs[:nbuf])
    ops1 = _Operands(operand_refs[nbuf:])

    gate = mod_ref[0, :, 2 * D_MODEL:3 * D_MODEL]

    def norm_modulation(m_ref):
        return (gnorm_ref[...] * (1.0 + m_ref[0, :, D_MODEL:2 * D_MODEL]), m_ref[0, :, 0:D_MODEL])

    n_chunks = ts // GLA_CHUNK
    n_blocks = ts // SWA_BLOCK
    units = [(n, kv) for n in range(n_blocks) for kv in range(SWA_KV_HEADS)]

    def proj(hb, c0, width):
        return _dot(hb, w_in_ref[:, c0:c0 + width])

    def prepare(x, pos, modulation, ops, prev_ops, starts_sequence):
        norm_gain, shift = modulation
        ms = jnp.mean(x * x, axis=-1, keepdims=True)
        hb = ((x * lax.rsqrt(ms + RMS_EPS)) * norm_gain + shift).astype(BF16)
        yield

        d0 = proj(hb, W_GA, LANES + 2 * SWA_KV_WIDTH)
        yield

        ang_t = invf_ref[...] * pos.astype(F32)
        reps = LANES // (SWA_HEAD_DIM // 2)
        cos_f = jnp.transpose(jnp.concatenate([jnp.cos(ang_t)] * reps, axis=0))
        sin_f = jnp.transpose(jnp.concatenate([jnp.sin(ang_t)] * reps, axis=0))
        lane = lax.broadcasted_iota(jnp.int32, (1, LANES), 1)
        first_half = (lane % SWA_HEAD_DIM) < (SWA_HEAD_DIM // 2)
        sin_s = jnp.where(first_half, -sin_f, sin_f)

        def rope(tv):
            rot = jnp.where(first_half,
                            pltpu.roll(tv, LANES - SWA_HEAD_DIM // 2, axis=1),
                            pltpu.roll(tv, SWA_HEAD_DIM // 2, axis=1))
            return tv * cos_f + rot * sin_s
        yield

        z = _dot(d0[:, 0:LANES].astype(BF16), w_dec_ref[...]) + b_dec_ref[...]
        k_rot = rope(d0[:, LANES:LANES + SWA_KV_WIDTH]).astype(BF16)
        for kv in range(SWA_KV_HEADS):
            hs = slice(kv * SWA_HEAD_DIM, (kv + 1) * SWA_HEAD_DIM)
            v0 = LANES + SWA_KV_WIDTH + kv * SWA_HEAD_DIM
            ops.kk[kv, SWA_BLOCK:SWA_BLOCK + ts, :] = k_rot[:, hs]
            ops.vv[kv, SWA_BLOCK:SWA_BLOCK + ts, :] = d0[:, v0:v0 + SWA_HEAD_DIM].astype(BF16)
            zeros = jnp.zeros((SWA_BLOCK, SWA_HEAD_DIM), BF16)
            if starts_sequence is True:
                k_carry, v_carry = zeros, zeros
            else:
                k_carry = prev_ops.kk[kv, ts:ts + SWA_BLOCK, :]
                v_carry = prev_ops.vv[kv, ts:ts + SWA_BLOCK, :]
                if starts_sequence is not False:
                    k_carry = jnp.where(starts_sequence, zeros, k_carry)
                    v_carry = jnp.where(starts_sequence, zeros, v_carry)
            ops.kk[kv, 0:SWA_BLOCK, :] = k_carry
            ops.vv[kv, 0:SWA_BLOCK, :] = v_carry
        yield

        log_a = _log_sigmoid(z) * (1.0 / GLA_GATE_TAU)
        la_hi = log_a.astype(BF16)
        la_lo = (log_a - la_hi.astype(F32)).astype(BF16)
        b2 = _dot(tri_ref[...], jnp.concatenate([la_hi, la_lo], axis=1))
        gqk = proj(hb, W_GQ, 2 * GLA_QK)
        gq = gqk[:, :GLA_QK]
        gk = gqk[:, GLA_QK:]
        yield
        b = b2[:, :GLA_QK] + b2[:, GLA_QK:]
        e_b = jnp.exp(b)
        e_last = jnp.concatenate(
            [jnp.broadcast_to(e_b[(c + 1) * GLA_CHUNK - 1:(c + 1) * GLA_CHUNK, :], (GLA_CHUNK, GLA_QK))
             for c in range(n_chunks)], axis=0)
        ops.dec[...] = jnp.transpose(e_last)
        q_d = gq * (GLA_DK ** -0.5) * e_b
        yield
        k_d = gk * (1.0 / e_b)
        ops.kd[...] = k_d.astype(BF16)
        k_tail = k_d * e_last

        blk = blk_ref[...]
        low_half = lax.broadcasted_iota(jnp.int32, (1, LANES), 1) < GLA_DK
        tile_reps = ts // LANES
        for pair in range(GLA_HEADS // 2):
            pr = slice(pair * LANES, (pair + 1) * LANES)
            q_pair, k_pair = q_d[:, pr], k_tail[:, pr]
            q_swap = pltpu.roll(q_pair, GLA_DK, axis=1)
            k_swap = pltpu.roll(k_pair, GLA_DK, axis=1)
            ops.qs[pair] = q_pair.astype(BF16)
            for j in range(2):
                a, b_ = (q_pair, q_swap) if j == 0 else (q_swap, q_pair)
                c, d_ = (k_pair, k_swap) if j == 0 else (k_swap, k_pair)
                q_dup = jnp.where(low_half, a, b_).astype(BF16)
                k_dup = jnp.where(low_half, c, d_).astype(BF16)
                ops.qb[2 * pair + j] = jnp.concatenate([q_dup] * tile_reps, axis=1) * blk
                ops.kb[2 * pair + j] = jnp.concatenate([k_dup] * tile_reps, axis=1) * blk
            yield

        for c0 in range(0, GLA_WIDTH, PROJ_COLS):
            ops.gv[:, c0:c0 + PROJ_COLS] = proj(hb, W_GV + c0, PROJ_COLS).astype(BF16)
            yield

        sq_all = proj(hb, W_SQ, SWA_WIDTH)
        for kv in range(SWA_KV_HEADS):
            sq = sq_all[:, kv * MXU_COLS:(kv + 1) * MXU_COLS]
            q_rot = jnp.concatenate([rope(sq[:, j * LANES:(j + 1) * LANES]) for j in range(MXU_COLS // LANES)],
                                    axis=1) * (SWA_HEAD_DIM ** -0.5 * LOG2E)
            for n in range(n_blocks):
                q_t = jnp.transpose(q_rot[n * SWA_BLOCK:(n + 1) * SWA_BLOCK, :])
                ops.qt[n * SWA_KV_HEADS + kv] = jnp.concatenate(
                    [q_t[j * SWA_HEAD_DIM:(j + 1) * SWA_HEAD_DIM, :] for j in range(SWA_GROUP)],
                    axis=1).astype(BF16)
            yield

        for c0 in range(0, GLA_WIDTH, PROJ_COLS):
            ops.gg[:, c0:c0 + PROJ_COLS] = _silu(proj(hb, W_GZ + c0, PROJ_COLS)).astype(BF16)
            yield
        for c0 in range(0, SWA_WIDTH, PROJ_COLS):
            ops.sg[:, c0:c0 + PROJ_COLS] = _silu(proj(hb, W_SZ + c0, PROJ_COLS)).astype(BF16)
            yield

    def gla(ops, mixed):
        row_i = lax.broadcasted_iota(jnp.int32, (ts, ts), 0)
        col_i = lax.broadcasted_iota(jnp.int32, (ts, ts), 1)
        causal = ((row_i // GLA_CHUNK) == (col_i // GLA_CHUNK)) & (col_i <= row_i)
        pair_lane = lax.broadcasted_iota(jnp.int32, (1, LANES), 1)
        heads = []
        for hd in range(GLA_HEADS):
            pr = slice((hd // 2) * LANES, (hd // 2 + 1) * LANES)
            v_h = ops.gv[:, hd * GLA_DV:(hd + 1) * GLA_DV]
            k_h = jnp.where((pair_lane // GLA_DK) == (hd % 2), ops.kd[:, pr], jnp.zeros((), BF16))
            s = jnp.where(causal, _dot_nt(ops.qs[hd // 2], k_h), 0.0).astype(BF16)
            u_all = _dot_tn(ops.kb[hd], v_h)
            heads.append((v_h, s, u_all))
            if hd % 2 == 1:
                yield

        staged = []
        for hd, (v_h, s, u_all) in enumerate(heads):
            st = state_ref[hd]
            entering = []
            for c in range(n_chunks):
                entering.append(st)
                last = c * GLA_CHUNK + GLA_CHUNK - 1
                d_col = ops.dec[hd * GLA_DK:(hd + 1) * GLA_DK, last:last + 1]
                st = st * d_col + u_all[c * GLA_DK:(c + 1) * GLA_DK, :]
            state_ref[hd] = st
            s_stack = jnp.concatenate(entering, axis=0).astype(BF16)
            staged.append((s, v_h, s_stack))
        yield

        outs = []
        for hd, (s, v_h, s_stack) in enumerate(staged):
            outs.append(_dot(s, v_h) + _dot(ops.qb[hd], s_stack))
            if hd % 2 == 1:
                yield

        g_gla = g_gla_ref[...]
        for hd, o_h in enumerate(outs):
            vs = slice(hd * GLA_DV, (hd + 1) * GLA_DV)
            o_ms = jnp.mean(o_h * o_h, axis=-1, keepdims=True)
            o_n = o_h * lax.rsqrt(o_ms + RMS_EPS) * g_gla[:, vs]
            mixed.append((o_n * ops.gg[:, vs].astype(F32)).astype(BF16))

    def swa(ops, mixed, first_tile):
        gq_lanes = SWA_GROUP * SWA_BLOCK
        kj = lax.broadcasted_iota(jnp.int32, (2 * SWA_BLOCK, gq_lanes), 0)
        ql = lax.broadcasted_iota(jnp.int32, (2 * SWA_BLOCK, gq_lanes), 1)
        dist = (ql % SWA_BLOCK) + SWA_BLOCK - kj
        valid = (dist >= 0) & (dist < WINDOW)
        head_of_lane = lax.broadcasted_iota(jnp.int32, (1, gq_lanes), 1) // SWA_BLOCK

        scores = []
        for u, (n, kv) in enumerate(units):
            bs = slice(n * SWA_BLOCK, (n + 2) * SWA_BLOCK)
            scores.append(_dot(ops.kk[kv, bs, :], ops.qt[u]))
            if u % 2 == 1:
                yield

        probs = []
        for (n, kv), s_raw in zip(units, scores):
            if n == 0 and first_tile is not None:
                ok = valid & ((kj >= SWA_BLOCK) | jnp.logical_not(first_tile))
            else:
                ok = valid
            s_t = s_raw + jnp.where(ok, 0.0, NEG_BIG)
            sink = jnp.zeros((1, gq_lanes), F32)
            for j in range(SWA_GROUP):
                sink = jnp.where(head_of_lane == j, sinks_ref[kv * SWA_GROUP + j] * LOG2E, sink)
            m = jnp.maximum(jnp.max(s_t, axis=0, keepdims=True), sink)
            e = jnp.exp2(s_t - m)
            den = jnp.sum(e, axis=0, keepdims=True) + jnp.exp2(sink - m)
            probs.append((e.astype(BF16), 1.0 / den))
            yield

        outs = []
        for (n, kv), (e, inv_den) in zip(units, probs):
            bs = slice(n * SWA_BLOCK, (n + 2) * SWA_BLOCK)
            outs.append(_dot_tn(ops.vv[kv, bs, :], e) * inv_den)
        yield

        for n in range(n_blocks):
            rs = slice(n * SWA_BLOCK, (n + 1) * SWA_BLOCK)
            o_groups = []
            for kv in range(SWA_KV_HEADS):
                o_t = outs[n * SWA_KV_HEADS + kv]
                o_stack = jnp.concatenate([o_t[:, j * SWA_BLOCK:(j + 1) * SWA_BLOCK] for j in range(SWA_GROUP)],
                                          axis=0)
                o_groups.append(jnp.transpose(o_stack))
            o_swa = jnp.concatenate(o_groups, axis=1)
            mixed.append((o_swa * ops.sg[rs, :].astype(F32)).astype(BF16))

    def finish(gla_heads, swa_blocks, rows):
        cat = jnp.concatenate([jnp.concatenate(gla_heads, axis=1), jnp.concatenate(swa_blocks, axis=0)], axis=1)
        y = _dot(cat, w_out_ref[...])
        yield
        x = x_ref[0, rows, :]
        xo = x + gate * y
        ms_o = jnp.mean(xo * xo, axis=-1, keepdims=True)
        o_ref[0, rows, :] = xo * lax.rsqrt(ms_o + RMS_EPS) * g_fin_ref[...]

    def interleave(primary, filler):
        live = list(primary)
        while live:
            for g in list(live):
                if next(g, StopIteration) is StopIteration:
                    live.remove(g)
                next(filler, None)
        for _ in filler:
            pass

    rows = [slice(k * ts, (k + 1) * ts) for k in range(STEP_TILES)]
    ops = (ops0, ops1)
    mixed = [([], []) for _ in range(STEP_TILES)]

    @pl.when(i == 0)
    def _():
        state_ref[...] = jnp.zeros_like(state_ref)

    @pl.when((i == 0) & (b_idx == 0))
    def _():
        for _ in prepare(x_ref[0, rows[0], :], pos_ref[0, :, rows[0]], norm_modulation(mod_ref), ops0, None, True):
            pass

    for k in range(STEP_TILES):
        cur, nxt = ops[k % 2], ops[(k + 1) % 2]
        primary = [gla(cur, mixed[k][0]), swa(cur, mixed[k][1], (i == 0) if k == 0 else None)]
        if k > 0:
            primary.insert(0, finish(*mixed[k - 1], rows[k - 1]))
        if k + 1 < STEP_TILES:
            filler = prepare(x_ref[0, rows[k + 1], :], pos_ref[0, :, rows[k + 1]], norm_modulation(mod_ref),
                             nxt, cur, False)
        else:
            filler = prepare(xn_ref[0], posn_ref[0], norm_modulation(modn_ref), nxt, cur, last_step)
        interleave(primary, filler)
    for _ in finish(*mixed[STEP_TILES - 1], rows[STEP_TILES - 1]):
        pass


def _chunk_matrices(ts):
    r = np.arange(ts)
    same = (r[:, None] // GLA_CHUNK) == (r[None, :] // GLA_CHUNK)
    return jnp.asarray(same & (r[None, :] <= r[:, None]), BF16), jnp.asarray(same, BF16)


def _rope_inv_freq_column():
    inv = 1.0 / (ROPE_THETA ** (jnp.arange(0, SWA_HEAD_DIM, 2, dtype=F32) / SWA_HEAD_DIM))
    return inv.reshape(SWA_HEAD_DIM // 2, 1)


def kernel(x, c, positions, w_ada, b_ada, g_norm, w_in, w_decay, b_decay, g_gla_head, sinks, w_out, g_final):
    B, S, D = x.shape
    ts = SEQ_TILE
    assert D == D_MODEL and S % (STEP_TILES * ts) == 0 and ts % SWA_BLOCK == 0 and STEP_TILES % 2 == 0
    assert w_ada.shape[0] == 1, "one layer"
    n_tiles = S // ts

    mod = _adaln(c.astype(F32), w_ada[0], b_ada[0][None, :]).reshape(B, 1, 3 * D)

    w_in_b, w_out_b = _prepare_weights(jnp.transpose(w_in[0]), w_out[0])
    w_dec = jnp.pad(w_decay[0].astype(BF16), ((0, LANES - GLA_GATE_RANK), (0, 0)))
    tri, blk = _chunk_matrices(ts)
    pos3 = positions.reshape(B, 1, S)

    const2 = lambda b, i, s: (0, 0)
    full = lambda a: pl.BlockSpec(a.shape, const2)
    n_steps = n_tiles // STEP_TILES

    def next_row_tile(b, i):
        wraps = i == n_steps - 1
        at_end = wraps & (b == B - 1)
        row = jnp.where(wraps & jnp.logical_not(at_end), b + 1, b)
        tile = jnp.where(wraps, jnp.where(at_end, n_tiles - 1, 0), STEP_TILES * (i + 1))
        return row, tile

    operands = [
        x, x, pos3, pos3,
        mod, mod,
        g_norm[0][None, :],
        _rope_inv_freq_column(),
        tri, blk,
        w_in_b,
        w_dec, b_decay[0][None, :], g_gla_head[0][None, :],
        w_out_b, g_final[None, :],
    ]
    in_specs = [
        pl.BlockSpec((1, STEP_TILES * ts, D), lambda b, i, s: (b, i, 0)),
        pl.BlockSpec((1, ts, D), lambda b, i, s: next_row_tile(b, i) + (0,)),
        pl.BlockSpec((1, 1, STEP_TILES * ts), lambda b, i, s: (b, 0, i)),
        pl.BlockSpec((1, 1, ts), lambda b, i, s: (next_row_tile(b, i)[0], 0, next_row_tile(b, i)[1])),
        pl.BlockSpec((1, 1, 3 * D), lambda b, i, s: (b, 0, 0)),
        pl.BlockSpec((1, 1, 3 * D), lambda b, i, s: (next_row_tile(b, i)[0], 0, 0)),
    ] + [full(a) for a in operands[6:]]

    operand_scratch = [pltpu.VMEM(shape, dtype) for _, shape, dtype in _OPERAND_BUFFERS]
    grid_spec = pltpu.PrefetchScalarGridSpec(
        num_scalar_prefetch=1,
        grid=(B, n_steps),
        in_specs=in_specs,
        out_specs=pl.BlockSpec((1, STEP_TILES * ts, D), lambda b, i, s: (b, i, 0)),
        scratch_shapes=[
            pltpu.VMEM((GLA_HEADS, GLA_DK, GLA_DV), F32),
        ] + operand_scratch + operand_scratch,
    )
    return pl.pallas_call(
        _layer_body,
        grid_spec=grid_spec,
        out_shape=jax.ShapeDtypeStruct((B, S, D), x.dtype),
        compiler_params=pltpu.CompilerParams(
            dimension_semantics=("arbitrary", "arbitrary"),
            vmem_limit_bytes=VMEM_LIMIT_BYTES,
        ),
        name="hymba_layer",
    )(sinks[0].astype(F32), *operands)
```

```python
import jax
import jax.numpy as jnp
import numpy as np
from jax import lax
from jax.experimental import pallas as pl
from jax.experimental.pallas import tpu as pltpu

D_MODEL = 1024
GLA_HEADS = 4
GLA_DK = 64
GLA_DV = 128
GLA_WIDTH = GLA_HEADS * GLA_DV
GLA_QK = GLA_HEADS * GLA_DK
GLA_GATE_RANK = 16
GLA_GATE_TAU = 16.0
GLA_CHUNK = 64
SWA_Q_HEADS = 8
SWA_KV_HEADS = 2
SWA_GROUP = SWA_Q_HEADS // SWA_KV_HEADS
SWA_HEAD_DIM = 64
SWA_WIDTH = SWA_Q_HEADS * SWA_HEAD_DIM
SWA_KV_WIDTH = SWA_KV_HEADS * SWA_HEAD_DIM
WINDOW = 128
SWA_BLOCK = 128
ROPE_THETA = 10000.0
RMS_EPS = 1e-6

LANES = 128
MXU_COLS = 256
PROJ_COLS = 2 * MXU_COLS
SEQ_TILE = 256
STEP_TILES = 2
LOG2E = 1.4426950408889634
NEG_BIG = -1e30
VMEM_LIMIT_BYTES = 40 * 1024 * 1024

W_GA = 0
W_SK = W_GA + LANES
W_SV = W_SK + SWA_KV_WIDTH
W_GQ = W_SV + SWA_KV_WIDTH
W_GK = W_GQ + GLA_QK
W_GV = W_GK + GLA_QK
W_SQ = W_GV + GLA_WIDTH
W_GZ = W_SQ + SWA_WIDTH
W_SZ = W_GZ + GLA_WIDTH
W_COLS = W_SZ + SWA_WIDTH

SWA_UNITS = (SEQ_TILE // SWA_BLOCK) * SWA_KV_HEADS

BF16 = jnp.bfloat16
F32 = jnp.float32


def _dot(a, b):
    return jnp.dot(a, b, preferred_element_type=F32)


def _dot_nt(a, b):
    return lax.dot_general(a, b, (((1,), (1,)), ((), ())), preferred_element_type=F32)


def _dot_tn(a, b):
    return lax.dot_general(a, b, (((0,), (0,)), ((), ())), preferred_element_type=F32)


def _silu(x):
    return x * (1.0 / (1.0 + jnp.exp2(x * (-LOG2E))))


def _log_sigmoid(z):
    return jnp.minimum(z, 0.0) - jnp.log(1.0 + jnp.exp2(jnp.abs(z) * (-LOG2E)))


_SRC_GA = 2 * GLA_QK + GLA_WIDTH
_SRC_GZ = _SRC_GA + GLA_GATE_RANK
_SRC_SQ = _SRC_GZ + GLA_WIDTH
_SRC_SK = _SRC_SQ + SWA_WIDTH
_SRC_SV = _SRC_SK + SWA_KV_WIDTH
_SRC_SZ = _SRC_SV + SWA_KV_WIDTH
_SRC_ROWS = _SRC_SZ + SWA_WIDTH
_REGROUP = (
    (W_SK, _SRC_SK, SWA_KV_WIDTH), (W_SV, _SRC_SV, SWA_KV_WIDTH),
    (W_GQ, 0, GLA_QK), (W_GK, GLA_QK, GLA_QK), (W_GV, 2 * GLA_QK, GLA_WIDTH),
    (W_SQ, _SRC_SQ, SWA_WIDTH), (W_GZ, _SRC_GZ, GLA_WIDTH), (W_SZ, _SRC_SZ, SWA_WIDTH),
)
W_COL_BLOCK = 256


def _setup_body(c_ref, w_ada_ref, b_ada_ref, wt_ref, w_out_ref, mod_ref, o_ref, w_out_o_ref):
    mod_ref[...] = _dot(_silu(c_ref[...]).astype(BF16), w_ada_ref[...].astype(BF16)) + b_ada_ref[...]
    w_out_o_ref[...] = w_out_ref[...].astype(BF16)
    row = lax.broadcasted_iota(jnp.int32, (LANES, 1), 0)
    ga = jnp.where(row < GLA_GATE_RANK, wt_ref[_SRC_GA:_SRC_GA + LANES, :], 0.0)
    o_ref[:, W_GA:W_GA + LANES] = jnp.transpose(ga).astype(BF16)
    for dst, src, width in _REGROUP:
        o_ref[:, dst:dst + width] = jnp.transpose(wt_ref[src:src + width, :]).astype(BF16)


def _setup(c, w_ada, b_ada, wt, w_out):
    n, d = wt.shape
    k_out, n_out = w_out.shape
    n_mod = w_ada.shape[1]
    steps = d // W_COL_BLOCK
    assert n == _SRC_ROWS and d % W_COL_BLOCK == 0 and k_out % steps == 0 and n_mod % (steps * LANES) == 0
    out_rows = k_out // steps
    mod_cols = n_mod // steps
    rows = c.shape[0]
    return pl.pallas_call(
        _setup_body,
        grid=(steps,),
        in_specs=[pl.BlockSpec((rows, w_ada.shape[0]), lambda r: (0, 0)),
                  pl.BlockSpec((w_ada.shape[0], mod_cols), lambda r: (0, r)),
                  pl.BlockSpec((1, mod_cols), lambda r: (0, r)),
                  pl.BlockSpec((n, W_COL_BLOCK), lambda r: (0, r)),
                  pl.BlockSpec((out_rows, n_out), lambda r: (r, 0))],
        out_specs=[pl.BlockSpec((rows, mod_cols), lambda r: (0, r)),
                   pl.BlockSpec((W_COL_BLOCK, W_COLS), lambda r: (r, 0)),
                   pl.BlockSpec((out_rows, n_out), lambda r: (r, 0))],
        out_shape=[jax.ShapeDtypeStruct((rows, n_mod), F32), jax.ShapeDtypeStruct((d, W_COLS), BF16),
                   jax.ShapeDtypeStruct((k_out, n_out), BF16)],
        compiler_params=pltpu.CompilerParams(vmem_limit_bytes=VMEM_LIMIT_BYTES),
        name="setup",
    )(c, w_ada, b_ada, wt, w_out)


_OPERAND_BUFFERS = (
    ("qs", (GLA_HEADS // 2, SEQ_TILE, LANES), BF16),
    ("qb", (GLA_HEADS, SEQ_TILE, SEQ_TILE), BF16),
    ("kb", (GLA_HEADS, SEQ_TILE, SEQ_TILE), BF16),
    ("kd", (SEQ_TILE, GLA_QK), BF16),
    ("gv", (SEQ_TILE, GLA_WIDTH), BF16),
    ("dec", (GLA_QK, SEQ_TILE), F32),
    ("gg", (SEQ_TILE, GLA_WIDTH), BF16),
    ("sg", (SEQ_TILE, SWA_WIDTH), BF16),
    ("qt", (SWA_UNITS, SWA_HEAD_DIM, SWA_GROUP * SWA_BLOCK), BF16),
    ("kk", (SWA_KV_HEADS, SEQ_TILE + SWA_BLOCK, SWA_HEAD_DIM), BF16),
    ("vv", (SWA_KV_HEADS, SEQ_TILE + SWA_BLOCK, SWA_HEAD_DIM), BF16),
)


class _Operands:
    def __init__(self, refs):
        for (name, _, _), ref in zip(_OPERAND_BUFFERS, refs):
            setattr(self, name, ref)


def _layer_body(sinks_ref, x_ref, xn_ref, pos_ref, posn_ref, mod_ref, modn_ref, gnorm_ref, invf_ref, tri_ref, blk_ref,
                w_in_ref, w_dec_ref, b_dec_ref, g_gla_ref, w_out_ref, g_fin_ref,
                o_ref, state_ref, *operand_refs):
    ts = SEQ_TILE
    b_idx = pl.program_id(0)
    i = pl.program_id(1)
    last_step = i == pl.num_programs(1) - 1
    nbuf = len(_OPERAND_BUFFERS)
    ops0 = _Operands(operand_refs[:nbuf])
    ops1 = _Operands(operand_refs[nbuf:])

    gate = mod_ref[0, :, 2 * D_MODEL:3 * D_MODEL]

    def norm_modulation(m_ref):
        return (gnorm_ref[...] * (1.0 + m_ref[0, :, D_MODEL:2 * D_MODEL]), m_ref[0, :, 0:D_MODEL])

    n_chunks = ts // GLA_CHUNK
    n_blocks = ts // SWA_BLOCK
    units = [(n, kv) for n in range(n_blocks) for kv in range(SWA_KV_HEADS)]

    def proj(hb, c0, width):
        return _dot(hb, w_in_ref[:, c0:c0 + width])

    def prepare(x, pos, modulation, ops, prev_ops, starts_sequence):
        norm_gain, shift = modulation
        ms = jnp.mean(x * x, axis=-1, keepdims=True)
        hb = ((x * lax.rsqrt(ms + RMS_EPS)) * norm_gain + shift).astype(BF16)
        yield

        d0 = proj(hb, W_GA, LANES + 2 * SWA_KV_WIDTH)
        yield

        ang_t = invf_ref[...] * pos.astype(F32)
        reps = LANES // (SWA_HEAD_DIM // 2)
        cos_f = jnp.transpose(jnp.concatenate([jnp.cos(ang_t)] * reps, axis=0))
        sin_f = jnp.transpose(jnp.concatenate([jnp.sin(ang_t)] * reps, axis=0))
        lane = lax.broadcasted_iota(jnp.int32, (1, LANES), 1)
        first_half = (lane % SWA_HEAD_DIM) < (SWA_HEAD_DIM // 2)
        sin_s = jnp.where(first_half, -sin_f, sin_f)

        def rope(tv):
            rot = jnp.where(first_half,
                            pltpu.roll(tv, LANES - SWA_HEAD_DIM // 2, axis=1),
                            pltpu.roll(tv, SWA_HEAD_DIM // 2, axis=1))
            return tv * cos_f + rot * sin_s
        yield

        z = _dot(d0[:, 0:LANES].astype(BF16), w_dec_ref[...]) + b_dec_ref[...]
        k_rot = rope(d0[:, LANES:LANES + SWA_KV_WIDTH]).astype(BF16)
        for kv in range(SWA_KV_HEADS):
            hs = slice(kv * SWA_HEAD_DIM, (kv + 1) * SWA_HEAD_DIM)
            v0 = LANES + SWA_KV_WIDTH + kv * SWA_HEAD_DIM
            ops.kk[kv, SWA_BLOCK:SWA_BLOCK + ts, :] = k_rot[:, hs]
            ops.vv[kv, SWA_BLOCK:SWA_BLOCK + ts, :] = d0[:, v0:v0 + SWA_HEAD_DIM].astype(BF16)
            zeros = jnp.zeros((SWA_BLOCK, SWA_HEAD_DIM), BF16)
            if starts_sequence is True:
                k_carry, v_carry = zeros, zeros
            else:
                k_carry = prev_ops.kk[kv, ts:ts + SWA_BLOCK, :]
                v_carry = prev_ops.vv[kv, ts:ts + SWA_BLOCK, :]
                if starts_sequence is not False:
                    k_carry = jnp.where(starts_sequence, zeros, k_carry)
                    v_carry = jnp.where(starts_sequence, zeros, v_carry)
            ops.kk[kv, 0:SWA_BLOCK, :] = k_carry
            ops.vv[kv, 0:SWA_BLOCK, :] = v_carry
        yield

        log_a = _log_sigmoid(z) * (1.0 / GLA_GATE_TAU)
        la_hi = log_a.astype(BF16)
        la_lo = (log_a - la_hi.astype(F32)).astype(BF16)
        b2 = _dot(tri_ref[...], jnp.concatenate([la_hi, la_lo], axis=1))
        gqk = proj(hb, W_GQ, 2 * GLA_QK)
        gq = gqk[:, :GLA_QK]
        gk = gqk[:, GLA_QK:]
        yield
        b = b2[:, :GLA_QK] + b2[:, GLA_QK:]
        e_b = jnp.exp(b)
        e_last = jnp.concatenate(
            [jnp.broadcast_to(e_b[(c + 1) * GLA_CHUNK - 1:(c + 1) * GLA_CHUNK, :], (GLA_CHUNK, GLA_QK))
             for c in range(n_chunks)], axis=0)
        ops.dec[...] = jnp.transpose(e_last)
        q_d = gq * (GLA_DK ** -0.5) * e_b
        yield
        k_d = gk * (1.0 / e_b)
        ops.kd[...] = k_d.astype(BF16)
        k_tail = k_d * e_last

        blk = blk_ref[...]
        low_half = lax.broadcasted_iota(jnp.int32, (1, LANES), 1) < GLA_DK
        tile_reps = ts // LANES
        for pair in range(GLA_HEADS // 2):
            pr = slice(pair * LANES, (pair + 1) * LANES)
            q_pair, k_pair = q_d[:, pr], k_tail[:, pr]
            q_swap = pltpu.roll(q_pair, GLA_DK, axis=1)
            k_swap = pltpu.roll(k_pair, GLA_DK, axis=1)
            ops.qs[pair] = q_pair.astype(BF16)
            for j in range(2):
                a, b_ = (q_pair, q_swap) if j == 0 else (q_swap, q_pair)
                c, d_ = (k_pair, k_swap) if j == 0 else (k_swap, k_pair)
                q_dup = jnp.where(low_half, a, b_).astype(BF16)
                k_dup = jnp.where(low_half, c, d_).astype(BF16)
                ops.qb[2 * pair + j] = jnp.concatenate([q_dup] * tile_reps, axis=1) * blk
                ops.kb[2 * pair + j] = jnp.concatenate([k_dup] * tile_reps, axis=1) * blk
            yield

        for c0 in range(0, GLA_WIDTH, PROJ_COLS):
            ops.gv[:, c0:c0 + PROJ_COLS] = proj(hb, W_GV + c0, PROJ_COLS).astype(BF16)
            yield

        sq_all = proj(hb, W_SQ, SWA_WIDTH)
        for kv in range(SWA_KV_HEADS):
            sq = sq_all[:, kv * MXU_COLS:(kv + 1) * MXU_COLS]
            q_rot = jnp.concatenate([rope(sq[:, j * LANES:(j + 1) * LANES]) for j in range(MXU_COLS // LANES)],
                                    axis=1) * (SWA_HEAD_DIM ** -0.5 * LOG2E)
            for n in range(n_blocks):
                q_t = jnp.transpose(q_rot[n * SWA_BLOCK:(n + 1) * SWA_BLOCK, :])
                ops.qt[n * SWA_KV_HEADS + kv] = jnp.concatenate(
                    [q_t[j * SWA_HEAD_DIM:(j + 1) * SWA_HEAD_DIM, :] for j in range(SWA_GROUP)],
                    axis=1).astype(BF16)
            yield

        for c0 in range(0, GLA_WIDTH, PROJ_COLS):
            ops.gg[:, c0:c0 + PROJ_COLS] = _silu(proj(hb, W_GZ + c0, PROJ_COLS)).astype(BF16)
            yield
        for c0 in range(0, SWA_WIDTH, PROJ_COLS):
            ops.sg[:, c0:c0 + PROJ_COLS] = _silu(proj(hb, W_SZ + c0, PROJ_COLS)).astype(BF16)
            yield

    def gla(ops, mixed):
        row_i = lax.broadcasted_iota(jnp.int32, (ts, ts), 0)
        col_i = lax.broadcasted_iota(jnp.int32, (ts, ts), 1)
        causal = ((row_i // GLA_CHUNK) == (col_i // GLA_CHUNK)) & (col_i <= row_i)
        pair_lane = lax.broadcasted_iota(jnp.int32, (1, LANES), 1)
        heads = []
        for hd in range(GLA_HEADS):
            pr = slice((hd // 2) * LANES, (hd // 2 + 1) * LANES)
            v_h = ops.gv[:, hd * GLA_DV:(hd + 1) * GLA_DV]
            k_h = jnp.where((pair_lane // GLA_DK) == (hd % 2), ops.kd[:, pr], jnp.zeros((), BF16))
            s = jnp.where(causal, _dot_nt(ops.qs[hd // 2], k_h), 0.0).astype(BF16)
            u_all = _dot_tn(ops.kb[hd], v_h)
            heads.append((v_h, s, u_all))
            if hd % 2 == 1:
                yield

        staged = []
        for hd, (v_h, s, u_all) in enumerate(heads):
            st = state_ref[hd]
            entering = []
            for c in range(n_chunks):
                entering.append(st)
                last = c * GLA_CHUNK + GLA_CHUNK - 1
                d_col = ops.dec[hd * GLA_DK:(hd + 1) * GLA_DK, last:last + 1]
                st = st * d_col + u_all[c * GLA_DK:(c + 1) * GLA_DK, :]
            state_ref[hd] = st
            s_stack = jnp.concatenate(entering, axis=0).astype(BF16)
            staged.append((s, v_h, s_stack))
        yield

        outs = []
        for hd, (s, v_h, s_stack) in enumerate(staged):
            outs.append(_dot(s, v_h) + _dot(ops.qb[hd], s_stack))
            if hd % 2 == 1:
                yield

        g_gla = g_gla_ref[...]
        for hd, o_h in enumerate(outs):
            vs = slice(hd * GLA_DV, (hd + 1) * GLA_DV)
            o_ms = jnp.mean(o_h * o_h, axis=-1, keepdims=True)
            o_n = o_h * lax.rsqrt(o_ms + RMS_EPS) * g_gla[:, vs]
            mixed.append((o_n * ops.gg[:, vs].astype(F32)).astype(BF16))

    def swa(ops, mixed, first_tile):
        gq_lanes = SWA_GROUP * SWA_BLOCK
        kj = lax.broadcasted_iota(jnp.int32, (2 * SWA_BLOCK, gq_lanes), 0)
        ql = lax.broadcasted_iota(jnp.int32, (2 * SWA_BLOCK, gq_lanes), 1)
        dist = (ql % SWA_BLOCK) + SWA_BLOCK - kj
        valid = (dist >= 0) & (dist < WINDOW)
        head_of_lane = lax.broadcasted_iota(jnp.int32, (1, gq_lanes), 1) // SWA_BLOCK

        scores = []
        for u, (n, kv) in enumerate(units):
            bs = slice(n * SWA_BLOCK, (n + 2) * SWA_BLOCK)
            scores.append(_dot(ops.kk[kv, bs, :], ops.qt[u]))
            if u % 2 == 1:
                yield

        probs = []
        for (n, kv), s_raw in zip(units, scores):
            if n == 0 and first_tile is not None:
                ok = valid & ((kj >= SWA_BLOCK) | jnp.logical_not(first_tile))
            else:
                ok = valid
            s_t = s_raw + jnp.where(ok, 0.0, NEG_BIG)
            sink = jnp.zeros((1, gq_lanes), F32)
            for j in range(SWA_GROUP):
                sink = jnp.where(head_of_lane == j, sinks_ref[kv * SWA_GROUP + j] * LOG2E, sink)
            m = jnp.maximum(jnp.max(s_t, axis=0, keepdims=True), sink)
            e = jnp.exp2(s_t - m)
            den = jnp.sum(e, axis=0, keepdims=True) + jnp.exp2(sink - m)
            probs.append((e.astype(BF16), 1.0 / den))
            yield

        outs = []
        for (n, kv), (e, inv_den) in zip(units, probs):
            bs = slice(n * SWA_BLOCK, (n + 2) * SWA_BLOCK)
            outs.append(_dot_tn(ops.vv[kv, bs, :], e) * inv_den)
        yield

        for n in range(n_blocks):
            rs = slice(n * SWA_BLOCK, (n + 1) * SWA_BLOCK)
            o_groups = []
            for kv in range(SWA_KV_HEADS):
                o_t = outs[n * SWA_KV_HEADS + kv]
                o_stack = jnp.concatenate([o_t[:, j * SWA_BLOCK:(j + 1) * SWA_BLOCK] for j in range(SWA_GROUP)],
                                          axis=0)
                o_groups.append(jnp.transpose(o_stack))
            o_swa = jnp.concatenate(o_groups, axis=1)
            mixed.append((o_swa * ops.sg[rs, :].astype(F32)).astype(BF16))

    def finish(gla_heads, swa_blocks, rows):
        cat = jnp.concatenate([jnp.concatenate(gla_heads, axis=1), jnp.concatenate(swa_blocks, axis=0)], axis=1)
        y = _dot(cat, w_out_ref[...])
        yield
        x = x_ref[0, rows, :]
        xo = x + gate * y
        ms_o = jnp.mean(xo * xo, axis=-1, keepdims=True)
        o_ref[0, rows, :] = xo * lax.rsqrt(ms_o + RMS_EPS) * g_fin_ref[...]

    def interleave(primary, filler):
        live = list(primary)
        while live:
            for g in list(live):
                if next(g, StopIteration) is StopIteration:
                    live.remove(g)
                next(filler, None)
        for _ in filler:
            pass

    rows = [slice(k * ts, (k + 1) * ts) for k in range(STEP_TILES)]
    ops = (ops0, ops1)
    mixed = [([], []) for _ in range(STEP_TILES)]

    @pl.when(i == 0)
    def _():
        state_ref[...] = jnp.zeros_like(state_ref)

    @pl.when((i == 0) & (b_idx == 0))
    def _():
        for _ in prepare(x_ref[0, rows[0], :], pos_ref[0, :, rows[0]], norm_modulation(mod_ref), ops0, None, True):
            pass

    for k in range(STEP_TILES):
        cur, nxt = ops[k % 2], ops[(k + 1) % 2]
        primary = [gla(cur, mixed[k][0]), swa(cur, mixed[k][1], (i == 0) if k == 0 else None)]
        if k > 0:
            primary.insert(0, finish(*mixed[k - 1], rows[k - 1]))
        if k + 1 < STEP_TILES:
            filler = prepare(x_ref[0, rows[k + 1], :], pos_ref[0, :, rows[k + 1]], norm_modulation(mod_ref),
                             nxt, cur, False)
        else:
            filler = prepare(xn_ref[0], posn_ref[0], norm_modulation(modn_ref), nxt, cur, last_step)
        interleave(primary, filler)
    for _ in finish(*mixed[STEP_TILES - 1], rows[STEP_TILES - 1]):
        pass


def _chunk_matrices(ts):
    r = np.arange(ts)
    same = (r[:, None] // GLA_CHUNK) == (r[None, :] // GLA_CHUNK)
    return jnp.asarray(same & (r[None, :] <= r[:, None]), BF16), jnp.asarray(same, BF16)


def _rope_inv_freq_column():
    inv = 1.0 / (ROPE_THETA ** (jnp.arange(0, SWA_HEAD_DIM, 2, dtype=F32) / SWA_HEAD_DIM))
    return inv.reshape(SWA_HEAD_DIM // 2, 1)


def kernel(x, c, positions, w_ada, b_ada, g_norm, w_in, w_decay, b_decay, g_gla_head, sinks, w_out, g_final):
    B, S, D = x.shape
    ts = SEQ_TILE
    assert D == D_MODEL and S % (STEP_TILES * ts) == 0 and ts % SWA_BLOCK == 0 and STEP_TILES % 2 == 0
    assert w_ada.shape[0] == 1, "one layer"
    n_tiles = S // ts

    mod, w_in_b, w_out_b = _setup(c.astype(F32), w_ada[0], b_ada[0][None, :], jnp.transpose(w_in[0]), w_out[0])
    mod = mod.reshape(B, 1, 3 * D)
    w_dec = jnp.pad(w_decay[0].astype(BF16), ((0, LANES - GLA_GATE_RANK), (0, 0)))
    tri, blk = _chunk_matrices(ts)
    pos3 = positions.reshape(B, 1, S)

    const2 = lambda b, i, s: (0, 0)
    full = lambda a: pl.BlockSpec(a.shape, const2)
    n_steps = n_tiles // STEP_TILES

    def next_row_tile(b, i):
        wraps = i == n_steps - 1
        at_end = wraps & (b == B - 1)
        row = jnp.where(wraps & jnp.logical_not(at_end), b + 1, b)
        tile = jnp.where(wraps, jnp.where(at_end, n_tiles - 1, 0), STEP_TILES * (i + 1))
        return row, tile

    operands = [
        x, x, pos3, pos3,
        mod, mod,
        g_norm[0][None, :],
        _rope_inv_freq_column(),
        tri, blk,
        w_in_b,
        w_dec, b_decay[0][None, :], g_gla_head[0][None, :],
        w_out_b, g_final[None, :],
    ]
    in_specs = [
        pl.BlockSpec((1, STEP_TILES * ts, D), lambda b, i, s: (b, i, 0)),
        pl.BlockSpec((1, ts, D), lambda b, i, s: next_row_tile(b, i) + (0,)),
        pl.BlockSpec((1, 1, STEP_TILES * ts), lambda b, i, s: (b, 0, i)),
        pl.BlockSpec((1, 1, ts), lambda b, i, s: (next_row_tile(b, i)[0], 0, next_row_tile(b, i)[1])),
        pl.BlockSpec((1, 1, 3 * D), lambda b, i, s: (b, 0, 0)),
        pl.BlockSpec((1, 1, 3 * D), lambda b, i, s: (next_row_tile(b, i)[0], 0, 0)),
    ] + [full(a) for a in operands[6:]]

    operand_scratch = [pltpu.VMEM(shape, dtype) for _, shape, dtype in _OPERAND_BUFFERS]
    grid_spec = pltpu.PrefetchScalarGridSpec(
        num_scalar_prefetch=1,
        grid=(B, n_steps),
        in_specs=in_specs,
        out_specs=pl.BlockSpec((1, STEP_TILES * ts, D), lambda b, i, s: (b, i, 0)),
        scratch_shapes=[
            pltpu.VMEM((GLA_HEADS, GLA_DK, GLA_DV), F32),
        ] + operand_scratch + operand_scratch,
    )
    return pl.pallas_call(
        _layer_body,
        grid_spec=grid_spec,
        out_shape=jax.ShapeDtypeStruct((B, S, D), x.dtype),
        compiler_params=pltpu.CompilerParams(
            dimension_semantics=("arbitrary", "arbitrary"),
            vmem_limit_bytes=VMEM_LIMIT_BYTES,
        ),
        name="hymba_layer",
    )(sinks[0].astype(F32), *operands)
```

```python
import jax
import jax.numpy as jnp
import numpy as np
from jax import lax
from jax.experimental import pallas as pl
from jax.experimental.pallas import tpu as pltpu

D_MODEL = 1024
GLA_HEADS = 4
GLA_DK = 64
GLA_DV = 128
GLA_WIDTH = GLA_HEADS * GLA_DV
GLA_QK = GLA_HEADS * GLA_DK
GLA_GATE_RANK = 16
GLA_GATE_TAU = 16.0
GLA_CHUNK = 64
SWA_Q_HEADS = 8
SWA_KV_HEADS = 2
SWA_GROUP = SWA_Q_HEADS // SWA_KV_HEADS
SWA_HEAD_DIM = 64
SWA_WIDTH = SWA_Q_HEADS * SWA_HEAD_DIM
SWA_KV_WIDTH = SWA_KV_HEADS * SWA_HEAD_DIM
WINDOW = 128
SWA_BLOCK = 128
ROPE_THETA = 10000.0
RMS_EPS = 1e-6

LANES = 128
MXU_COLS = 256
PROJ_COLS = 2 * MXU_COLS
SEQ_TILE = 256
STEP_TILES = 2
LOG2E = 1.4426950408889634
NEG_BIG = -1e30
VMEM_LIMIT_BYTES = 40 * 1024 * 1024

W_GA = 0
W_SK = W_GA + LANES
W_SV = W_SK + SWA_KV_WIDTH
W_GQ = W_SV + SWA_KV_WIDTH
W_GK = W_GQ + GLA_QK
W_GV = W_GK + GLA_QK
W_SQ = W_GV + GLA_WIDTH
W_GZ = W_SQ + SWA_WIDTH
W_SZ = W_GZ + GLA_WIDTH
W_COLS = W_SZ + SWA_WIDTH

SWA_UNITS = (SEQ_TILE // SWA_BLOCK) * SWA_KV_HEADS

BF16 = jnp.bfloat16
F32 = jnp.float32


def _dot(a, b):
    return jnp.dot(a, b, preferred_element_type=F32)


def _dot_nt(a, b):
    return lax.dot_general(a, b, (((1,), (1,)), ((), ())), preferred_element_type=F32)


def _dot_tn(a, b):
    return lax.dot_general(a, b, (((0,), (0,)), ((), ())), preferred_element_type=F32)


def _silu(x):
    return x * (1.0 / (1.0 + jnp.exp2(x * (-LOG2E))))


def _log_sigmoid(z):
    return jnp.minimum(z, 0.0) - jnp.log(1.0 + jnp.exp2(jnp.abs(z) * (-LOG2E)))


_SRC_GA = 2 * GLA_QK + GLA_WIDTH
_SRC_GZ = _SRC_GA + GLA_GATE_RANK
_SRC_SQ = _SRC_GZ + GLA_WIDTH
_SRC_SK = _SRC_SQ + SWA_WIDTH
_SRC_SV = _SRC_SK + SWA_KV_WIDTH
_SRC_SZ = _SRC_SV + SWA_KV_WIDTH
_SRC_ROWS = _SRC_SZ + SWA_WIDTH
_REGROUP = (
    (W_SK, _SRC_SK, SWA_KV_WIDTH), (W_SV, _SRC_SV, SWA_KV_WIDTH),
    (W_GQ, 0, GLA_QK), (W_GK, GLA_QK, GLA_QK), (W_GV, 2 * GLA_QK, GLA_WIDTH),
    (W_SQ, _SRC_SQ, SWA_WIDTH), (W_GZ, _SRC_GZ, GLA_WIDTH), (W_SZ, _SRC_SZ, SWA_WIDTH),
)
W_COL_BLOCK = 256


def _setup_body(c_ref, w_ada_ref, b_ada_ref, wt_ref, w_out_ref, mod_ref, o_ref, w_out_o_ref):
    mod_ref[...] = _dot(_silu(c_ref[...]).astype(BF16), w_ada_ref[...].astype(BF16)) + b_ada_ref[...]
    w_out_o_ref[...] = w_out_ref[...].astype(BF16)
    row = lax.broadcasted_iota(jnp.int32, (LANES, 1), 0)
    ga = jnp.where(row < GLA_GATE_RANK, wt_ref[_SRC_GA:_SRC_GA + LANES, :], 0.0)
    o_ref[:, W_GA:W_GA + LANES] = jnp.transpose(ga).astype(BF16)
    for dst, src, width in _REGROUP:
        o_ref[:, dst:dst + width] = jnp.transpose(wt_ref[src:src + width, :]).astype(BF16)


def _setup(c, w_ada, b_ada, wt, w_out):
    n, d = wt.shape
    k_out, n_out = w_out.shape
    n_mod = w_ada.shape[1]
    steps = d // W_COL_BLOCK
    assert n == _SRC_ROWS and d % W_COL_BLOCK == 0 and k_out % steps == 0 and n_mod % (steps * LANES) == 0
    out_rows = k_out // steps
    mod_cols = n_mod // steps
    rows = c.shape[0]
    return pl.pallas_call(
        _setup_body,
        grid=(steps,),
        in_specs=[pl.BlockSpec((rows, w_ada.shape[0]), lambda r: (0, 0)),
                  pl.BlockSpec((w_ada.shape[0], mod_cols), lambda r: (0, r)),
                  pl.BlockSpec((1, mod_cols), lambda r: (0, r)),
                  pl.BlockSpec((n, W_COL_BLOCK), lambda r: (0, r)),
                  pl.BlockSpec((out_rows, n_out), lambda r: (r, 0))],
        out_specs=[pl.BlockSpec((rows, mod_cols), lambda r: (0, r)),
                   pl.BlockSpec((W_COL_BLOCK, W_COLS), lambda r: (r, 0)),
                   pl.BlockSpec((out_rows, n_out), lambda r: (r, 0))],
        out_shape=[jax.ShapeDtypeStruct((rows, n_mod), F32), jax.ShapeDtypeStruct((d, W_COLS), BF16),
                   jax.ShapeDtypeStruct((k_out, n_out), BF16)],
        name="setup",
    )(c, w_ada, b_ada, wt, w_out)


_OPERAND_BUFFERS = (
    ("qs", (GLA_HEADS // 2, SEQ_TILE, LANES), BF16),
    ("qb", (GLA_HEADS, SEQ_TILE, SEQ_TILE), BF16),
    ("kb", (GLA_HEADS, SEQ_TILE, SEQ_TILE), BF16),
    ("kd", (SEQ_TILE, GLA_QK), BF16),
    ("gv", (SEQ_TILE, GLA_WIDTH), BF16),
    ("dec", (GLA_QK, SEQ_TILE), F32),
    ("gg", (SEQ_TILE, GLA_WIDTH), BF16),
    ("sg", (SEQ_TILE, SWA_WIDTH), BF16),
    ("qt", (SWA_UNITS, SWA_HEAD_DIM, SWA_GROUP * SWA_BLOCK), BF16),
    ("kk", (SWA_KV_HEADS, SEQ_TILE + SWA_BLOCK, SWA_HEAD_DIM), BF16),
    ("vv", (SWA_KV_HEADS, SEQ_TILE + SWA_BLOCK, SWA_HEAD_DIM), BF16),
)


class _Operands:
    def __init__(self, refs):
        for (name, _, _), ref in zip(_OPERAND_BUFFERS, refs):
            setattr(self, name, ref)


def _layer_body(sinks_ref, x_ref, xn_ref, pos_ref, posn_ref, mod_ref, modn_ref, gnorm_ref, invf_ref, tri_ref, blk_ref,
                w_in_ref, w_dec_ref, b_dec_ref, g_gla_ref, w_out_ref, g_fin_ref,
                o_ref, state_ref, *operand_refs):
    ts = SEQ_TILE
    b_idx = pl.program_id(0)
    i = pl.program_id(1)
    last_step = i == pl.num_programs(1) - 1
    nbuf = len(_OPERAND_BUFFERS)
    ops0 = _Operands(operand_refs[:nbuf])
    ops1 = _Operands(operand_refs[nbuf:])

    gate = mod_ref[0, :, 2 * D_MODEL:3 * D_MODEL]

    def norm_modulation(m_ref):
        return (gnorm_ref[...] * (1.0 + m_ref[0, :, D_MODEL:2 * D_MODEL]), m_ref[0, :, 0:D_MODEL])

    n_chunks = ts // GLA_CHUNK
    n_blocks = ts // SWA_BLOCK
    units = [(n, kv) for n in range(n_blocks) for kv in range(SWA_KV_HEADS)]

    def proj(hb, c0, width):
        return _dot(hb, w_in_ref[:, c0:c0 + width])

    def prepare(x, pos, modulation, ops, prev_ops, starts_sequence):
        norm_gain, shift = modulation
        ms = jnp.mean(x * x, axis=-1, keepdims=True)
        hb = ((x * lax.rsqrt(ms + RMS_EPS)) * norm_gain + shift).astype(BF16)
        yield

        d0 = proj(hb, W_GA, LANES + 2 * SWA_KV_WIDTH)
        yield

        ang_t = invf_ref[...] * pos.astype(F32)
        reps = LANES // (SWA_HEAD_DIM // 2)
        cos_f = jnp.transpose(jnp.concatenate([jnp.cos(ang_t)] * reps, axis=0))
        sin_f = jnp.transpose(jnp.concatenate([jnp.sin(ang_t)] * reps, axis=0))
        lane = lax.broadcasted_iota(jnp.int32, (1, LANES), 1)
        first_half = (lane % SWA_HEAD_DIM) < (SWA_HEAD_DIM // 2)
        sin_s = jnp.where(first_half, -sin_f, sin_f)

        def rope(tv):
            rot = jnp.where(first_half,
                            pltpu.roll(tv, LANES - SWA_HEAD_DIM // 2, axis=1),
                            pltpu.roll(tv, SWA_HEAD_DIM // 2, axis=1))
            return tv * cos_f + rot * sin_s
        yield

        z = _dot(d0[:, 0:LANES].astype(BF16), w_dec_ref[...]) + b_dec_ref[...]
        k_rot = rope(d0[:, LANES:LANES + SWA_KV_WIDTH]).astype(BF16)
        for kv in range(SWA_KV_HEADS):
            hs = slice(kv * SWA_HEAD_DIM, (kv + 1) * SWA_HEAD_DIM)
            v0 = LANES + SWA_KV_WIDTH + kv * SWA_HEAD_DIM
            ops.kk[kv, SWA_BLOCK:SWA_BLOCK + ts, :] = k_rot[:, hs]
            ops.vv[kv, SWA_BLOCK:SWA_BLOCK + ts, :] = d0[:, v0:v0 + SWA_HEAD_DIM].astype(BF16)
            zeros = jnp.zeros((SWA_BLOCK, SWA_HEAD_DIM), BF16)
            if starts_sequence is True:
                k_carry, v_carry = zeros, zeros
            else:
                k_carry = prev_ops.kk[kv, ts:ts + SWA_BLOCK, :]
                v_carry = prev_ops.vv[kv, ts:ts + SWA_BLOCK, :]
                if starts_sequence is not False:
                    k_carry = jnp.where(starts_sequence, zeros, k_carry)
                    v_carry = jnp.where(starts_sequence, zeros, v_carry)
            ops.kk[kv, 0:SWA_BLOCK, :] = k_carry
            ops.vv[kv, 0:SWA_BLOCK, :] = v_carry
        yield

        log_a = _log_sigmoid(z) * (1.0 / GLA_GATE_TAU)
        la_hi = log_a.astype(BF16)
        la_lo = (log_a - la_hi.astype(F32)).astype(BF16)
        b2 = _dot(tri_ref[...], jnp.concatenate([la_hi, la_lo], axis=1))
        gqk = proj(hb, W_GQ, 2 * GLA_QK)
        gq = gqk[:, :GLA_QK]
        gk = gqk[:, GLA_QK:]
        yield
        b = b2[:, :GLA_QK] + b2[:, GLA_QK:]
        e_b = jnp.exp(b)
        e_last = jnp.concatenate(
            [jnp.broadcast_to(e_b[(c + 1) * GLA_CHUNK - 1:(c + 1) * GLA_CHUNK, :], (GLA_CHUNK, GLA_QK))
             for c in range(n_chunks)], axis=0)
        ops.dec[...] = jnp.transpose(e_last)
        q_d = gq * (GLA_DK ** -0.5) * e_b
        yield
        k_d = gk * (1.0 / e_b)
        ops.kd[...] = k_d.astype(BF16)
        k_tail = k_d * e_last

        blk = blk_ref[...]
        low_half = lax.broadcasted_iota(jnp.int32, (1, LANES), 1) < GLA_DK
        tile_reps = ts // LANES
        for pair in range(GLA_HEADS // 2):
            pr = slice(pair * LANES, (pair + 1) * LANES)
            q_pair, k_pair = q_d[:, pr], k_tail[:, pr]
            q_swap = pltpu.roll(q_pair, GLA_DK, axis=1)
            k_swap = pltpu.roll(k_pair, GLA_DK, axis=1)
            ops.qs[pair] = q_pair.astype(BF16)
            for j in range(2):
                a, b_ = (q_pair, q_swap) if j == 0 else (q_swap, q_pair)
                c, d_ = (k_pair, k_swap) if j == 0 else (k_swap, k_pair)
                q_dup = jnp.where(low_half, a, b_).astype(BF16)
                k_dup = jnp.where(low_half, c, d_).astype(BF16)
                ops.qb[2 * pair + j] = jnp.concatenate([q_dup] * tile_reps, axis=1) * blk
                ops.kb[2 * pair + j] = jnp.concatenate([k_dup] * tile_reps, axis=1) * blk
            yield

        for c0 in range(0, GLA_WIDTH, PROJ_COLS):
            ops.gv[:, c0:c0 + PROJ_COLS] = proj(hb, W_GV + c0, PROJ_COLS).astype(BF16)
            yield

        sq_all = proj(hb, W_SQ, SWA_WIDTH)
        for kv in range(SWA_KV_HEADS):
            sq = sq_all[:, kv * MXU_COLS:(kv + 1) * MXU_COLS]
            q_rot = jnp.concatenate([rope(sq[:, j * LANES:(j + 1) * LANES]) for j in range(MXU_COLS // LANES)],
                                    axis=1) * (SWA_HEAD_DIM ** -0.5 * LOG2E)
            for n in range(n_blocks):
                q_t = jnp.transpose(q_rot[n * SWA_BLOCK:(n + 1) * SWA_BLOCK, :])
                ops.qt[n * SWA_KV_HEADS + kv] = jnp.concatenate(
                    [q_t[j * SWA_HEAD_DIM:(j + 1) * SWA_HEAD_DIM, :] for j in range(SWA_GROUP)],
                    axis=1).astype(BF16)
            yield

        for c0 in range(0, GLA_WIDTH, PROJ_COLS):
            ops.gg[:, c0:c0 + PROJ_COLS] = _silu(proj(hb, W_GZ + c0, PROJ_COLS)).astype(BF16)
            yield
        for c0 in range(0, SWA_WIDTH, PROJ_COLS):
            ops.sg[:, c0:c0 + PROJ_COLS] = _silu(proj(hb, W_SZ + c0, PROJ_COLS)).astype(BF16)
            yield

    def gla(ops, mixed):
        row_i = lax.broadcasted_iota(jnp.int32, (ts, ts), 0)
        col_i = lax.broadcasted_iota(jnp.int32, (ts, ts), 1)
        causal = ((row_i // GLA_CHUNK) == (col_i // GLA_CHUNK)) & (col_i <= row_i)
        pair_lane = lax.broadcasted_iota(jnp.int32, (1, LANES), 1)
        heads = []
        for hd in range(GLA_HEADS):
            pr = slice((hd // 2) * LANES, (hd // 2 + 1) * LANES)
            v_h = ops.gv[:, hd * GLA_DV:(hd + 1) * GLA_DV]
            k_h = jnp.where((pair_lane // GLA_DK) == (hd % 2), ops.kd[:, pr], jnp.zeros((), BF16))
            s = jnp.where(causal, _dot_nt(ops.qs[hd // 2], k_h), 0.0).astype(BF16)
            u_all = _dot_tn(ops.kb[hd], v_h)
            heads.append((v_h, s, u_all))
            if hd % 2 == 1:
                yield

        staged = []
        for hd, (v_h, s, u_all) in enumerate(heads):
            st = state_ref[hd]
            entering = []
            for c in range(n_chunks):
                entering.append(st)
                last = c * GLA_CHUNK + GLA_CHUNK - 1
                d_col = ops.dec[hd * GLA_DK:(hd + 1) * GLA_DK, last:last + 1]
                st = st * d_col + u_all[c * GLA_DK:(c + 1) * GLA_DK, :]
            state_ref[hd] = st
            s_stack = jnp.concatenate(entering, axis=0).astype(BF16)
            staged.append((s, v_h, s_stack))
        yield

        outs = []
        for hd, (s, v_h, s_stack) in enumerate(staged):
            outs.append(_dot(s, v_h) + _dot(ops.qb[hd], s_stack))
            if hd % 2 == 1:
                yield

        g_gla = g_gla_ref[...]
        for hd, o_h in enumerate(outs):
            vs = slice(hd * GLA_DV, (hd + 1) * GLA_DV)
            o_ms = jnp.mean(o_h * o_h, axis=-1, keepdims=True)
            o_n = o_h * lax.rsqrt(o_ms + RMS_EPS) * g_gla[:, vs]
            mixed.append((o_n * ops.gg[:, vs].astype(F32)).astype(BF16))

    def swa(ops, mixed, first_tile):
        gq_lanes = SWA_GROUP * SWA_BLOCK
        kj = lax.broadcasted_iota(jnp.int32, (2 * SWA_BLOCK, gq_lanes), 0)
        ql = lax.broadcasted_iota(jnp.int32, (2 * SWA_BLOCK, gq_lanes), 1)
        dist = (ql % SWA_BLOCK) + SWA_BLOCK - kj
        valid = (dist >= 0) & (dist < WINDOW)
        head_of_lane = lax.broadcasted_iota(jnp.int32, (1, gq_lanes), 1) // SWA_BLOCK

        scores = []
        for u, (n, kv) in enumerate(units):
            bs = slice(n * SWA_BLOCK, (n + 2) * SWA_BLOCK)
            scores.append(_dot(ops.kk[kv, bs, :], ops.qt[u]))
            if u % 2 == 1:
                yield

        probs = []
        for (n, kv), s_raw in zip(units, scores):
            if n == 0 and first_tile is not None:
                ok = valid & ((kj >= SWA_BLOCK) | jnp.logical_not(first_tile))
            else:
                ok = valid
            s_t = s_raw + jnp.where(ok, 0.0, NEG_BIG)
            sink = jnp.zeros((1, gq_lanes), F32)
            for j in range(SWA_GROUP):
                sink = jnp.where(head_of_lane == j, sinks_ref[kv * SWA_GROUP + j] * LOG2E, sink)
            m = jnp.maximum(jnp.max(s_t, axis=0, keepdims=True), sink)
            e = jnp.exp2(s_t - m)
            den = jnp.sum(e, axis=0, keepdims=True) + jnp.exp2(sink - m)
            probs.append((e.astype(BF16), 1.0 / den))
            yield

        outs = []
        for (n, kv), (e, inv_den) in zip(units, probs):
            bs = slice(n * SWA_BLOCK, (n + 2) * SWA_BLOCK)
            outs.append(_dot_tn(ops.vv[kv, bs, :], e) * inv_den)
        yield

        for n in range(n_blocks):
            rs = slice(n * SWA_BLOCK, (n + 1) * SWA_BLOCK)
            o_groups = []
            for kv in range(SWA_KV_HEADS):
                o_t = outs[n * SWA_KV_HEADS + kv]
                o_stack = jnp.concatenate([o_t[:, j * SWA_BLOCK:(j + 1) * SWA_BLOCK] for j in range(SWA_GROUP)],
                                          axis=0)
                o_groups.append(jnp.transpose(o_stack))
            o_swa = jnp.concatenate(o_groups, axis=1)
            mixed.append((o_swa * ops.sg[rs, :].astype(F32)).astype(BF16))

    def finish(gla_heads, swa_blocks, rows):
        cat = jnp.concatenate([jnp.concatenate(gla_heads, axis=1), jnp.concatenate(swa_blocks, axis=0)], axis=1)
        y = _dot(cat, w_out_ref[...])
        yield
        x = x_ref[0, rows, :]
        xo = x + gate * y
        ms_o = jnp.mean(xo * xo, axis=-1, keepdims=True)
        o_ref[0, rows, :] = xo * lax.rsqrt(ms_o + RMS_EPS) * g_fin_ref[...]

    def interleave(primary, filler):
        live = list(primary)
        while live:
            for g in list(live):
                if next(g, StopIteration) is StopIteration:
                    live.remove(g)
                next(filler, None)
        for _ in filler:
            pass

    rows = [slice(k * ts, (k + 1) * ts) for k in range(STEP_TILES)]
    ops = (ops0, ops1)
    mixed = [([], []) for _ in range(STEP_TILES)]

    @pl.when(i == 0)
    def _():
        state_ref[...] = jnp.zeros_like(state_ref)

    @pl.when((i == 0) & (b_idx == 0))
    def _():
        for _ in prepare(x_ref[0, rows[0], :], pos_ref[0, :, rows[0]], norm_modulation(mod_ref), ops0, None, True):
            pass

    for k in range(STEP_TILES):
        cur, nxt = ops[k % 2], ops[(k + 1) % 2]
        primary = [gla(cur, mixed[k][0]), swa(cur, mixed[k][1], (i == 0) if k == 0 else None)]
        if k > 0:
            primary.insert(0, finish(*mixed[k - 1], rows[k - 1]))
        if k + 1 < STEP_TILES:
            filler = prepare(x_ref[0, rows[k + 1], :], pos_ref[0, :, rows[k + 1]], norm_modulation(mod_ref),
                             nxt, cur, False)
        else:
            filler = prepare(xn_ref[0], posn_ref[0], norm_modulation(modn_ref), nxt, cur, last_step)
        interleave(primary, filler)
    for _ in finish(*mixed[STEP_TILES - 1], rows[STEP_TILES - 1]):
        pass


def _chunk_matrices(ts):
    r = np.arange(ts)
    same = (r[:, None] // GLA_CHUNK) == (r[None, :] // GLA_CHUNK)
    return jnp.asarray(same & (r[None, :] <= r[:, None]), BF16), jnp.asarray(same, BF16)


def _rope_inv_freq_column():
    inv = 1.0 / (ROPE_THETA ** (jnp.arange(0, SWA_HEAD_DIM, 2, dtype=F32) / SWA_HEAD_DIM))
    return inv.reshape(SWA_HEAD_DIM // 2, 1)


def kernel(x, c, positions, w_ada, b_ada, g_norm, w_in, w_decay, b_decay, g_gla_head, sinks, w_out, g_final):
    B, S, D = x.shape
    ts = SEQ_TILE
    assert D == D_MODEL and S % (STEP_TILES * ts) == 0 and ts % SWA_BLOCK == 0 and STEP_TILES % 2 == 0
    assert w_ada.shape[0] == 1, "one layer"
    n_tiles = S // ts

    mod, w_in_b, w_out_b = _setup(c.astype(F32), w_ada[0], b_ada[0][None, :], jnp.transpose(w_in[0]), w_out[0])
    mod = mod.reshape(B, 1, 3 * D)
    w_dec = jnp.pad(w_decay[0].astype(BF16), ((0, LANES - GLA_GATE_RANK), (0, 0)))
    tri, blk = _chunk_matrices(ts)
    pos3 = positions.reshape(B, 1, S)

    const2 = lambda b, i, s: (0, 0)
    full = lambda a: pl.BlockSpec(a.shape, const2)
    n_steps = n_tiles // STEP_TILES

    def next_row_tile(b, i):
        wraps = i == n_steps - 1
        at_end = wraps & (b == B - 1)
        row = jnp.where(wraps & jnp.logical_not(at_end), b + 1, b)
        tile = jnp.where(wraps, jnp.where(at_end, n_tiles - 1, 0), STEP_TILES * (i + 1))
        return row, tile

    operands = [
        x, x, pos3, pos3,
        mod, mod,
        g_norm[0][None, :],
        _rope_inv_freq_column(),
        tri, blk,
        w_in_b,
        w_dec, b_decay[0][None, :], g_gla_head[0][None, :],
        w_out_b, g_final[None, :],
    ]
    in_specs = [
        pl.BlockSpec((1, STEP_TILES * ts, D), lambda b, i, s: (b, i, 0)),
        pl.BlockSpec((1, ts, D), lambda b, i, s: next_row_tile(b, i) + (0,)),
        pl.BlockSpec((1, 1, STEP_TILES * ts), lambda b, i, s: (b, 0, i)),
        pl.BlockSpec((1, 1, ts), lambda b, i, s: (next_row_tile(b, i)[0], 0, next_row_tile(b, i)[1])),
        pl.BlockSpec((1, 1, 3 * D), lambda b, i, s: (b, 0, 0)),
        pl.BlockSpec((1, 1, 3 * D), lambda b, i, s: (next_row_tile(b, i)[0], 0, 0)),
    ] + [full(a) for a in operands[6:]]

    operand_scratch = [pltpu.VMEM(shape, dtype) for _, shape, dtype in _OPERAND_BUFFERS]
    grid_spec = pltpu.PrefetchScalarGridSpec(
        num_scalar_prefetch=1,
        grid=(B, n_steps),
        in_specs=in_specs,
        out_specs=pl.BlockSpec((1, STEP_TILES * ts, D), lambda b, i, s: (b, i, 0)),
        scratch_shapes=[
            pltpu.VMEM((GLA_HEADS, GLA_DK, GLA_DV), F32),
        ] + operand_scratch + operand_scratch,
    )
    return pl.pallas_call(
        _layer_body,
        grid_spec=grid_spec,
        out_shape=jax.ShapeDtypeStruct((B, S, D), x.dtype),
        compiler_params=pltpu.CompilerParams(
            dimension_semantics=("arbitrary", "arbitrary"),
            vmem_limit_bytes=VMEM_LIMIT_BYTES,
        ),
        name="hymba_layer",
    )(sinks[0].astype(F32), *operands)
```

```python
import jax
import jax.numpy as jnp
import numpy as np
from jax import lax
from jax.experimental import pallas as pl
from jax.experimental.pallas import tpu as pltpu

D_MODEL = 1024
GLA_HEADS = 4
GLA_DK = 64
GLA_DV = 128
GLA_WIDTH = GLA_HEADS * GLA_DV
GLA_QK = GLA_HEADS * GLA_DK
GLA_GATE_RANK = 16
GLA_GATE_TAU = 16.0
GLA_CHUNK = 64
SWA_Q_HEADS = 8
SWA_KV_HEADS = 2
SWA_GROUP = SWA_Q_HEADS // SWA_KV_HEADS
SWA_HEAD_DIM = 64
SWA_WIDTH = SWA_Q_HEADS * SWA_HEAD_DIM
SWA_KV_WIDTH = SWA_KV_HEADS * SWA_HEAD_DIM
WINDOW = 128
SWA_BLOCK = 128
ROPE_THETA = 10000.0
RMS_EPS = 1e-6

LANES = 128
MXU_COLS = 256
PROJ_COLS = 2 * MXU_COLS
SEQ_TILE = 256
STEP_TILES = 2
LOG2E = 1.4426950408889634
NEG_BIG = -1e30
VMEM_LIMIT_BYTES = 40 * 1024 * 1024

W_GA = 0
W_SK = W_GA + LANES
W_SV = W_SK + SWA_KV_WIDTH
W_GQ = W_SV + SWA_KV_WIDTH
W_GK = W_GQ + GLA_QK
W_GV = W_GK + GLA_QK
W_SQ = W_GV + GLA_WIDTH
W_GZ = W_SQ + SWA_WIDTH
W_SZ = W_GZ + GLA_WIDTH
W_COLS = W_SZ + SWA_WIDTH

SWA_UNITS = (SEQ_TILE // SWA_BLOCK) * SWA_KV_HEADS

BF16 = jnp.bfloat16
F32 = jnp.float32


def _dot(a, b):
    return jnp.dot(a, b, preferred_element_type=F32)


def _dot_nt(a, b):
    return lax.dot_general(a, b, (((1,), (1,)), ((), ())), preferred_element_type=F32)


def _dot_tn(a, b):
    return lax.dot_general(a, b, (((0,), (0,)), ((), ())), preferred_element_type=F32)


def _silu(x):
    return x * (1.0 / (1.0 + jnp.exp2(x * (-LOG2E))))


def _log_sigmoid(z):
    return jnp.minimum(z, 0.0) - jnp.log(1.0 + jnp.exp2(jnp.abs(z) * (-LOG2E)))


_SRC_GA = 2 * GLA_QK + GLA_WIDTH
_SRC_GZ = _SRC_GA + GLA_GATE_RANK
_SRC_SQ = _SRC_GZ + GLA_WIDTH
_SRC_SK = _SRC_SQ + SWA_WIDTH
_SRC_SV = _SRC_SK + SWA_KV_WIDTH
_SRC_SZ = _SRC_SV + SWA_KV_WIDTH
_SRC_ROWS = _SRC_SZ + SWA_WIDTH
_REGROUP = (
    (W_SK, _SRC_SK, SWA_KV_WIDTH), (W_SV, _SRC_SV, SWA_KV_WIDTH),
    (W_GQ, 0, GLA_QK), (W_GK, GLA_QK, GLA_QK), (W_GV, 2 * GLA_QK, GLA_WIDTH),
    (W_SQ, _SRC_SQ, SWA_WIDTH), (W_GZ, _SRC_GZ, GLA_WIDTH), (W_SZ, _SRC_SZ, SWA_WIDTH),
)
W_COL_BLOCK = 256
SETUP_VMEM_LIMIT_BYTES = 24 * 1024 * 1024


def _setup_body(c_ref, w_ada_ref, b_ada_ref, wt_ref, w_out_ref, mod_ref, o_ref, w_out_o_ref):
    mod_ref[...] = _dot(_silu(c_ref[...]).astype(BF16), w_ada_ref[...].astype(BF16)) + b_ada_ref[...]
    w_out_o_ref[...] = w_out_ref[...].astype(BF16)
    row = lax.broadcasted_iota(jnp.int32, (LANES, 1), 0)
    ga = jnp.where(row < GLA_GATE_RANK, wt_ref[_SRC_GA:_SRC_GA + LANES, :], 0.0)
    o_ref[:, W_GA:W_GA + LANES] = jnp.transpose(ga).astype(BF16)
    for dst, src, width in _REGROUP:
        o_ref[:, dst:dst + width] = jnp.transpose(wt_ref[src:src + width, :]).astype(BF16)


def _setup(c, w_ada, b_ada, wt, w_out):
    n, d = wt.shape
    k_out, n_out = w_out.shape
    n_mod = w_ada.shape[1]
    steps = d // W_COL_BLOCK
    assert n == _SRC_ROWS and d % W_COL_BLOCK == 0 and k_out % steps == 0 and n_mod % (steps * LANES) == 0
    out_rows = k_out // steps
    mod_cols = n_mod // steps
    rows = c.shape[0]
    return pl.pallas_call(
        _setup_body,
        grid=(steps,),
        in_specs=[pl.BlockSpec((rows, w_ada.shape[0]), lambda r: (0, 0)),
                  pl.BlockSpec((w_ada.shape[0], mod_cols), lambda r: (0, r)),
                  pl.BlockSpec((1, mod_cols), lambda r: (0, r)),
                  pl.BlockSpec((n, W_COL_BLOCK), lambda r: (0, r)),
                  pl.BlockSpec((out_rows, n_out), lambda r: (r, 0))],
        out_specs=[pl.BlockSpec((rows, mod_cols), lambda r: (0, r)),
                   pl.BlockSpec((W_COL_BLOCK, W_COLS), lambda r: (r, 0)),
                   pl.BlockSpec((out_rows, n_out), lambda r: (r, 0))],
        out_shape=[jax.ShapeDtypeStruct((rows, n_mod), F32), jax.ShapeDtypeStruct((d, W_COLS), BF16),
                   jax.ShapeDtypeStruct((k_out, n_out), BF16)],
        compiler_params=pltpu.CompilerParams(vmem_limit_bytes=SETUP_VMEM_LIMIT_BYTES),
        name="setup",
    )(c, w_ada, b_ada, wt, w_out)


_OPERAND_BUFFERS = (
    ("qs", (GLA_HEADS // 2, SEQ_TILE, LANES), BF16),
    ("qb", (GLA_HEADS, SEQ_TILE, SEQ_TILE), BF16),
    ("kb", (GLA_HEADS, SEQ_TILE, SEQ_TILE), BF16),
    ("kd", (SEQ_TILE, GLA_QK), BF16),
    ("gv", (SEQ_TILE, GLA_WIDTH), BF16),
    ("dec", (GLA_QK, SEQ_TILE), F32),
    ("gg", (SEQ_TILE, GLA_WIDTH), BF16),
    ("sg", (SEQ_TILE, SWA_WIDTH), BF16),
    ("qt", (SWA_UNITS, SWA_HEAD_DIM, SWA_GROUP * SWA_BLOCK), BF16),
    ("kk", (SWA_KV_HEADS, SEQ_TILE + SWA_BLOCK, SWA_HEAD_DIM), BF16),
    ("vv", (SWA_KV_HEADS, SEQ_TILE + SWA_BLOCK, SWA_HEAD_DIM), BF16),
)


class _Operands:
    def __init__(self, refs):
        for (name, _, _), ref in zip(_OPERAND_BUFFERS, refs):
            setattr(self, name, ref)


def _layer_body(sinks_ref, x_ref, xn_ref, pos_ref, posn_ref, mod_ref, modn_ref, gnorm_ref, invf_ref, tri_ref, blk_ref,
                w_in_ref, w_dec_ref, b_dec_ref, g_gla_ref, w_out_ref, g_fin_ref,
                o_ref, state_ref, *operand_refs):
    ts = SEQ_TILE
    b_idx = pl.program_id(0)
    i = pl.program_id(1)
    last_step = i == pl.num_programs(1) - 1
    nbuf = len(_OPERAND_BUFFERS)
    ops0 = _Operands(operand_refs[:nbuf])
    ops1 = _Operands(operand_refs[nbuf:])

    gate = mod_ref[0, :, 2 * D_MODEL:3 * D_MODEL]

    def norm_modulation(m_ref):
        return (gnorm_ref[...] * (1.0 + m_ref[0, :, D_MODEL:2 * D_MODEL]), m_ref[0, :, 0:D_MODEL])

    n_chunks = ts // GLA_CHUNK
    n_blocks = ts // SWA_BLOCK
    units = [(n, kv) for n in range(n_blocks) for kv in range(SWA_KV_HEADS)]

    def proj(hb, c0, width):
        return _dot(hb, w_in_ref[:, c0:c0 + width])

    def prepare(x, pos, modulation, ops, prev_ops, starts_sequence):
        norm_gain, shift = modulation
        ms = jnp.mean(x * x, axis=-1, keepdims=True)
        hb = ((x * lax.rsqrt(ms + RMS_EPS)) * norm_gain + shift).astype(BF16)
        yield

        d0 = proj(hb, W_GA, LANES + 2 * SWA_KV_WIDTH)
        yield

        ang_t = invf_ref[...] * pos.astype(F32)
        reps = LANES // (SWA_HEAD_DIM // 2)
        cos_f = jnp.transpose(jnp.concatenate([jnp.cos(ang_t)] * reps, axis=0))
        sin_f = jnp.transpose(jnp.concatenate([jnp.sin(ang_t)] * reps, axis=0))
        lane = lax.broadcasted_iota(jnp.int32, (1, LANES), 1)
        first_half = (lane % SWA_HEAD_DIM) < (SWA_HEAD_DIM // 2)
        sin_s = jnp.where(first_half, -sin_f, sin_f)

        def rope(tv):
            rot = jnp.where(first_half,
                            pltpu.roll(tv, LANES - SWA_HEAD_DIM // 2, axis=1),
                            pltpu.roll(tv, SWA_HEAD_DIM // 2, axis=1))
            return tv * cos_f + rot * sin_s
        yield

        z = _dot(d0[:, 0:LANES].astype(BF16), w_dec_ref[...]) + b_dec_ref[...]
        k_rot = rope(d0[:, LANES:LANES + SWA_KV_WIDTH]).astype(BF16)
        for kv in range(SWA_KV_HEADS):
            hs = slice(kv * SWA_HEAD_DIM, (kv + 1) * SWA_HEAD_DIM)
            v0 = LANES + SWA_KV_WIDTH + kv * SWA_HEAD_DIM
            ops.kk[kv, SWA_BLOCK:SWA_BLOCK + ts, :] = k_rot[:, hs]
            ops.vv[kv, SWA_BLOCK:SWA_BLOCK + ts, :] = d0[:, v0:v0 + SWA_HEAD_DIM].astype(BF16)
            zeros = jnp.zeros((SWA_BLOCK, SWA_HEAD_DIM), BF16)
            if starts_sequence is True:
                k_carry, v_carry = zeros, zeros
            else:
                k_carry = prev_ops.kk[kv, ts:ts + SWA_BLOCK, :]
                v_carry = prev_ops.vv[kv, ts:ts + SWA_BLOCK, :]
                if starts_sequence is not False:
                    k_carry = jnp.where(starts_sequence, zeros, k_carry)
                    v_carry = jnp.where(starts_sequence, zeros, v_carry)
            ops.kk[kv, 0:SWA_BLOCK, :] = k_carry
            ops.vv[kv, 0:SWA_BLOCK, :] = v_carry
        yield

        log_a = _log_sigmoid(z) * (1.0 / GLA_GATE_TAU)
        la_hi = log_a.astype(BF16)
        la_lo = (log_a - la_hi.astype(F32)).astype(BF16)
        b2 = _dot(tri_ref[...], jnp.concatenate([la_hi, la_lo], axis=1))
        gqk = proj(hb, W_GQ, 2 * GLA_QK)
        gq = gqk[:, :GLA_QK]
        gk = gqk[:, GLA_QK:]
        yield
        b = b2[:, :GLA_QK] + b2[:, GLA_QK:]
        e_b = jnp.exp(b)
        e_last = jnp.concatenate(
            [jnp.broadcast_to(e_b[(c + 1) * GLA_CHUNK - 1:(c + 1) * GLA_CHUNK, :], (GLA_CHUNK, GLA_QK))
             for c in range(n_chunks)], axis=0)
        ops.dec[...] = jnp.transpose(e_last)
        q_d = gq * (GLA_DK ** -0.5) * e_b
        yield
        k_d = gk * (1.0 / e_b)
        ops.kd[...] = k_d.astype(BF16)
        k_tail = k_d * e_last

        blk = blk_ref[...]
        low_half = lax.broadcasted_iota(jnp.int32, (1, LANES), 1) < GLA_DK
        tile_reps = ts // LANES
        for pair in range(GLA_HEADS // 2):
            pr = slice(pair * LANES, (pair + 1) * LANES)
            q_pair, k_pair = q_d[:, pr], k_tail[:, pr]
            q_swap = pltpu.roll(q_pair, GLA_DK, axis=1)
            k_swap = pltpu.roll(k_pair, GLA_DK, axis=1)
            ops.qs[pair] = q_pair.astype(BF16)
            for j in range(2):
                a, b_ = (q_pair, q_swap) if j == 0 else (q_swap, q_pair)
                c, d_ = (k_pair, k_swap) if j == 0 else (k_swap, k_pair)
                q_dup = jnp.where(low_half, a, b_).astype(BF16)
                k_dup = jnp.where(low_half, c, d_).astype(BF16)
                ops.qb[2 * pair + j] = jnp.concatenate([q_dup] * tile_reps, axis=1) * blk
                ops.kb[2 * pair + j] = jnp.concatenate([k_dup] * tile_reps, axis=1) * blk
            yield

        for c0 in range(0, GLA_WIDTH, PROJ_COLS):
            ops.gv[:, c0:c0 + PROJ_COLS] = proj(hb, W_GV + c0, PROJ_COLS).astype(BF16)
            yield

        sq_all = proj(hb, W_SQ, SWA_WIDTH)
        for kv in range(SWA_KV_HEADS):
            sq = sq_all[:, kv * MXU_COLS:(kv + 1) * MXU_COLS]
            q_rot = jnp.concatenate([rope(sq[:, j * LANES:(j + 1) * LANES]) for j in range(MXU_COLS // LANES)],
                                    axis=1) * (SWA_HEAD_DIM ** -0.5 * LOG2E)
            for n in range(n_blocks):
                q_t = jnp.transpose(q_rot[n * SWA_BLOCK:(n + 1) * SWA_BLOCK, :])
                ops.qt[n * SWA_KV_HEADS + kv] = jnp.concatenate(
                    [q_t[j * SWA_HEAD_DIM:(j + 1) * SWA_HEAD_DIM, :] for j in range(SWA_GROUP)],
                    axis=1).astype(BF16)
            yield

        for c0 in range(0, GLA_WIDTH, PROJ_COLS):
            ops.gg[:, c0:c0 + PROJ_COLS] = _silu(proj(hb, W_GZ + c0, PROJ_COLS)).astype(BF16)
            yield
        for c0 in range(0, SWA_WIDTH, PROJ_COLS):
            ops.sg[:, c0:c0 + PROJ_COLS] = _silu(proj(hb, W_SZ + c0, PROJ_COLS)).astype(BF16)
            yield

    def gla(ops, mixed):
        row_i = lax.broadcasted_iota(jnp.int32, (ts, ts), 0)
        col_i = lax.broadcasted_iota(jnp.int32, (ts, ts), 1)
        causal = ((row_i // GLA_CHUNK) == (col_i // GLA_CHUNK)) & (col_i <= row_i)
        pair_lane = lax.broadcasted_iota(jnp.int32, (1, LANES), 1)
        heads = []
        for hd in range(GLA_HEADS):
            pr = slice((hd // 2) * LANES, (hd // 2 + 1) * LANES)
            v_h = ops.gv[:, hd * GLA_DV:(hd + 1) * GLA_DV]
            k_h = jnp.where((pair_lane // GLA_DK) == (hd % 2), ops.kd[:, pr], jnp.zeros((), BF16))
            s = jnp.where(causal, _dot_nt(ops.qs[hd // 2], k_h), 0.0).astype(BF16)
            u_all = _dot_tn(ops.kb[hd], v_h)
            heads.append((v_h, s, u_all))
            if hd % 2 == 1:
                yield

        staged = []
        for hd, (v_h, s, u_all) in enumerate(heads):
            st = state_ref[hd]
            entering = []
            for c in range(n_chunks):
                entering.append(st)
                last = c * GLA_CHUNK + GLA_CHUNK - 1
                d_col = ops.dec[hd * GLA_DK:(hd + 1) * GLA_DK, last:last + 1]
                st = st * d_col + u_all[c * GLA_DK:(c + 1) * GLA_DK, :]
            state_ref[hd] = st
            s_stack = jnp.concatenate(entering, axis=0).astype(BF16)
            staged.append((s, v_h, s_stack))
        yield

        outs = []
        for hd, (s, v_h, s_stack) in enumerate(staged):
            outs.append(_dot(s, v_h) + _dot(ops.qb[hd], s_stack))
            if hd % 2 == 1:
                yield

        g_gla = g_gla_ref[...]
        for hd, o_h in enumerate(outs):
            vs = slice(hd * GLA_DV, (hd + 1) * GLA_DV)
            o_ms = jnp.mean(o_h * o_h, axis=-1, keepdims=True)
            o_n = o_h * lax.rsqrt(o_ms + RMS_EPS) * g_gla[:, vs]
            mixed.append((o_n * ops.gg[:, vs].astype(F32)).astype(BF16))

    def swa(ops, mixed, first_tile):
        gq_lanes = SWA_GROUP * SWA_BLOCK
        kj = lax.broadcasted_iota(jnp.int32, (2 * SWA_BLOCK, gq_lanes), 0)
        ql = lax.broadcasted_iota(jnp.int32, (2 * SWA_BLOCK, gq_lanes), 1)
        dist = (ql % SWA_BLOCK) + SWA_BLOCK - kj
        valid = (dist >= 0) & (dist < WINDOW)
        head_of_lane = lax.broadcasted_iota(jnp.int32, (1, gq_lanes), 1) // SWA_BLOCK

        scores = []
        for u, (n, kv) in enumerate(units):
            bs = slice(n * SWA_BLOCK, (n + 2) * SWA_BLOCK)
            scores.append(_dot(ops.kk[kv, bs, :], ops.qt[u]))
            if u % 2 == 1:
                yield

        probs = []
        for (n, kv), s_raw in zip(units, scores):
            if n == 0 and first_tile is not None:
                ok = valid & ((kj >= SWA_BLOCK) | jnp.logical_not(first_tile))
            else:
                ok = valid
            s_t = s_raw + jnp.where(ok, 0.0, NEG_BIG)
            sink = jnp.zeros((1, gq_lanes), F32)
            for j in range(SWA_GROUP):
                sink = jnp.where(head_of_lane == j, sinks_ref[kv * SWA_GROUP + j] * LOG2E, sink)
            m = jnp.maximum(jnp.max(s_t, axis=0, keepdims=True), sink)
            e = jnp.exp2(s_t - m)
            den = jnp.sum(e, axis=0, keepdims=True) + jnp.exp2(sink - m)
            probs.append((e.astype(BF16), 1.0 / den))
            yield

        outs = []
        for (n, kv), (e, inv_den) in zip(units, probs):
            bs = slice(n * SWA_BLOCK, (n + 2) * SWA_BLOCK)
            outs.append(_dot_tn(ops.vv[kv, bs, :], e) * inv_den)
        yield

        for n in range(n_blocks):
            rs = slice(n * SWA_BLOCK, (n + 1) * SWA_BLOCK)
            o_groups = []
            for kv in range(SWA_KV_HEADS):
                o_t = outs[n * SWA_KV_HEADS + kv]
                o_stack = jnp.concatenate([o_t[:, j * SWA_BLOCK:(j + 1) * SWA_BLOCK] for j in range(SWA_GROUP)],
                                          axis=0)
                o_groups.append(jnp.transpose(o_stack))
            o_swa = jnp.concatenate(o_groups, axis=1)
            mixed.append((o_swa * ops.sg[rs, :].astype(F32)).astype(BF16))

    def finish(gla_heads, swa_blocks, rows):
        cat = jnp.concatenate([jnp.concatenate(gla_heads, axis=1), jnp.concatenate(swa_blocks, axis=0)], axis=1)
        y = _dot(cat, w_out_ref[...])
        yield
        x = x_ref[0, rows, :]
        xo = x + gate * y
        ms_o = jnp.mean(xo * xo, axis=-1, keepdims=True)
        o_ref[0, rows, :] = xo * lax.rsqrt(ms_o + RMS_EPS) * g_fin_ref[...]

    def interleave(primary, filler):
        live = list(primary)
        while live:
            for g in list(live):
                if next(g, StopIteration) is StopIteration:
                    live.remove(g)
                next(filler, None)
        for _ in filler:
            pass

    rows = [slice(k * ts, (k + 1) * ts) for k in range(STEP_TILES)]
    ops = (ops0, ops1)
    mixed = [([], []) for _ in range(STEP_TILES)]

    @pl.when(i == 0)
    def _():
        state_ref[...] = jnp.zeros_like(state_ref)

    @pl.when((i == 0) & (b_idx == 0))
    def _():
        for _ in prepare(x_ref[0, rows[0], :], pos_ref[0, :, rows[0]], norm_modulation(mod_ref), ops0, None, True):
            pass

    for k in range(STEP_TILES):
        cur, nxt = ops[k % 2], ops[(k + 1) % 2]
        primary = [gla(cur, mixed[k][0]), swa(cur, mixed[k][1], (i == 0) if k == 0 else None)]
        if k > 0:
            primary.insert(0, finish(*mixed[k - 1], rows[k - 1]))
        if k + 1 < STEP_TILES:
            filler = prepare(x_ref[0, rows[k + 1], :], pos_ref[0, :, rows[k + 1]], norm_modulation(mod_ref),
                             nxt, cur, False)
        else:
            filler = prepare(xn_ref[0], posn_ref[0], norm_modulation(modn_ref), nxt, cur, last_step)
        interleave(primary, filler)
    for _ in finish(*mixed[STEP_TILES - 1], rows[STEP_TILES - 1]):
        pass


def _chunk_matrices(ts):
    r = np.arange(ts)
    same = (r[:, None] // GLA_CHUNK) == (r[None, :] // GLA_CHUNK)
    return jnp.asarray(same & (r[None, :] <= r[:, None]), BF16), jnp.asarray(same, BF16)


def _rope_inv_freq_column():
    inv = 1.0 / (ROPE_THETA ** (jnp.arange(0, SWA_HEAD_DIM, 2, dtype=F32) / SWA_HEAD_DIM))
    return inv.reshape(SWA_HEAD_DIM // 2, 1)


def kernel(x, c, positions, w_ada, b_ada, g_norm, w_in, w_decay, b_decay, g_gla_head, sinks, w_out, g_final):
    B, S, D = x.shape
    ts = SEQ_TILE
    assert D == D_MODEL and S % (STEP_TILES * ts) == 0 and ts % SWA_BLOCK == 0 and STEP_TILES % 2 == 0
    assert w_ada.shape[0] == 1, "one layer"
    n_tiles = S // ts

    mod, w_in_b, w_out_b = _setup(c.astype(F32), w_ada[0], b_ada[0][None, :], jnp.transpose(w_in[0]), w_out[0])
    mod = mod.reshape(B, 1, 3 * D)
    w_dec = jnp.pad(w_decay[0].astype(BF16), ((0, LANES - GLA_GATE_RANK), (0, 0)))
    tri, blk = _chunk_matrices(ts)
    pos3 = positions.reshape(B, 1, S)

    const2 = lambda b, i, s: (0, 0)
    full = lambda a: pl.BlockSpec(a.shape, const2)
    n_steps = n_tiles // STEP_TILES

    def next_row_tile(b, i):
        wraps = i == n_steps - 1
        at_end = wraps & (b == B - 1)
        row = jnp.where(wraps & jnp.logical_not(at_end), b + 1, b)
        tile = jnp.where(wraps, jnp.where(at_end, n_tiles - 1, 0), STEP_TILES * (i + 1))
        return row, tile

    operands = [
        x, x, pos3, pos3,
        mod, mod,
        g_norm[0][None, :],
        _rope_inv_freq_column(),
        tri, blk,
        w_in_b,
        w_dec, b_decay[0][None, :], g_gla_head[0][None, :],
        w_out_b, g_final[None, :],
    ]
    in_specs = [
        pl.BlockSpec((1, STEP_TILES * ts, D), lambda b, i, s: (b, i, 0)),
        pl.BlockSpec((1, ts, D), lambda b, i, s: next_row_tile(b, i) + (0,)),
        pl.BlockSpec((1, 1, STEP_TILES * ts), lambda b, i, s: (b, 0, i)),
        pl.BlockSpec((1, 1, ts), lambda b, i, s: (next_row_tile(b, i)[0], 0, next_row_tile(b, i)[1])),
        pl.BlockSpec((1, 1, 3 * D), lambda b, i, s: (b, 0, 0)),
        pl.BlockSpec((1, 1, 3 * D), lambda b, i, s: (next_row_tile(b, i)[0], 0, 0)),
    ] + [full(a) for a in operands[6:]]

    operand_scratch = [pltpu.VMEM(shape, dtype) for _, shape, dtype in _OPERAND_BUFFERS]
    grid_spec = pltpu.PrefetchScalarGridSpec(
        num_scalar_prefetch=1,
        grid=(B, n_steps),
        in_specs=in_specs,
        out_specs=pl.BlockSpec((1, STEP_TILES * ts, D), lambda b, i, s: (b, i, 0)),
        scratch_shapes=[
            pltpu.VMEM((GLA_HEADS, GLA_DK, GLA_DV), F32),
        ] + operand_scratch + operand_scratch,
    )
    return pl.pallas_call(
        _layer_body,
        grid_spec=grid_spec,
        out_shape=jax.ShapeDtypeStruct((B, S, D), x.dtype),
        compiler_params=pltpu.CompilerParams(
            dimension_semantics=("arbitrary", "arbitrary"),
            vmem_limit_bytes=VMEM_LIMIT_BYTES,
        ),
        name="hymba_layer",
    )(sinks[0].astype(F32), *operands)
```

```python
import jax
import jax.numpy as jnp
import numpy as np
from jax import lax
from jax.experimental import pallas as pl
from jax.experimental.pallas import tpu as pltpu

D_MODEL = 1024
GLA_HEADS = 4
GLA_DK = 64
GLA_DV = 128
GLA_WIDTH = GLA_HEADS * GLA_DV
GLA_QK = GLA_HEADS * GLA_DK
GLA_GATE_RANK = 16
GLA_GATE_TAU = 16.0
GLA_CHUNK = 64
SWA_Q_HEADS = 8
SWA_KV_HEADS = 2
SWA_GROUP = SWA_Q_HEADS // SWA_KV_HEADS
SWA_HEAD_DIM = 64
SWA_WIDTH = SWA_Q_HEADS * SWA_HEAD_DIM
SWA_KV_WIDTH = SWA_KV_HEADS * SWA_HEAD_DIM
WINDOW = 128
SWA_BLOCK = 128
ROPE_THETA = 10000.0
RMS_EPS = 1e-6

LANES = 128
MXU_COLS = 256
PROJ_COLS = 2 * MXU_COLS
SEQ_TILE = 256
STEP_TILES = 2
LOG2E = 1.4426950408889634
NEG_BIG = -1e30
VMEM_LIMIT_BYTES = 40 * 1024 * 1024

W_GA = 0
W_SK = W_GA + LANES
W_SV = W_SK + SWA_KV_WIDTH
W_GQ = W_SV + SWA_KV_WIDTH
W_GK = W_GQ + GLA_QK
W_GV = W_GK + GLA_QK
W_SQ = W_GV + GLA_WIDTH
W_GZ = W_SQ + SWA_WIDTH
W_SZ = W_GZ + GLA_WIDTH
W_COLS = W_SZ + SWA_WIDTH

SWA_UNITS = (SEQ_TILE // SWA_BLOCK) * SWA_KV_HEADS

BF16 = jnp.bfloat16
F32 = jnp.float32


def _dot(a, b):
    return jnp.dot(a, b, preferred_element_type=F32)


def _dot_nt(a, b):
    return lax.dot_general(a, b, (((1,), (1,)), ((), ())), preferred_element_type=F32)


def _dot_tn(a, b):
    return lax.dot_general(a, b, (((0,), (0,)), ((), ())), preferred_element_type=F32)


def _silu(x):
    return x * (1.0 / (1.0 + jnp.exp2(x * (-LOG2E))))


def _log_sigmoid(z):
    return jnp.minimum(z, 0.0) - jnp.log(1.0 + jnp.exp2(jnp.abs(z) * (-LOG2E)))


_SRC_GA = 2 * GLA_QK + GLA_WIDTH
_SRC_GZ = _SRC_GA + GLA_GATE_RANK
_SRC_SQ = _SRC_GZ + GLA_WIDTH
_SRC_SK = _SRC_SQ + SWA_WIDTH
_SRC_SV = _SRC_SK + SWA_KV_WIDTH
_SRC_SZ = _SRC_SV + SWA_KV_WIDTH
_SRC_ROWS = _SRC_SZ + SWA_WIDTH
_REGROUP = (
    (W_SK, _SRC_SK, SWA_KV_WIDTH), (W_SV, _SRC_SV, SWA_KV_WIDTH),
    (W_GQ, 0, GLA_QK), (W_GK, GLA_QK, GLA_QK), (W_GV, 2 * GLA_QK, GLA_WIDTH),
    (W_SQ, _SRC_SQ, SWA_WIDTH), (W_GZ, _SRC_GZ, GLA_WIDTH), (W_SZ, _SRC_SZ, SWA_WIDTH),
)
W_COL_BLOCK = 128


def _setup_body(c_ref, w_ada_ref, b_ada_ref, wt_ref, w_out_ref, mod_ref, o_ref, w_out_o_ref):
    mod_ref[...] = _dot(_silu(c_ref[...]).astype(BF16), w_ada_ref[...].astype(BF16)) + b_ada_ref[...]
    w_out_o_ref[...] = w_out_ref[...].astype(BF16)
    row = lax.broadcasted_iota(jnp.int32, (LANES, 1), 0)
    ga = jnp.where(row < GLA_GATE_RANK, wt_ref[_SRC_GA:_SRC_GA + LANES, :], 0.0)
    o_ref[:, W_GA:W_GA + LANES] = jnp.transpose(ga).astype(BF16)
    for dst, src, width in _REGROUP:
        o_ref[:, dst:dst + width] = jnp.transpose(wt_ref[src:src + width, :]).astype(BF16)


def _setup(c, w_ada, b_ada, wt, w_out):
    n, d = wt.shape
    k_out, n_out = w_out.shape
    n_mod = w_ada.shape[1]
    steps = d // W_COL_BLOCK
    assert n == _SRC_ROWS and d % W_COL_BLOCK == 0 and k_out % steps == 0 and n_mod % (steps * LANES) == 0
    out_rows = k_out // steps
    mod_cols = n_mod // steps
    rows = c.shape[0]
    return pl.pallas_call(
        _setup_body,
        grid=(steps,),
        in_specs=[pl.BlockSpec((rows, w_ada.shape[0]), lambda r: (0, 0)),
                  pl.BlockSpec((w_ada.shape[0], mod_cols), lambda r: (0, r)),
                  pl.BlockSpec((1, mod_cols), lambda r: (0, r)),
                  pl.BlockSpec((n, W_COL_BLOCK), lambda r: (0, r)),
                  pl.BlockSpec((out_rows, n_out), lambda r: (r, 0))],
        out_specs=[pl.BlockSpec((rows, mod_cols), lambda r: (0, r)),
                   pl.BlockSpec((W_COL_BLOCK, W_COLS), lambda r: (r, 0)),
                   pl.BlockSpec((out_rows, n_out), lambda r: (r, 0))],
        out_shape=[jax.ShapeDtypeStruct((rows, n_mod), F32), jax.ShapeDtypeStruct((d, W_COLS), BF16),
                   jax.ShapeDtypeStruct((k_out, n_out), BF16)],
        name="setup",
    )(c, w_ada, b_ada, wt, w_out)


_OPERAND_BUFFERS = (
    ("qs", (GLA_HEADS // 2, SEQ_TILE, LANES), BF16),
    ("qb", (GLA_HEADS, SEQ_TILE, SEQ_TILE), BF16),
    ("kb", (GLA_HEADS, SEQ_TILE, SEQ_TILE), BF16),
    ("kd", (SEQ_TILE, GLA_QK), BF16),
    ("gv", (SEQ_TILE, GLA_WIDTH), BF16),
    ("dec", (GLA_QK, SEQ_TILE), F32),
    ("gg", (SEQ_TILE, GLA_WIDTH), BF16),
    ("sg", (SEQ_TILE, SWA_WIDTH), BF16),
    ("qt", (SWA_UNITS, SWA_HEAD_DIM, SWA_GROUP * SWA_BLOCK), BF16),
    ("kk", (SWA_KV_HEADS, SEQ_TILE + SWA_BLOCK, SWA_HEAD_DIM), BF16),
    ("vv", (SWA_KV_HEADS, SEQ_TILE + SWA_BLOCK, SWA_HEAD_DIM), BF16),
)


class _Operands:
    def __init__(self, refs):
        for (name, _, _), ref in zip(_OPERAND_BUFFERS, refs):
            setattr(self, name, ref)


def _layer_body(sinks_ref, x_ref, xn_ref, pos_ref, posn_ref, mod_ref, modn_ref, gnorm_ref, invf_ref, tri_ref, blk_ref,
                w_in_ref, w_dec_ref, b_dec_ref, g_gla_ref, w_out_ref, g_fin_ref,
                o_ref, state_ref, *operand_refs):
    ts = SEQ_TILE
    b_idx = pl.program_id(0)
    i = pl.program_id(1)
    last_step = i == pl.num_programs(1) - 1
    nbuf = len(_OPERAND_BUFFERS)
    ops0 = _Operands(operand_refs[:nbuf])
    ops1 = _Operands(operand_refs[nbuf:])

    gate = mod_ref[0, :, 2 * D_MODEL:3 * D_MODEL]

    def norm_modulation(m_ref):
        return (gnorm_ref[...] * (1.0 + m_ref[0, :, D_MODEL:2 * D_MODEL]), m_ref[0, :, 0:D_MODEL])

    n_chunks = ts // GLA_CHUNK
    n_blocks = ts // SWA_BLOCK
    units = [(n, kv) for n in range(n_blocks) for kv in range(SWA_KV_HEADS)]

    def proj(hb, c0, width):
        return _dot(hb, w_in_ref[:, c0:c0 + width])

    def prepare(x, pos, modulation, ops, prev_ops, starts_sequence):
        norm_gain, shift = modulation
        ms = jnp.mean(x * x, axis=-1, keepdims=True)
        hb = ((x * lax.rsqrt(ms + RMS_EPS)) * norm_gain + shift).astype(BF16)
        yield

        d0 = proj(hb, W_GA, LANES + 2 * SWA_KV_WIDTH)
        yield

        ang_t = invf_ref[...] * pos.astype(F32)
        reps = LANES // (SWA_HEAD_DIM // 2)
        cos_f = jnp.transpose(jnp.concatenate([jnp.cos(ang_t)] * reps, axis=0))
        sin_f = jnp.transpose(jnp.concatenate([jnp.sin(ang_t)] * reps, axis=0))
        lane = lax.broadcasted_iota(jnp.int32, (1, LANES), 1)
        first_half = (lane % SWA_HEAD_DIM) < (SWA_HEAD_DIM // 2)
        sin_s = jnp.where(first_half, -sin_f, sin_f)

        def rope(tv):
            rot = jnp.where(first_half,
                            pltpu.roll(tv, LANES - SWA_HEAD_DIM // 2, axis=1),
                            pltpu.roll(tv, SWA_HEAD_DIM // 2, axis=1))
            return tv * cos_f + rot * sin_s
        yield

        z = _dot(d0[:, 0:LANES].astype(BF16), w_dec_ref[...]) + b_dec_ref[...]
        k_rot = rope(d0[:, LANES:LANES + SWA_KV_WIDTH]).astype(BF16)
        for kv in range(SWA_KV_HEADS):
            hs = slice(kv * SWA_HEAD_DIM, (kv + 1) * SWA_HEAD_DIM)
            v0 = LANES + SWA_KV_WIDTH + kv * SWA_HEAD_DIM
            ops.kk[kv, SWA_BLOCK:SWA_BLOCK + ts, :] = k_rot[:, hs]
            ops.vv[kv, SWA_BLOCK:SWA_BLOCK + ts, :] = d0[:, v0:v0 + SWA_HEAD_DIM].astype(BF16)
            zeros = jnp.zeros((SWA_BLOCK, SWA_HEAD_DIM), BF16)
            if starts_sequence is True:
                k_carry, v_carry = zeros, zeros
            else:
                k_carry = prev_ops.kk[kv, ts:ts + SWA_BLOCK, :]
                v_carry = prev_ops.vv[kv, ts:ts + SWA_BLOCK, :]
                if starts_sequence is not False:
                    k_carry = jnp.where(starts_sequence, zeros, k_carry)
                    v_carry = jnp.where(starts_sequence, zeros, v_carry)
            ops.kk[kv, 0:SWA_BLOCK, :] = k_carry
            ops.vv[kv, 0:SWA_BLOCK, :] = v_carry
        yield

        log_a = _log_sigmoid(z) * (1.0 / GLA_GATE_TAU)
        la_hi = log_a.astype(BF16)
        la_lo = (log_a - la_hi.astype(F32)).astype(BF16)
        b2 = _dot(tri_ref[...], jnp.concatenate([la_hi, la_lo], axis=1))
        gqk = proj(hb, W_GQ, 2 * GLA_QK)
        gq = gqk[:, :GLA_QK]
        gk = gqk[:, GLA_QK:]
        yield
        b = b2[:, :GLA_QK] + b2[:, GLA_QK:]
        e_b = jnp.exp(b)
        e_last = jnp.concatenate(
            [jnp.broadcast_to(e_b[(c + 1) * GLA_CHUNK - 1:(c + 1) * GLA_CHUNK, :], (GLA_CHUNK, GLA_QK))
             for c in range(n_chunks)], axis=0)
        ops.dec[...] = jnp.transpose(e_last)
        q_d = gq * (GLA_DK ** -0.5) * e_b
        yield
        k_d = gk * (1.0 / e_b)
        ops.kd[...] = k_d.astype(BF16)
        k_tail = k_d * e_last

        blk = blk_ref[...]
        low_half = lax.broadcasted_iota(jnp.int32, (1, LANES), 1) < GLA_DK
        tile_reps = ts // LANES
        for pair in range(GLA_HEADS // 2):
            pr = slice(pair * LANES, (pair + 1) * LANES)
            q_pair, k_pair = q_d[:, pr], k_tail[:, pr]
            q_swap = pltpu.roll(q_pair, GLA_DK, axis=1)
            k_swap = pltpu.roll(k_pair, GLA_DK, axis=1)
            ops.qs[pair] = q_pair.astype(BF16)
            for j in range(2):
                a, b_ = (q_pair, q_swap) if j == 0 else (q_swap, q_pair)
                c, d_ = (k_pair, k_swap) if j == 0 else (k_swap, k_pair)
                q_dup = jnp.where(low_half, a, b_).astype(BF16)
                k_dup = jnp.where(low_half, c, d_).astype(BF16)
                ops.qb[2 * pair + j] = jnp.concatenate([q_dup] * tile_reps, axis=1) * blk
                ops.kb[2 * pair + j] = jnp.concatenate([k_dup] * tile_reps, axis=1) * blk
            yield

        for c0 in range(0, GLA_WIDTH, PROJ_COLS):
            ops.gv[:, c0:c0 + PROJ_COLS] = proj(hb, W_GV + c0, PROJ_COLS).astype(BF16)
            yield

        sq_all = proj(hb, W_SQ, SWA_WIDTH)
        for kv in range(SWA_KV_HEADS):
            sq = sq_all[:, kv * MXU_COLS:(kv + 1) * MXU_COLS]
            q_rot = jnp.concatenate([rope(sq[:, j * LANES:(j + 1) * LANES]) for j in range(MXU_COLS // LANES)],
                                    axis=1) * (SWA_HEAD_DIM ** -0.5 * LOG2E)
            for n in range(n_blocks):
                q_t = jnp.transpose(q_rot[n * SWA_BLOCK:(n + 1) * SWA_BLOCK, :])
                ops.qt[n * SWA_KV_HEADS + kv] = jnp.concatenate(
                    [q_t[j * SWA_HEAD_DIM:(j + 1) * SWA_HEAD_DIM, :] for j in range(SWA_GROUP)],
                    axis=1).astype(BF16)
            yield

        for c0 in range(0, GLA_WIDTH, PROJ_COLS):
            ops.gg[:, c0:c0 + PROJ_COLS] = _silu(proj(hb, W_GZ + c0, PROJ_COLS)).astype(BF16)
            yield
        for c0 in range(0, SWA_WIDTH, PROJ_COLS):
            ops.sg[:, c0:c0 + PROJ_COLS] = _silu(proj(hb, W_SZ + c0, PROJ_COLS)).astype(BF16)
            yield

    def gla(ops, mixed):
        row_i = lax.broadcasted_iota(jnp.int32, (ts, ts), 0)
        col_i = lax.broadcasted_iota(jnp.int32, (ts, ts), 1)
        causal = ((row_i // GLA_CHUNK) == (col_i // GLA_CHUNK)) & (col_i <= row_i)
        pair_lane = lax.broadcasted_iota(jnp.int32, (1, LANES), 1)
        heads = []
        for hd in range(GLA_HEADS):
            pr = slice((hd // 2) * LANES, (hd // 2 + 1) * LANES)
            v_h = ops.gv[:, hd * GLA_DV:(hd + 1) * GLA_DV]
            k_h = jnp.where((pair_lane // GLA_DK) == (hd % 2), ops.kd[:, pr], jnp.zeros((), BF16))
            s = jnp.where(causal, _dot_nt(ops.qs[hd // 2], k_h), 0.0).astype(BF16)
            u_all = _dot_tn(ops.kb[hd], v_h)
            heads.append((v_h, s, u_all))
            if hd % 2 == 1:
                yield

        staged = []
        for hd, (v_h, s, u_all) in enumerate(heads):
            st = state_ref[hd]
            entering = []
            for c in range(n_chunks):
                entering.append(st)
                last = c * GLA_CHUNK + GLA_CHUNK - 1
                d_col = ops.dec[hd * GLA_DK:(hd + 1) * GLA_DK, last:last + 1]
                st = st * d_col + u_all[c * GLA_DK:(c + 1) * GLA_DK, :]
            state_ref[hd] = st
            s_stack = jnp.concatenate(entering, axis=0).astype(BF16)
            staged.append((s, v_h, s_stack))
        yield

        outs = []
        for hd, (s, v_h, s_stack) in enumerate(staged):
            outs.append(_dot(s, v_h) + _dot(ops.qb[hd], s_stack))
            if hd % 2 == 1:
                yield

        g_gla = g_gla_ref[...]
        for hd, o_h in enumerate(outs):
            vs = slice(hd * GLA_DV, (hd + 1) * GLA_DV)
            o_ms = jnp.mean(o_h * o_h, axis=-1, keepdims=True)
            o_n = o_h * lax.rsqrt(o_ms + RMS_EPS) * g_gla[:, vs]
            mixed.append((o_n * ops.gg[:, vs].astype(F32)).astype(BF16))

    def swa(ops, mixed, first_tile):
        gq_lanes = SWA_GROUP * SWA_BLOCK
        kj = lax.broadcasted_iota(jnp.int32, (2 * SWA_BLOCK, gq_lanes), 0)
        ql = lax.broadcasted_iota(jnp.int32, (2 * SWA_BLOCK, gq_lanes), 1)
        dist = (ql % SWA_BLOCK) + SWA_BLOCK - kj
        valid = (dist >= 0) & (dist < WINDOW)
        head_of_lane = lax.broadcasted_iota(jnp.int32, (1, gq_lanes), 1) // SWA_BLOCK

        scores = []
        for u, (n, kv) in enumerate(units):
            bs = slice(n * SWA_BLOCK, (n + 2) * SWA_BLOCK)
            scores.append(_dot(ops.kk[kv, bs, :], ops.qt[u]))
            if u % 2 == 1:
                yield

        probs = []
        for (n, kv), s_raw in zip(units, scores):
            if n == 0 and first_tile is not None:
                ok = valid & ((kj >= SWA_BLOCK) | jnp.logical_not(first_tile))
            else:
                ok = valid
            s_t = s_raw + jnp.where(ok, 0.0, NEG_BIG)
            sink = jnp.zeros((1, gq_lanes), F32)
            for j in range(SWA_GROUP):
                sink = jnp.where(head_of_lane == j, sinks_ref[kv * SWA_GROUP + j] * LOG2E, sink)
            m = jnp.maximum(jnp.max(s_t, axis=0, keepdims=True), sink)
            e = jnp.exp2(s_t - m)
            den = jnp.sum(e, axis=0, keepdims=True) + jnp.exp2(sink - m)
            probs.append((e.astype(BF16), 1.0 / den))
            yield

        outs = []
        for (n, kv), (e, inv_den) in zip(units, probs):
            bs = slice(n * SWA_BLOCK, (n + 2) * SWA_BLOCK)
            outs.append(_dot_tn(ops.vv[kv, bs, :], e) * inv_den)
        yield

        for n in range(n_blocks):
            rs = slice(n * SWA_BLOCK, (n + 1) * SWA_BLOCK)
            o_groups = []
            for kv in range(SWA_KV_HEADS):
                o_t = outs[n * SWA_KV_HEADS + kv]
                o_stack = jnp.concatenate([o_t[:, j * SWA_BLOCK:(j + 1) * SWA_BLOCK] for j in range(SWA_GROUP)],
                                          axis=0)
                o_groups.append(jnp.transpose(o_stack))
            o_swa = jnp.concatenate(o_groups, axis=1)
            mixed.append((o_swa * ops.sg[rs, :].astype(F32)).astype(BF16))

    def finish(gla_heads, swa_blocks, rows):
        cat = jnp.concatenate([jnp.concatenate(gla_heads, axis=1), jnp.concatenate(swa_blocks, axis=0)], axis=1)
        y = _dot(cat, w_out_ref[...])
        yield
        x = x_ref[0, rows, :]
        xo = x + gate * y
        ms_o = jnp.mean(xo * xo, axis=-1, keepdims=True)
        o_ref[0, rows, :] = xo * lax.rsqrt(ms_o + RMS_EPS) * g_fin_ref[...]

    def interleave(primary, filler):
        live = list(primary)
        while live:
            for g in list(live):
                if next(g, StopIteration) is StopIteration:
                    live.remove(g)
                next(filler, None)
        for _ in filler:
            pass

    rows = [slice(k * ts, (k + 1) * ts) for k in range(STEP_TILES)]
    ops = (ops0, ops1)
    mixed = [([], []) for _ in range(STEP_TILES)]

    @pl.when(i == 0)
    def _():
        state_ref[...] = jnp.zeros_like(state_ref)

    @pl.when((i == 0) & (b_idx == 0))
    def _():
        for _ in prepare(x_ref[0, rows[0], :], pos_ref[0, :, rows[0]], norm_modulation(mod_ref), ops0, None, True):
            pass

    for k in range(STEP_TILES):
        cur, nxt = ops[k % 2], ops[(k + 1) % 2]
        primary = [gla(cur, mixed[k][0]), swa(cur, mixed[k][1], (i == 0) if k == 0 else None)]
        if k > 0:
            primary.insert(0, finish(*mixed[k - 1], rows[k - 1]))
        if k + 1 < STEP_TILES:
            filler = prepare(x_ref[0, rows[k + 1], :], pos_ref[0, :, rows[k + 1]], norm_modulation(mod_ref),
                             nxt, cur, False)
        else:
            filler = prepare(xn_ref[0], posn_ref[0], norm_modulation(modn_ref), nxt, cur, last_step)
        interleave(primary, filler)
    for _ in finish(*mixed[STEP_TILES - 1], rows[STEP_TILES - 1]):
        pass


def _chunk_matrices(ts):
    r = np.arange(ts)
    same = (r[:, None] // GLA_CHUNK) == (r[None, :] // GLA_CHUNK)
    return jnp.asarray(same & (r[None, :] <= r[:, None]), BF16), jnp.asarray(same, BF16)


def _rope_inv_freq_column():
    inv = 1.0 / (ROPE_THETA ** (jnp.arange(0, SWA_HEAD_DIM, 2, dtype=F32) / SWA_HEAD_DIM))
    return inv.reshape(SWA_HEAD_DIM // 2, 1)


def kernel(x, c, positions, w_ada, b_ada, g_norm, w_in, w_decay, b_decay, g_gla_head, sinks, w_out, g_final):
    B, S, D = x.shape
    ts = SEQ_TILE
    assert D == D_MODEL and S % (STEP_TILES * ts) == 0 and ts % SWA_BLOCK == 0 and STEP_TILES % 2 == 0
    assert w_ada.shape[0] == 1, "one layer"
    n_tiles = S // ts

    mod, w_in_b, w_out_b = _setup(c.astype(F32), w_ada[0], b_ada[0][None, :], jnp.transpose(w_in[0]), w_out[0])
    mod = mod.reshape(B, 1, 3 * D)
    w_dec = jnp.pad(w_decay[0].astype(BF16), ((0, LANES - GLA_GATE_RANK), (0, 0)))
    tri, blk = _chunk_matrices(ts)
    pos3 = positions.reshape(B, 1, S)

    const2 = lambda b, i, s: (0, 0)
    full = lambda a: pl.BlockSpec(a.shape, const2)
    n_steps = n_tiles // STEP_TILES

    def next_row_tile(b, i):
        wraps = i == n_steps - 1
        at_end = wraps & (b == B - 1)
        row = jnp.where(wraps & jnp.logical_not(at_end), b + 1, b)
        tile = jnp.where(wraps, jnp.where(at_end, n_tiles - 1, 0), STEP_TILES * (i + 1))
        return row, tile

    operands = [
        x, x, pos3, pos3,
        mod, mod,
        g_norm[0][None, :],
        _rope_inv_freq_column(),
        tri, blk,
        w_in_b,
        w_dec, b_decay[0][None, :], g_gla_head[0][None, :],
        w_out_b, g_final[None, :],
    ]
    in_specs = [
        pl.BlockSpec((1, STEP_TILES * ts, D), lambda b, i, s: (b, i, 0)),
        pl.BlockSpec((1, ts, D), lambda b, i, s: next_row_tile(b, i) + (0,)),
        pl.BlockSpec((1, 1, STEP_TILES * ts), lambda b, i, s: (b, 0, i)),
        pl.BlockSpec((1, 1, ts), lambda b, i, s: (next_row_tile(b, i)[0], 0, next_row_tile(b, i)[1])),
        pl.BlockSpec((1, 1, 3 * D), lambda b, i, s: (b, 0, 0)),
        pl.BlockSpec((1, 1, 3 * D), lambda b, i, s: (next_row_tile(b, i)[0], 0, 0)),
    ] + [full(a) for a in operands[6:]]

    operand_scratch = [pltpu.VMEM(shape, dtype) for _, shape, dtype in _OPERAND_BUFFERS]
    grid_spec = pltpu.PrefetchScalarGridSpec(
        num_scalar_prefetch=1,
        grid=(B, n_steps),
        in_specs=in_specs,
        out_specs=pl.BlockSpec((1, STEP_TILES * ts, D), lambda b, i, s: (b, i, 0)),
        scratch_shapes=[
            pltpu.VMEM((GLA_HEADS, GLA_DK, GLA_DV), F32),
        ] + operand_scratch + operand_scratch,
    )
    return pl.pallas_call(
        _layer_body,
        grid_spec=grid_spec,
        out_shape=jax.ShapeDtypeStruct((B, S, D), x.dtype),
        compiler_params=pltpu.CompilerParams(
            dimension_semantics=("arbitrary", "arbitrary"),
            vmem_limit_bytes=VMEM_LIMIT_BYTES,
        ),
        name="hymba_layer",
    )(sinks[0].astype(F32), *operands)
```

```python
import jax
import jax.numpy as jnp
import numpy as np
from jax import lax
from jax.experimental import pallas as pl
from jax.experimental.pallas import tpu as pltpu

D_MODEL = 1024
GLA_HEADS = 4
GLA_DK = 64
GLA_DV = 128
GLA_WIDTH = GLA_HEADS * GLA_DV
GLA_QK = GLA_HEADS * GLA_DK
GLA_GATE_RANK = 16
GLA_GATE_TAU = 16.0
GLA_CHUNK = 64
SWA_Q_HEADS = 8
SWA_KV_HEADS = 2
SWA_GROUP = SWA_Q_HEADS // SWA_KV_HEADS
SWA_HEAD_DIM = 64
SWA_WIDTH = SWA_Q_HEADS * SWA_HEAD_DIM
SWA_KV_WIDTH = SWA_KV_HEADS * SWA_HEAD_DIM
WINDOW = 128
SWA_BLOCK = 128
ROPE_THETA = 10000.0
RMS_EPS = 1e-6

LANES = 128
MXU_COLS = 256
PROJ_COLS = 2 * MXU_COLS
SEQ_TILE = 256
STEP_TILES = 2
LOG2E = 1.4426950408889634
NEG_BIG = -1e30
VMEM_LIMIT_BYTES = 40 * 1024 * 1024

W_GA = 0
W_SK = W_GA + LANES
W_SV = W_SK + SWA_KV_WIDTH
W_GQ = W_SV + SWA_KV_WIDTH
W_GK = W_GQ + GLA_QK
W_GV = W_GK + GLA_QK
W_SQ = W_GV + GLA_WIDTH
W_GZ = W_SQ + SWA_WIDTH
W_SZ = W_GZ + GLA_WIDTH
W_COLS = W_SZ + SWA_WIDTH

SWA_UNITS = (SEQ_TILE // SWA_BLOCK) * SWA_KV_HEADS

BF16 = jnp.bfloat16
F32 = jnp.float32


def _dot(a, b):
    return jnp.dot(a, b, preferred_element_type=F32)


def _dot_nt(a, b):
    return lax.dot_general(a, b, (((1,), (1,)), ((), ())), preferred_element_type=F32)


def _dot_tn(a, b):
    return lax.dot_general(a, b, (((0,), (0,)), ((), ())), preferred_element_type=F32)


def _silu(x):
    return x * (1.0 / (1.0 + jnp.exp2(x * (-LOG2E))))


def _log_sigmoid(z):
    return jnp.minimum(z, 0.0) - jnp.log(1.0 + jnp.exp2(jnp.abs(z) * (-LOG2E)))


_SRC_GA = 2 * GLA_QK + GLA_WIDTH
_SRC_GZ = _SRC_GA + GLA_GATE_RANK
_SRC_SQ = _SRC_GZ + GLA_WIDTH
_SRC_SK = _SRC_SQ + SWA_WIDTH
_SRC_SV = _SRC_SK + SWA_KV_WIDTH
_SRC_SZ = _SRC_SV + SWA_KV_WIDTH
_SRC_ROWS = _SRC_SZ + SWA_WIDTH
_REGROUP = (
    (W_SK, _SRC_SK, SWA_KV_WIDTH), (W_SV, _SRC_SV, SWA_KV_WIDTH),
    (W_GQ, 0, GLA_QK), (W_GK, GLA_QK, GLA_QK), (W_GV, 2 * GLA_QK, GLA_WIDTH),
    (W_SQ, _SRC_SQ, SWA_WIDTH), (W_GZ, _SRC_GZ, GLA_WIDTH), (W_SZ, _SRC_SZ, SWA_WIDTH),
)
W_COL_BLOCK = 256


def _setup_body(c_ref, w_ada_ref, b_ada_ref, wt_ref, w_out_ref, mod_ref, o_ref, w_out_o_ref):
    mod_ref[...] = _dot(_silu(c_ref[...]).astype(BF16), w_ada_ref[...].astype(BF16)) + b_ada_ref[...]
    w_out_o_ref[...] = w_out_ref[...].astype(BF16)
    row = lax.broadcasted_iota(jnp.int32, (LANES, 1), 0)
    ga = jnp.where(row < GLA_GATE_RANK, wt_ref[_SRC_GA:_SRC_GA + LANES, :], 0.0)
    o_ref[:, W_GA:W_GA + LANES] = jnp.transpose(ga).astype(BF16)
    for dst, src, width in _REGROUP:
        o_ref[:, dst:dst + width] = jnp.transpose(wt_ref[src:src + width, :]).astype(BF16)


def _setup(c, w_ada, b_ada, wt, w_out):
    n, d = wt.shape
    k_out, n_out = w_out.shape
    n_mod = w_ada.shape[1]
    steps = d // W_COL_BLOCK
    assert n == _SRC_ROWS and d % W_COL_BLOCK == 0 and k_out % steps == 0 and n_mod % (steps * LANES) == 0
    out_rows = k_out // steps
    mod_cols = n_mod // steps
    rows = c.shape[0]
    return pl.pallas_call(
        _setup_body,
        grid=(steps,),
        in_specs=[pl.BlockSpec((rows, w_ada.shape[0]), lambda r: (0, 0)),
                  pl.BlockSpec((w_ada.shape[0], mod_cols), lambda r: (0, r)),
                  pl.BlockSpec((1, mod_cols), lambda r: (0, r)),
                  pl.BlockSpec((n, W_COL_BLOCK), lambda r: (0, r)),
                  pl.BlockSpec((out_rows, n_out), lambda r: (r, 0))],
        out_specs=[pl.BlockSpec((rows, mod_cols), lambda r: (0, r)),
                   pl.BlockSpec((W_COL_BLOCK, W_COLS), lambda r: (r, 0)),
                   pl.BlockSpec((out_rows, n_out), lambda r: (r, 0))],
        out_shape=[jax.ShapeDtypeStruct((rows, n_mod), F32), jax.ShapeDtypeStruct((d, W_COLS), BF16),
                   jax.ShapeDtypeStruct((k_out, n_out), BF16)],
        name="setup",
    )(c, w_ada, b_ada, wt, w_out)


_OPERAND_BUFFERS = (
    ("qs", (GLA_HEADS // 2, SEQ_TILE, LANES), BF16),
    ("qb", (GLA_HEADS, SEQ_TILE, SEQ_TILE), BF16),
    ("kb", (GLA_HEADS, SEQ_TILE, SEQ_TILE), BF16),
    ("kd", (SEQ_TILE, GLA_QK), BF16),
    ("gv", (SEQ_TILE, GLA_WIDTH), BF16),
    ("dec", (GLA_QK, SEQ_TILE), F32),
    ("gg", (SEQ_TILE, GLA_WIDTH), BF16),
    ("sg", (SEQ_TILE, SWA_WIDTH), BF16),
    ("qt", (SWA_UNITS, SWA_HEAD_DIM, SWA_GROUP * SWA_BLOCK), BF16),
    ("kk", (SWA_KV_HEADS, SEQ_TILE + SWA_BLOCK, SWA_HEAD_DIM), BF16),
    ("vv", (SWA_KV_HEADS, SEQ_TILE + SWA_BLOCK, SWA_HEAD_DIM), BF16),
)


class _Operands:
    def __init__(self, refs):
        for (name, _, _), ref in zip(_OPERAND_BUFFERS, refs):
            setattr(self, name, ref)


def _layer_body(sinks_ref, x_ref, xn_ref, pos_ref, posn_ref, mod_ref, gnorm_ref, invf_ref, tri_ref, blk_ref,
                w_in_ref, w_dec_ref, b_dec_ref, g_gla_ref, w_out_ref, g_fin_ref,
                o_ref, state_ref, *operand_refs):
    ts = SEQ_TILE
    b_idx = pl.program_id(0)
    i = pl.program_id(1)
    last_step = i == pl.num_programs(1) - 1
    nbuf = len(_OPERAND_BUFFERS)
    ops0 = _Operands(operand_refs[:nbuf])
    ops1 = _Operands(operand_refs[nbuf:])

    next_b = jnp.where(last_step, jnp.minimum(b_idx + 1, pl.num_programs(0) - 1), b_idx)

    def mod_part(b, k):
        return mod_ref[pl.ds(b, 1), k * D_MODEL:(k + 1) * D_MODEL]

    gate = mod_part(b_idx, 2)

    def norm_modulation(b):
        return (gnorm_ref[...] * (1.0 + mod_part(b, 1)), mod_part(b, 0))

    n_chunks = ts // GLA_CHUNK
    n_blocks = ts // SWA_BLOCK
    units = [(n, kv) for n in range(n_blocks) for kv in range(SWA_KV_HEADS)]

    def proj(hb, c0, width):
        return _dot(hb, w_in_ref[:, c0:c0 + width])

    def prepare(x, pos, modulation, ops, prev_ops, starts_sequence):
        norm_gain, shift = modulation
        ms = jnp.mean(x * x, axis=-1, keepdims=True)
        hb = ((x * lax.rsqrt(ms + RMS_EPS)) * norm_gain + shift).astype(BF16)
        yield

        d0 = proj(hb, W_GA, LANES + 2 * SWA_KV_WIDTH)
        yield

        ang_t = invf_ref[...] * pos.astype(F32)
        reps = LANES // (SWA_HEAD_DIM // 2)
        cos_f = jnp.transpose(jnp.concatenate([jnp.cos(ang_t)] * reps, axis=0))
        sin_f = jnp.transpose(jnp.concatenate([jnp.sin(ang_t)] * reps, axis=0))
        lane = lax.broadcasted_iota(jnp.int32, (1, LANES), 1)
        first_half = (lane % SWA_HEAD_DIM) < (SWA_HEAD_DIM // 2)
        sin_s = jnp.where(first_half, -sin_f, sin_f)

        def rope(tv):
            rot = jnp.where(first_half,
                            pltpu.roll(tv, LANES - SWA_HEAD_DIM // 2, axis=1),
                            pltpu.roll(tv, SWA_HEAD_DIM // 2, axis=1))
            return tv * cos_f + rot * sin_s
        yield

        z = _dot(d0[:, 0:LANES].astype(BF16), w_dec_ref[...]) + b_dec_ref[...]
        k_rot = rope(d0[:, LANES:LANES + SWA_KV_WIDTH]).astype(BF16)
        for kv in range(SWA_KV_HEADS):
            hs = slice(kv * SWA_HEAD_DIM, (kv + 1) * SWA_HEAD_DIM)
            v0 = LANES + SWA_KV_WIDTH + kv * SWA_HEAD_DIM
            ops.kk[kv, SWA_BLOCK:SWA_BLOCK + ts, :] = k_rot[:, hs]
            ops.vv[kv, SWA_BLOCK:SWA_BLOCK + ts, :] = d0[:, v0:v0 + SWA_HEAD_DIM].astype(BF16)
            zeros = jnp.zeros((SWA_BLOCK, SWA_HEAD_DIM), BF16)
            if starts_sequence is True:
                k_carry, v_carry = zeros, zeros
            else:
                k_carry = prev_ops.kk[kv, ts:ts + SWA_BLOCK, :]
                v_carry = prev_ops.vv[kv, ts:ts + SWA_BLOCK, :]
                if starts_sequence is not False:
                    k_carry = jnp.where(starts_sequence, zeros, k_carry)
                    v_carry = jnp.where(starts_sequence, zeros, v_carry)
            ops.kk[kv, 0:SWA_BLOCK, :] = k_carry
            ops.vv[kv, 0:SWA_BLOCK, :] = v_carry
        yield

        log_a = _log_sigmoid(z) * (1.0 / GLA_GATE_TAU)
        la_hi = log_a.astype(BF16)
        la_lo = (log_a - la_hi.astype(F32)).astype(BF16)
        b2 = _dot(tri_ref[...], jnp.concatenate([la_hi, la_lo], axis=1))
        gqk = proj(hb, W_GQ, 2 * GLA_QK)
        gq = gqk[:, :GLA_QK]
        gk = gqk[:, GLA_QK:]
        yield
        b = b2[:, :GLA_QK] + b2[:, GLA_QK:]
        e_b = jnp.exp(b)
        e_last = jnp.concatenate(
            [jnp.broadcast_to(e_b[(c + 1) * GLA_CHUNK - 1:(c + 1) * GLA_CHUNK, :], (GLA_CHUNK, GLA_QK))
             for c in range(n_chunks)], axis=0)
        ops.dec[...] = jnp.transpose(e_last)
        q_d = gq * (GLA_DK ** -0.5) * e_b
        yield
        k_d = gk * (1.0 / e_b)
        ops.kd[...] = k_d.astype(BF16)
        k_tail = k_d * e_last

        blk = blk_ref[...]
        low_half = lax.broadcasted_iota(jnp.int32, (1, LANES), 1) < GLA_DK
        tile_reps = ts // LANES
        for pair in range(GLA_HEADS // 2):
            pr = slice(pair * LANES, (pair + 1) * LANES)
            q_pair, k_pair = q_d[:, pr], k_tail[:, pr]
            q_swap = pltpu.roll(q_pair, GLA_DK, axis=1)
            k_swap = pltpu.roll(k_pair, GLA_DK, axis=1)
            ops.qs[pair] = q_pair.astype(BF16)
            for j in range(2):
                a, b_ = (q_pair, q_swap) if j == 0 else (q_swap, q_pair)
                c, d_ = (k_pair, k_swap) if j == 0 else (k_swap, k_pair)
                q_dup = jnp.where(low_half, a, b_).astype(BF16)
                k_dup = jnp.where(low_half, c, d_).astype(BF16)
                ops.qb[2 * pair + j] = jnp.concatenate([q_dup] * tile_reps, axis=1) * blk
                ops.kb[2 * pair + j] = jnp.concatenate([k_dup] * tile_reps, axis=1) * blk
            yield

        for c0 in range(0, GLA_WIDTH, PROJ_COLS):
            ops.gv[:, c0:c0 + PROJ_COLS] = proj(hb, W_GV + c0, PROJ_COLS).astype(BF16)
            yield

        sq_all = proj(hb, W_SQ, SWA_WIDTH)
        for kv in range(SWA_KV_HEADS):
            sq = sq_all[:, kv * MXU_COLS:(kv + 1) * MXU_COLS]
            q_rot = jnp.concatenate([rope(sq[:, j * LANES:(j + 1) * LANES]) for j in range(MXU_COLS // LANES)],
                                    axis=1) * (SWA_HEAD_DIM ** -0.5 * LOG2E)
            for n in range(n_blocks):
                q_t = jnp.transpose(q_rot[n * SWA_BLOCK:(n + 1) * SWA_BLOCK, :])
                ops.qt[n * SWA_KV_HEADS + kv] = jnp.concatenate(
                    [q_t[j * SWA_HEAD_DIM:(j + 1) * SWA_HEAD_DIM, :] for j in range(SWA_GROUP)],
                    axis=1).astype(BF16)
            yield

        for c0 in range(0, GLA_WIDTH, PROJ_COLS):
            ops.gg[:, c0:c0 + PROJ_COLS] = _silu(proj(hb, W_GZ + c0, PROJ_COLS)).astype(BF16)
            yield
        for c0 in range(0, SWA_WIDTH, PROJ_COLS):
            ops.sg[:, c0:c0 + PROJ_COLS] = _silu(proj(hb, W_SZ + c0, PROJ_COLS)).astype(BF16)
            yield

    def gla(ops, mixed):
        row_i = lax.broadcasted_iota(jnp.int32, (ts, ts), 0)
        col_i = lax.broadcasted_iota(jnp.int32, (ts, ts), 1)
        causal = ((row_i // GLA_CHUNK) == (col_i // GLA_CHUNK)) & (col_i <= row_i)
        pair_lane = lax.broadcasted_iota(jnp.int32, (1, LANES), 1)
        heads = []
        for hd in range(GLA_HEADS):
            pr = slice((hd // 2) * LANES, (hd // 2 + 1) * LANES)
            v_h = ops.gv[:, hd * GLA_DV:(hd + 1) * GLA_DV]
            k_h = jnp.where((pair_lane // GLA_DK) == (hd % 2), ops.kd[:, pr], jnp.zeros((), BF16))
            s = jnp.where(causal, _dot_nt(ops.qs[hd // 2], k_h), 0.0).astype(BF16)
            u_all = _dot_tn(ops.kb[hd], v_h)
            heads.append((v_h, s, u_all))
            if hd % 2 == 1:
                yield

        staged = []
        for hd, (v_h, s, u_all) in enumerate(heads):
            st = state_ref[hd]
            entering = []
            for c in range(n_chunks):
                entering.append(st)
                last = c * GLA_CHUNK + GLA_CHUNK - 1
                d_col = ops.dec[hd * GLA_DK:(hd + 1) * GLA_DK, last:last + 1]
                st = st * d_col + u_all[c * GLA_DK:(c + 1) * GLA_DK, :]
            state_ref[hd] = st
            s_stack = jnp.concatenate(entering, axis=0).astype(BF16)
            staged.append((s, v_h, s_stack))
        yield

        outs = []
        for hd, (s, v_h, s_stack) in enumerate(staged):
            outs.append(_dot(s, v_h) + _dot(ops.qb[hd], s_stack))
            if hd % 2 == 1:
                yield

        g_gla = g_gla_ref[...]
        for hd, o_h in enumerate(outs):
            vs = slice(hd * GLA_DV, (hd + 1) * GLA_DV)
            o_ms = jnp.mean(o_h * o_h, axis=-1, keepdims=True)
            o_n = o_h * lax.rsqrt(o_ms + RMS_EPS) * g_gla[:, vs]
            mixed.append((o_n * ops.gg[:, vs].astype(F32)).astype(BF16))

    def swa(ops, mixed, first_tile):
        gq_lanes = SWA_GROUP * SWA_BLOCK
        kj = lax.broadcasted_iota(jnp.int32, (2 * SWA_BLOCK, gq_lanes), 0)
        ql = lax.broadcasted_iota(jnp.int32, (2 * SWA_BLOCK, gq_lanes), 1)
        dist = (ql % SWA_BLOCK) + SWA_BLOCK - kj
        valid = (dist >= 0) & (dist < WINDOW)
        head_of_lane = lax.broadcasted_iota(jnp.int32, (1, gq_lanes), 1) // SWA_BLOCK

        scores = []
        for u, (n, kv) in enumerate(units):
            bs = slice(n * SWA_BLOCK, (n + 2) * SWA_BLOCK)
            scores.append(_dot(ops.kk[kv, bs, :], ops.qt[u]))
            if u % 2 == 1:
                yield

        probs = []
        for (n, kv), s_raw in zip(units, scores):
            if n == 0 and first_tile is not None:
                ok = valid & ((kj >= SWA_BLOCK) | jnp.logical_not(first_tile))
            else:
                ok = valid
            s_t = s_raw + jnp.where(ok, 0.0, NEG_BIG)
            sink = jnp.zeros((1, gq_lanes), F32)
            for j in range(SWA_GROUP):
                sink = jnp.where(head_of_lane == j, sinks_ref[kv * SWA_GROUP + j] * LOG2E, sink)
            m = jnp.maximum(jnp.max(s_t, axis=0, keepdims=True), sink)
            e = jnp.exp2(s_t - m)
            den = jnp.sum(e, axis=0, keepdims=True) + jnp.exp2(sink - m)
            probs.append((e.astype(BF16), 1.0 / den))
            yield

        outs = []
        for (n, kv), (e, inv_den) in zip(units, probs):
            bs = slice(n * SWA_BLOCK, (n + 2) * SWA_BLOCK)
            outs.append(_dot_tn(ops.vv[kv, bs, :], e) * inv_den)
        yield

        for n in range(n_blocks):
            rs = slice(n * SWA_BLOCK, (n + 1) * SWA_BLOCK)
            o_groups = []
            for kv in range(SWA_KV_HEADS):
                o_t = outs[n * SWA_KV_HEADS + kv]
                o_stack = jnp.concatenate([o_t[:, j * SWA_BLOCK:(j + 1) * SWA_BLOCK] for j in range(SWA_GROUP)],
                                          axis=0)
                o_groups.append(jnp.transpose(o_stack))
            o_swa = jnp.concatenate(o_groups, axis=1)
            mixed.append((o_swa * ops.sg[rs, :].astype(F32)).astype(BF16))

    def finish(gla_heads, swa_blocks, rows):
        cat = jnp.concatenate([jnp.concatenate(gla_heads, axis=1), jnp.concatenate(swa_blocks, axis=0)], axis=1)
        y = _dot(cat, w_out_ref[...])
        yield
        x = x_ref[0, rows, :]
        xo = x + gate * y
        ms_o = jnp.mean(xo * xo, axis=-1, keepdims=True)
        o_ref[0, rows, :] = xo * lax.rsqrt(ms_o + RMS_EPS) * g_fin_ref[...]

    def interleave(primary, filler):
        live = list(primary)
        while live:
            for g in list(live):
                if next(g, StopIteration) is StopIteration:
                    live.remove(g)
                next(filler, None)
        for _ in filler:
            pass

    rows = [slice(k * ts, (k + 1) * ts) for k in range(STEP_TILES)]
    ops = (ops0, ops1)
    mixed = [([], []) for _ in range(STEP_TILES)]

    @pl.when(i == 0)
    def _():
        state_ref[...] = jnp.zeros_like(state_ref)

    @pl.when((i == 0) & (b_idx == 0))
    def _():
        for _ in prepare(x_ref[0, rows[0], :], pos_ref[pl.ds(b_idx, 1), rows[0]], norm_modulation(b_idx), ops0, None, True):
            pass

    for k in range(STEP_TILES):
        cur, nxt = ops[k % 2], ops[(k + 1) % 2]
        primary = [gla(cur, mixed[k][0]), swa(cur, mixed[k][1], (i == 0) if k == 0 else None)]
        if k > 0:
            primary.insert(0, finish(*mixed[k - 1], rows[k - 1]))
        if k + 1 < STEP_TILES:
            filler = prepare(x_ref[0, rows[k + 1], :], pos_ref[pl.ds(b_idx, 1), rows[k + 1]], norm_modulation(b_idx),
                             nxt, cur, False)
        else:
            filler = prepare(xn_ref[0], posn_ref[pl.ds(next_b, 1), :], norm_modulation(next_b), nxt, cur, last_step)
        interleave(primary, filler)
    for _ in finish(*mixed[STEP_TILES - 1], rows[STEP_TILES - 1]):
        pass


def _chunk_matrices(ts):
    r = np.arange(ts)
    same = (r[:, None] // GLA_CHUNK) == (r[None, :] // GLA_CHUNK)
    return jnp.asarray(same & (r[None, :] <= r[:, None]), BF16), jnp.asarray(same, BF16)


def _rope_inv_freq_column():
    inv = 1.0 / (ROPE_THETA ** (jnp.arange(0, SWA_HEAD_DIM, 2, dtype=F32) / SWA_HEAD_DIM))
    return inv.reshape(SWA_HEAD_DIM // 2, 1)


def kernel(x, c, positions, w_ada, b_ada, g_norm, w_in, w_decay, b_decay, g_gla_head, sinks, w_out, g_final):
    B, S, D = x.shape
    ts = SEQ_TILE
    assert D == D_MODEL and S % (STEP_TILES * ts) == 0 and ts % SWA_BLOCK == 0 and STEP_TILES % 2 == 0
    assert w_ada.shape[0] == 1, "one layer"
    n_tiles = S // ts

    mod, w_in_b, w_out_b = _setup(c.astype(F32), w_ada[0], b_ada[0][None, :], jnp.transpose(w_in[0]), w_out[0])
    w_dec = jnp.pad(w_decay[0].astype(BF16), ((0, LANES - GLA_GATE_RANK), (0, 0)))
    tri, blk = _chunk_matrices(ts)

    const2 = lambda b, i, s: (0, 0)
    full = lambda a: pl.BlockSpec(a.shape, const2)
    n_steps = n_tiles // STEP_TILES

    def next_row_tile(b, i):
        wraps = i == n_steps - 1
        at_end = wraps & (b == B - 1)
        row = jnp.where(wraps & jnp.logical_not(at_end), b + 1, b)
        tile = jnp.where(wraps, jnp.where(at_end, n_tiles - 1, 0), STEP_TILES * (i + 1))
        return row, tile

    operands = [
        x, x, positions, positions,
        mod,
        g_norm[0][None, :],
        _rope_inv_freq_column(),
        tri, blk,
        w_in_b,
        w_dec, b_decay[0][None, :], g_gla_head[0][None, :],
        w_out_b, g_final[None, :],
    ]
    in_specs = [
        pl.BlockSpec((1, STEP_TILES * ts, D), lambda b, i, s: (b, i, 0)),
        pl.BlockSpec((1, ts, D), lambda b, i, s: next_row_tile(b, i) + (0,)),
        pl.BlockSpec((B, STEP_TILES * ts), lambda b, i, s: (0, i)),
        pl.BlockSpec((B, ts), lambda b, i, s: (0, next_row_tile(b, i)[1])),
    ] + [full(a) for a in operands[4:]]

    operand_scratch = [pltpu.VMEM(shape, dtype) for _, shape, dtype in _OPERAND_BUFFERS]
    grid_spec = pltpu.PrefetchScalarGridSpec(
        num_scalar_prefetch=1,
        grid=(B, n_steps),
        in_specs=in_specs,
        out_specs=pl.BlockSpec((1, STEP_TILES * ts, D), lambda b, i, s: (b, i, 0)),
        scratch_shapes=[
            pltpu.VMEM((GLA_HEADS, GLA_DK, GLA_DV), F32),
        ] + operand_scratch + operand_scratch,
    )
    return pl.pallas_call(
        _layer_body,
        grid_spec=grid_spec,
        out_shape=jax.ShapeDtypeStruct((B, S, D), x.dtype),
        compiler_params=pltpu.CompilerParams(
            dimension_semantics=("arbitrary", "arbitrary"),
            vmem_limit_bytes=VMEM_LIMIT_BYTES,
        ),
        name="hymba_layer",
    )(sinks[0].astype(F32), *operands)
```

```python
import jax
import jax.numpy as jnp
import numpy as np
from jax import lax
from jax.experimental import pallas as pl
from jax.experimental.pallas import tpu as pltpu

D_MODEL = 1024
GLA_HEADS = 4
GLA_DK = 64
GLA_DV = 128
GLA_WIDTH = GLA_HEADS * GLA_DV
GLA_QK = GLA_HEADS * GLA_DK
GLA_GATE_RANK = 16
GLA_GATE_TAU = 16.0
GLA_CHUNK = 64
SWA_Q_HEADS = 8
SWA_KV_HEADS = 2
SWA_GROUP = SWA_Q_HEADS // SWA_KV_HEADS
SWA_HEAD_DIM = 64
SWA_WIDTH = SWA_Q_HEADS * SWA_HEAD_DIM
SWA_KV_WIDTH = SWA_KV_HEADS * SWA_HEAD_DIM
WINDOW = 128
SWA_BLOCK = 128
ROPE_THETA = 10000.0
RMS_EPS = 1e-6

LANES = 128
MXU_COLS = 256
PROJ_COLS = 2 * MXU_COLS
SEQ_TILE = 256
STEP_TILES = 2
LOG2E = 1.4426950408889634
NEG_BIG = -1e30
VMEM_LIMIT_BYTES = 40 * 1024 * 1024

W_GA = 0
W_SK = W_GA + LANES
W_SV = W_SK + SWA_KV_WIDTH
W_GQ = W_SV + SWA_KV_WIDTH
W_GK = W_GQ + GLA_QK
W_GV = W_GK + GLA_QK
W_SQ = W_GV + GLA_WIDTH
W_GZ = W_SQ + SWA_WIDTH
W_SZ = W_GZ + GLA_WIDTH
W_COLS = W_SZ + SWA_WIDTH

SWA_UNITS = (SEQ_TILE // SWA_BLOCK) * SWA_KV_HEADS

BF16 = jnp.bfloat16
F32 = jnp.float32


def _dot(a, b):
    return jnp.dot(a, b, preferred_element_type=F32)


def _dot_nt(a, b):
    return lax.dot_general(a, b, (((1,), (1,)), ((), ())), preferred_element_type=F32)


def _dot_tn(a, b):
    return lax.dot_general(a, b, (((0,), (0,)), ((), ())), preferred_element_type=F32)


def _silu(x):
    return x * (1.0 / (1.0 + jnp.exp2(x * (-LOG2E))))


def _log_sigmoid(z):
    return jnp.minimum(z, 0.0) - jnp.log(1.0 + jnp.exp2(jnp.abs(z) * (-LOG2E)))


_SRC_GA = 2 * GLA_QK + GLA_WIDTH
_SRC_GZ = _SRC_GA + GLA_GATE_RANK
_SRC_SQ = _SRC_GZ + GLA_WIDTH
_SRC_SK = _SRC_SQ + SWA_WIDTH
_SRC_SV = _SRC_SK + SWA_KV_WIDTH
_SRC_SZ = _SRC_SV + SWA_KV_WIDTH
_SRC_ROWS = _SRC_SZ + SWA_WIDTH
_REGROUP = (
    (W_SK, _SRC_SK, SWA_KV_WIDTH), (W_SV, _SRC_SV, SWA_KV_WIDTH),
    (W_GQ, 0, GLA_QK), (W_GK, GLA_QK, GLA_QK), (W_GV, 2 * GLA_QK, GLA_WIDTH),
    (W_SQ, _SRC_SQ, SWA_WIDTH), (W_GZ, _SRC_GZ, GLA_WIDTH), (W_SZ, _SRC_SZ, SWA_WIDTH),
)
W_COL_BLOCK = 256


def _setup_body(c_ref, w_ada_ref, b_ada_ref, wt_ref, w_out_ref, mod_ref, o_ref, w_out_o_ref):
    mod_ref[...] = _dot(_silu(c_ref[...]).astype(BF16), w_ada_ref[...].astype(BF16)) + b_ada_ref[...]
    w_out_o_ref[...] = w_out_ref[...].astype(BF16)
    row = lax.broadcasted_iota(jnp.int32, (LANES, 1), 0)
    ga = jnp.where(row < GLA_GATE_RANK, wt_ref[_SRC_GA:_SRC_GA + LANES, :], 0.0)
    o_ref[:, W_GA:W_GA + LANES] = jnp.transpose(ga).astype(BF16)
    for dst, src, width in _REGROUP:
        o_ref[:, dst:dst + width] = jnp.transpose(wt_ref[src:src + width, :]).astype(BF16)


def _setup(c, w_ada, b_ada, wt, w_out):
    n, d = wt.shape
    k_out, n_out = w_out.shape
    n_mod = w_ada.shape[1]
    steps = d // W_COL_BLOCK
    assert n == _SRC_ROWS and d % W_COL_BLOCK == 0 and k_out % steps == 0 and n_mod % (steps * LANES) == 0
    out_rows = k_out // steps
    mod_cols = n_mod // steps
    rows = c.shape[0]
    return pl.pallas_call(
        _setup_body,
        grid=(steps,),
        in_specs=[pl.BlockSpec((rows, w_ada.shape[0]), lambda r: (0, 0)),
                  pl.BlockSpec((w_ada.shape[0], mod_cols), lambda r: (0, r)),
                  pl.BlockSpec((1, mod_cols), lambda r: (0, r)),
                  pl.BlockSpec((n, W_COL_BLOCK), lambda r: (0, r)),
                  pl.BlockSpec((out_rows, n_out), lambda r: (r, 0))],
        out_specs=[pl.BlockSpec((rows, mod_cols), lambda r: (0, r)),
                   pl.BlockSpec((W_COL_BLOCK, W_COLS), lambda r: (r, 0)),
                   pl.BlockSpec((out_rows, n_out), lambda r: (r, 0))],
        out_shape=[jax.ShapeDtypeStruct((rows, n_mod), F32), jax.ShapeDtypeStruct((d, W_COLS), BF16),
                   jax.ShapeDtypeStruct((k_out, n_out), BF16)],
        name="setup",
    )(c, w_ada, b_ada, wt, w_out)


_OPERAND_BUFFERS = (
    ("qs", (GLA_HEADS // 2, SEQ_TILE, LANES), BF16),
    ("qb", (GLA_HEADS, SEQ_TILE, SEQ_TILE), BF16),
    ("kb", (GLA_HEADS, SEQ_TILE, SEQ_TILE), BF16),
    ("kd", (SEQ_TILE, GLA_QK), BF16),
    ("gv", (SEQ_TILE, GLA_WIDTH), BF16),
    ("dec", (GLA_QK, SEQ_TILE), F32),
    ("gg", (SEQ_TILE, GLA_WIDTH), BF16),
    ("sg", (SEQ_TILE, SWA_WIDTH), BF16),
    ("qt", (SWA_UNITS, SWA_HEAD_DIM, SWA_GROUP * SWA_BLOCK), BF16),
    ("kk", (SWA_KV_HEADS, SEQ_TILE + SWA_BLOCK, SWA_HEAD_DIM), BF16),
    ("vv", (SWA_KV_HEADS, SEQ_TILE + SWA_BLOCK, SWA_HEAD_DIM), BF16),
)


class _Operands:
    def __init__(self, refs):
        for (name, _, _), ref in zip(_OPERAND_BUFFERS, refs):
            setattr(self, name, ref)


def _layer_body(sinks_ref, x_ref, xn_ref, pos_ref, posn_ref, mod_ref, modn_ref, gnorm_ref, invf_ref, tri_ref, blk_ref,
                w_in_ref, w_dec_ref, b_dec_ref, g_gla_ref, w_out_ref, g_fin_ref,
                o_ref, state_ref, *operand_refs):
    ts = SEQ_TILE
    b_idx = pl.program_id(0)
    i = pl.program_id(1)
    last_step = i == pl.num_programs(1) - 1
    nbuf = len(_OPERAND_BUFFERS)
    ops0 = _Operands(operand_refs[:nbuf])
    ops1 = _Operands(operand_refs[nbuf:])

    gate = mod_ref[0, :, 2 * D_MODEL:3 * D_MODEL]

    def norm_modulation(m_ref):
        return (gnorm_ref[...] * (1.0 + m_ref[0, :, D_MODEL:2 * D_MODEL]), m_ref[0, :, 0:D_MODEL])

    n_chunks = ts // GLA_CHUNK
    n_blocks = ts // SWA_BLOCK
    units = [(n, kv) for n in range(n_blocks) for kv in range(SWA_KV_HEADS)]

    def proj(hb, c0, width):
        return _dot(hb, w_in_ref[:, c0:c0 + width])

    def prepare(x, pos, modulation, ops, prev_ops, starts_sequence):
        norm_gain, shift = modulation
        ms = jnp.mean(x * x, axis=-1, keepdims=True)
        hb = ((x * lax.rsqrt(ms + RMS_EPS)) * norm_gain + shift).astype(BF16)
        yield

        d0 = proj(hb, W_GA, LANES + 2 * SWA_KV_WIDTH)
        yield

        ang_t = invf_ref[...] * pos.astype(F32)
        reps = LANES // (SWA_HEAD_DIM // 2)
        cos_f = jnp.transpose(jnp.concatenate([jnp.cos(ang_t)] * reps, axis=0))
        sin_f = jnp.transpose(jnp.concatenate([jnp.sin(ang_t)] * reps, axis=0))
        lane = lax.broadcasted_iota(jnp.int32, (1, LANES), 1)
        first_half = (lane % SWA_HEAD_DIM) < (SWA_HEAD_DIM // 2)
        sin_s = jnp.where(first_half, -sin_f, sin_f)

        def rope(tv):
            rot = jnp.where(first_half,
                            pltpu.roll(tv, LANES - SWA_HEAD_DIM // 2, axis=1),
                            pltpu.roll(tv, SWA_HEAD_DIM // 2, axis=1))
            return tv * cos_f + rot * sin_s
        yield

        z = _dot(d0[:, 0:LANES].astype(BF16), w_dec_ref[...]) + b_dec_ref[...]
        k_rot = rope(d0[:, LANES:LANES + SWA_KV_WIDTH]).astype(BF16)
        for kv in range(SWA_KV_HEADS):
            hs = slice(kv * SWA_HEAD_DIM, (kv + 1) * SWA_HEAD_DIM)
            v0 = LANES + SWA_KV_WIDTH + kv * SWA_HEAD_DIM
            ops.kk[kv, SWA_BLOCK:SWA_BLOCK + ts, :] = k_rot[:, hs]
            ops.vv[kv, SWA_BLOCK:SWA_BLOCK + ts, :] = d0[:, v0:v0 + SWA_HEAD_DIM].astype(BF16)
            zeros = jnp.zeros((SWA_BLOCK, SWA_HEAD_DIM), BF16)
            if starts_sequence is True:
                k_carry, v_carry = zeros, zeros
            else:
                k_carry = prev_ops.kk[kv, ts:ts + SWA_BLOCK, :]
                v_carry = prev_ops.vv[kv, ts:ts + SWA_BLOCK, :]
                if starts_sequence is not False:
                    k_carry = jnp.where(starts_sequence, zeros, k_carry)
                    v_carry = jnp.where(starts_sequence, zeros, v_carry)
            ops.kk[kv, 0:SWA_BLOCK, :] = k_carry
            ops.vv[kv, 0:SWA_BLOCK, :] = v_carry
        yield

        log_a = _log_sigmoid(z) * (1.0 / GLA_GATE_TAU)
        la_hi = log_a.astype(BF16)
        la_lo = (log_a - la_hi.astype(F32)).astype(BF16)
        b2 = _dot(tri_ref[...], jnp.concatenate([la_hi, la_lo], axis=1))
        gqk = proj(hb, W_GQ, 2 * GLA_QK)
        gq = gqk[:, :GLA_QK]
        gk = gqk[:, GLA_QK:]
        yield
        b = b2[:, :GLA_QK] + b2[:, GLA_QK:]
        e_b = jnp.exp(b)
        e_last = jnp.concatenate(
            [jnp.broadcast_to(e_b[(c + 1) * GLA_CHUNK - 1:(c + 1) * GLA_CHUNK, :], (GLA_CHUNK, GLA_QK))
             for c in range(n_chunks)], axis=0)
        ops.dec[...] = jnp.transpose(e_last)
        q_d = gq * (GLA_DK ** -0.5) * e_b
        yield
        k_d = gk * (1.0 / e_b)
        ops.kd[...] = k_d.astype(BF16)
        k_tail = k_d * e_last

        blk = blk_ref[...]
        low_half = lax.broadcasted_iota(jnp.int32, (1, LANES), 1) < GLA_DK
        tile_reps = ts // LANES
        for pair in range(GLA_HEADS // 2):
            pr = slice(pair * LANES, (pair + 1) * LANES)
            q_pair, k_pair = q_d[:, pr], k_tail[:, pr]
            q_swap = pltpu.roll(q_pair, GLA_DK, axis=1)
            k_swap = pltpu.roll(k_pair, GLA_DK, axis=1)
            ops.qs[pair] = q_pair.astype(BF16)
            for j in range(2):
                a, b_ = (q_pair, q_swap) if j == 0 else (q_swap, q_pair)
                c, d_ = (k_pair, k_swap) if j == 0 else (k_swap, k_pair)
                q_dup = jnp.where(low_half, a, b_).astype(BF16)
                k_dup = jnp.where(low_half, c, d_).astype(BF16)
                ops.qb[2 * pair + j] = jnp.concatenate([q_dup] * tile_reps, axis=1) * blk
                ops.kb[2 * pair + j] = jnp.concatenate([k_dup] * tile_reps, axis=1) * blk
            yield

        for c0 in range(0, GLA_WIDTH, PROJ_COLS):
            ops.gv[:, c0:c0 + PROJ_COLS] = proj(hb, W_GV + c0, PROJ_COLS).astype(BF16)
            yield

        sq_all = proj(hb, W_SQ, SWA_WIDTH)
        for kv in range(SWA_KV_HEADS):
            sq = sq_all[:, kv * MXU_COLS:(kv + 1) * MXU_COLS]
            q_rot = jnp.concatenate([rope(sq[:, j * LANES:(j + 1) * LANES]) for j in range(MXU_COLS // LANES)],
                                    axis=1) * (SWA_HEAD_DIM ** -0.5 * LOG2E)
            for n in range(n_blocks):
                q_t = jnp.transpose(q_rot[n * SWA_BLOCK:(n + 1) * SWA_BLOCK, :])
                ops.qt[n * SWA_KV_HEADS + kv] = jnp.concatenate(
                    [q_t[j * SWA_HEAD_DIM:(j + 1) * SWA_HEAD_DIM, :] for j in range(SWA_GROUP)],
                    axis=1).astype(BF16)
            yield

        for c0 in range(0, GLA_WIDTH, PROJ_COLS):
            ops.gg[:, c0:c0 + PROJ_COLS] = _silu(proj(hb, W_GZ + c0, PROJ_COLS)).astype(BF16)
            yield
        for c0 in range(0, SWA_WIDTH, PROJ_COLS):
            ops.sg[:, c0:c0 + PROJ_COLS] = _silu(proj(hb, W_SZ + c0, PROJ_COLS)).astype(BF16)
            yield

    def gla(ops, mixed):
        row_i = lax.broadcasted_iota(jnp.int32, (ts, ts), 0)
        col_i = lax.broadcasted_iota(jnp.int32, (ts, ts), 1)
        causal = ((row_i // GLA_CHUNK) == (col_i // GLA_CHUNK)) & (col_i <= row_i)
        pair_lane = lax.broadcasted_iota(jnp.int32, (1, LANES), 1)
        heads = []
        for hd in range(GLA_HEADS):
            pr = slice((hd // 2) * LANES, (hd // 2 + 1) * LANES)
            v_h = ops.gv[:, hd * GLA_DV:(hd + 1) * GLA_DV]
            k_h = jnp.where((pair_lane // GLA_DK) == (hd % 2), ops.kd[:, pr], jnp.zeros((), BF16))
            s = jnp.where(causal, _dot_nt(ops.qs[hd // 2], k_h), 0.0).astype(BF16)
            u_all = _dot_tn(ops.kb[hd], v_h)
            heads.append((v_h, s, u_all))
            if hd % 2 == 1:
                yield

        staged = []
        for hd, (v_h, s, u_all) in enumerate(heads):
            st = state_ref[hd]
            entering = []
            for c in range(n_chunks):
                entering.append(st)
                last = c * GLA_CHUNK + GLA_CHUNK - 1
                d_col = ops.dec[hd * GLA_DK:(hd + 1) * GLA_DK, last:last + 1]
                st = st * d_col + u_all[c * GLA_DK:(c + 1) * GLA_DK, :]
            state_ref[hd] = st
            s_stack = jnp.concatenate(entering, axis=0).astype(BF16)
            staged.append((s, v_h, s_stack))
        yield

        outs = []
        for hd, (s, v_h, s_stack) in enumerate(staged):
            outs.append(_dot(s, v_h) + _dot(ops.qb[hd], s_stack))
            if hd % 2 == 1:
                yield

        g_gla = g_gla_ref[...]
        for hd, o_h in enumerate(outs):
            vs = slice(hd * GLA_DV, (hd + 1) * GLA_DV)
            o_ms = jnp.mean(o_h * o_h, axis=-1, keepdims=True)
            o_n = o_h * lax.rsqrt(o_ms + RMS_EPS) * g_gla[:, vs]
            mixed.append((o_n * ops.gg[:, vs].astype(F32)).astype(BF16))

    def swa(ops, mixed, first_tile):
        gq_lanes = SWA_GROUP * SWA_BLOCK
        kj = lax.broadcasted_iota(jnp.int32, (2 * SWA_BLOCK, gq_lanes), 0)
        ql = lax.broadcasted_iota(jnp.int32, (2 * SWA_BLOCK, gq_lanes), 1)
        dist = (ql % SWA_BLOCK) + SWA_BLOCK - kj
        valid = (dist >= 0) & (dist < WINDOW)
        head_of_lane = lax.broadcasted_iota(jnp.int32, (1, gq_lanes), 1) // SWA_BLOCK

        scores = []
        for u, (n, kv) in enumerate(units):
            bs = slice(n * SWA_BLOCK, (n + 2) * SWA_BLOCK)
            scores.append(_dot(ops.kk[kv, bs, :], ops.qt[u]))
            if u % 2 == 1:
                yield

        probs = []
        for (n, kv), s_raw in zip(units, scores):
            if n == 0 and first_tile is not None:
                ok = valid & ((kj >= SWA_BLOCK) | jnp.logical_not(first_tile))
            else:
                ok = valid
            s_t = s_raw + jnp.where(ok, 0.0, NEG_BIG)
            sink = jnp.zeros((1, gq_lanes), F32)
            for j in range(SWA_GROUP):
                sink = jnp.where(head_of_lane == j, sinks_ref[kv * SWA_GROUP + j] * LOG2E, sink)
            m = jnp.maximum(jnp.max(s_t, axis=0, keepdims=True), sink)
            e = jnp.exp2(s_t - m)
            den = jnp.sum(e, axis=0, keepdims=True) + jnp.exp2(sink - m)
            probs.append((e.astype(BF16), 1.0 / den))
            yield

        outs = []
        for (n, kv), (e, inv_den) in zip(units, probs):
            bs = slice(n * SWA_BLOCK, (n + 2) * SWA_BLOCK)
            outs.append(_dot_tn(ops.vv[kv, bs, :], e) * inv_den)
        yield

        for n in range(n_blocks):
            rs = slice(n * SWA_BLOCK, (n + 1) * SWA_BLOCK)
            o_groups = []
            for kv in range(SWA_KV_HEADS):
                o_t = outs[n * SWA_KV_HEADS + kv]
                o_stack = jnp.concatenate([o_t[:, j * SWA_BLOCK:(j + 1) * SWA_BLOCK] for j in range(SWA_GROUP)],
                                          axis=0)
                o_groups.append(jnp.transpose(o_stack))
            o_swa = jnp.concatenate(o_groups, axis=1)
            mixed.append((o_swa * ops.sg[rs, :].astype(F32)).astype(BF16))

    def finish(gla_heads, swa_blocks, rows):
        cat = jnp.concatenate([jnp.concatenate(gla_heads, axis=1), jnp.concatenate(swa_blocks, axis=0)], axis=1)
        y = _dot(cat, w_out_ref[...])
        yield
        x = x_ref[0, rows, :]
        xo = x + gate * y
        ms_o = jnp.mean(xo * xo, axis=-1, keepdims=True)
        o_ref[0, rows, :] = xo * lax.rsqrt(ms_o + RMS_EPS) * g_fin_ref[...]

    def interleave(primary, filler):
        live = list(primary)
        while live:
            for g in list(live):
                if next(g, StopIteration) is StopIteration:
                    live.remove(g)
                next(filler, None)
        for _ in filler:
            pass

    rows = [slice(k * ts, (k + 1) * ts) for k in range(STEP_TILES)]
    ops = (ops0, ops1)
    mixed = [([], []) for _ in range(STEP_TILES)]

    @pl.when(i == 0)
    def _():
        state_ref[...] = jnp.zeros_like(state_ref)

    @pl.when((i == 0) & (b_idx == 0))
    def _():
        for _ in prepare(x_ref[0, rows[0], :], pos_ref[0, :, rows[0]], norm_modulation(mod_ref), ops0, None, True):
            pass

    for k in range(STEP_TILES):
        cur, nxt = ops[k % 2], ops[(k + 1) % 2]
        primary = [gla(cur, mixed[k][0]), swa(cur, mixed[k][1], (i == 0) if k == 0 else None)]
        if k > 0:
            primary.insert(0, finish(*mixed[k - 1], rows[k - 1]))
        if k + 1 < STEP_TILES:
            filler = prepare(x_ref[0, rows[k + 1], :], pos_ref[0, :, rows[k + 1]], norm_modulation(mod_ref),
                             nxt, cur, False)
        else:
            filler = prepare(xn_ref[0], posn_ref[0], norm_modulation(modn_ref), nxt, cur, last_step)
        interleave(primary, filler)
    for _ in finish(*mixed[STEP_TILES - 1], rows[STEP_TILES - 1]):
        pass


def _chunk_matrices(ts):
    r = np.arange(ts)
    same = (r[:, None] // GLA_CHUNK) == (r[None, :] // GLA_CHUNK)
    return jnp.asarray(same & (r[None, :] <= r[:, None]), BF16), jnp.asarray(same, BF16)


def _rope_inv_freq_column():
    inv = 1.0 / (ROPE_THETA ** (jnp.arange(0, SWA_HEAD_DIM, 2, dtype=F32) / SWA_HEAD_DIM))
    return inv.reshape(SWA_HEAD_DIM // 2, 1)


def kernel(x, c, positions, w_ada, b_ada, g_norm, w_in, w_decay, b_decay, g_gla_head, sinks, w_out, g_final):
    B, S, D = x.shape
    ts = SEQ_TILE
    assert D == D_MODEL and S % (STEP_TILES * ts) == 0 and ts % SWA_BLOCK == 0 and STEP_TILES % 2 == 0
    assert w_ada.shape[0] == 1, "one layer"
    n_tiles = S // ts

    mod, w_in_b, w_out_b = _setup(c.astype(F32), w_ada[0], b_ada[0][None, :], jnp.transpose(w_in[0]), w_out[0])
    mod = mod.reshape(B, 1, 3 * D)
    w_dec = jnp.pad(w_decay[0].astype(BF16), ((0, LANES - GLA_GATE_RANK), (0, 0)))
    tri, blk = _chunk_matrices(ts)
    pos3 = positions.reshape(B, 1, S)

    const2 = lambda b, i, s: (0, 0)
    full = lambda a: pl.BlockSpec(a.shape, const2, pipeline_mode=pl.Buffered(1))
    n_steps = n_tiles // STEP_TILES

    def next_row_tile(b, i):
        wraps = i == n_steps - 1
        at_end = wraps & (b == B - 1)
        row = jnp.where(wraps & jnp.logical_not(at_end), b + 1, b)
        tile = jnp.where(wraps, jnp.where(at_end, n_tiles - 1, 0), STEP_TILES * (i + 1))
        return row, tile

    operands = [
        x, x, pos3, pos3,
        mod, mod,
        g_norm[0][None, :],
        _rope_inv_freq_column(),
        tri, blk,
        w_in_b,
        w_dec, b_decay[0][None, :], g_gla_head[0][None, :],
        w_out_b, g_final[None, :],
    ]
    in_specs = [
        pl.BlockSpec((1, STEP_TILES * ts, D), lambda b, i, s: (b, i, 0)),
        pl.BlockSpec((1, ts, D), lambda b, i, s: next_row_tile(b, i) + (0,)),
        pl.BlockSpec((1, 1, STEP_TILES * ts), lambda b, i, s: (b, 0, i)),
        pl.BlockSpec((1, 1, ts), lambda b, i, s: (next_row_tile(b, i)[0], 0, next_row_tile(b, i)[1])),
        pl.BlockSpec((1, 1, 3 * D), lambda b, i, s: (b, 0, 0)),
        pl.BlockSpec((1, 1, 3 * D), lambda b, i, s: (next_row_tile(b, i)[0], 0, 0)),
    ] + [full(a) for a in operands[6:]]

    operand_scratch = [pltpu.VMEM(shape, dtype) for _, shape, dtype in _OPERAND_BUFFERS]
    grid_spec = pltpu.PrefetchScalarGridSpec(
        num_scalar_prefetch=1,
        grid=(B, n_steps),
        in_specs=in_specs,
        out_specs=pl.BlockSpec((1, STEP_TILES * ts, D), lambda b, i, s: (b, i, 0)),
        scratch_shapes=[
            pltpu.VMEM((GLA_HEADS, GLA_DK, GLA_DV), F32),
        ] + operand_scratch + operand_scratch,
    )
    return pl.pallas_call(
        _layer_body,
        grid_spec=grid_spec,
        out_shape=jax.ShapeDtypeStruct((B, S, D), x.dtype),
        compiler_params=pltpu.CompilerParams(
            dimension_semantics=("arbitrary", "arbitrary"),
            vmem_limit_bytes=VMEM_LIMIT_BYTES,
        ),
        name="hymba_layer",
    )(sinks[0].astype(F32), *operands)
```

```python
import jax
import jax.numpy as jnp
import numpy as np
from jax import lax
from jax.experimental import pallas as pl
from jax.experimental.pallas import tpu as pltpu

D_MODEL = 1024
GLA_HEADS = 4
GLA_DK = 64
GLA_DV = 128
GLA_WIDTH = GLA_HEADS * GLA_DV
GLA_QK = GLA_HEADS * GLA_DK
GLA_GATE_RANK = 16
GLA_GATE_TAU = 16.0
GLA_CHUNK = 64
SWA_Q_HEADS = 8
SWA_KV_HEADS = 2
SWA_GROUP = SWA_Q_HEADS // SWA_KV_HEADS
SWA_HEAD_DIM = 64
SWA_WIDTH = SWA_Q_HEADS * SWA_HEAD_DIM
SWA_KV_WIDTH = SWA_KV_HEADS * SWA_HEAD_DIM
WINDOW = 128
SWA_BLOCK = 128
ROPE_THETA = 10000.0
RMS_EPS = 1e-6

LANES = 128
MXU_COLS = 256
PROJ_COLS = 2 * MXU_COLS
SEQ_TILE = 256
STEP_TILES = 2
LOG2E = 1.4426950408889634
NEG_BIG = -1e30
VMEM_LIMIT_BYTES = 32 * 1024 * 1024

W_GA = 0
W_SK = W_GA + LANES
W_SV = W_SK + SWA_KV_WIDTH
W_GQ = W_SV + SWA_KV_WIDTH
W_GK = W_GQ + GLA_QK
W_GV = W_GK + GLA_QK
W_SQ = W_GV + GLA_WIDTH
W_GZ = W_SQ + SWA_WIDTH
W_SZ = W_GZ + GLA_WIDTH
W_COLS = W_SZ + SWA_WIDTH

SWA_UNITS = (SEQ_TILE // SWA_BLOCK) * SWA_KV_HEADS

BF16 = jnp.bfloat16
F32 = jnp.float32


def _dot(a, b):
    return jnp.dot(a, b, preferred_element_type=F32)


def _dot_nt(a, b):
    return lax.dot_general(a, b, (((1,), (1,)), ((), ())), preferred_element_type=F32)


def _dot_tn(a, b):
    return lax.dot_general(a, b, (((0,), (0,)), ((), ())), preferred_element_type=F32)


def _silu(x):
    return x * (1.0 / (1.0 + jnp.exp2(x * (-LOG2E))))


def _log_sigmoid(z):
    return jnp.minimum(z, 0.0) - jnp.log(1.0 + jnp.exp2(jnp.abs(z) * (-LOG2E)))


_SRC_GA = 2 * GLA_QK + GLA_WIDTH
_SRC_GZ = _SRC_GA + GLA_GATE_RANK
_SRC_SQ = _SRC_GZ + GLA_WIDTH
_SRC_SK = _SRC_SQ + SWA_WIDTH
_SRC_SV = _SRC_SK + SWA_KV_WIDTH
_SRC_SZ = _SRC_SV + SWA_KV_WIDTH
_SRC_ROWS = _SRC_SZ + SWA_WIDTH
_REGROUP = (
    (W_SK, _SRC_SK, SWA_KV_WIDTH), (W_SV, _SRC_SV, SWA_KV_WIDTH),
    (W_GQ, 0, GLA_QK), (W_GK, GLA_QK, GLA_QK), (W_GV, 2 * GLA_QK, GLA_WIDTH),
    (W_SQ, _SRC_SQ, SWA_WIDTH), (W_GZ, _SRC_GZ, GLA_WIDTH), (W_SZ, _SRC_SZ, SWA_WIDTH),
)
W_COL_BLOCK = 256


def _setup_body(c_ref, w_ada_ref, b_ada_ref, wt_ref, w_out_ref, mod_ref, o_ref, w_out_o_ref):
    mod_ref[...] = _dot(_silu(c_ref[...]).astype(BF16), w_ada_ref[...].astype(BF16)) + b_ada_ref[...]
    w_out_o_ref[...] = w_out_ref[...].astype(BF16)
    row = lax.broadcasted_iota(jnp.int32, (LANES, 1), 0)
    ga = jnp.where(row < GLA_GATE_RANK, wt_ref[_SRC_GA:_SRC_GA + LANES, :], 0.0)
    o_ref[:, W_GA:W_GA + LANES] = jnp.transpose(ga).astype(BF16)
    for dst, src, width in _REGROUP:
        o_ref[:, dst:dst + width] = jnp.transpose(wt_ref[src:src + width, :]).astype(BF16)


def _setup(c, w_ada, b_ada, wt, w_out):
    n, d = wt.shape
    k_out, n_out = w_out.shape
    n_mod = w_ada.shape[1]
    steps = d // W_COL_BLOCK
    assert n == _SRC_ROWS and d % W_COL_BLOCK == 0 and k_out % steps == 0 and n_mod % (steps * LANES) == 0
    out_rows = k_out // steps
    mod_cols = n_mod // steps
    rows = c.shape[0]
    return pl.pallas_call(
        _setup_body,
        grid=(steps,),
        in_specs=[pl.BlockSpec((rows, w_ada.shape[0]), lambda r: (0, 0)),
                  pl.BlockSpec((w_ada.shape[0], mod_cols), lambda r: (0, r)),
                  pl.BlockSpec((1, mod_cols), lambda r: (0, r)),
                  pl.BlockSpec((n, W_COL_BLOCK), lambda r: (0, r)),
                  pl.BlockSpec((out_rows, n_out), lambda r: (r, 0))],
        out_specs=[pl.BlockSpec((rows, mod_cols), lambda r: (0, r)),
                   pl.BlockSpec((W_COL_BLOCK, W_COLS), lambda r: (r, 0)),
                   pl.BlockSpec((out_rows, n_out), lambda r: (r, 0))],
        out_shape=[jax.ShapeDtypeStruct((rows, n_mod), F32), jax.ShapeDtypeStruct((d, W_COLS), BF16),
                   jax.ShapeDtypeStruct((k_out, n_out), BF16)],
        name="setup",
    )(c, w_ada, b_ada, wt, w_out)


_OPERAND_BUFFERS = (
    ("qs", (GLA_HEADS // 2, SEQ_TILE, LANES), BF16),
    ("qb", (GLA_HEADS, SEQ_TILE, SEQ_TILE), BF16),
    ("kb", (GLA_HEADS, SEQ_TILE, SEQ_TILE), BF16),
    ("kd", (SEQ_TILE, GLA_QK), BF16),
    ("gv", (SEQ_TILE, GLA_WIDTH), BF16),
    ("dec", (GLA_QK, SEQ_TILE), F32),
    ("gg", (SEQ_TILE, GLA_WIDTH), BF16),
    ("sg", (SEQ_TILE, SWA_WIDTH), BF16),
    ("qt", (SWA_UNITS, SWA_HEAD_DIM, SWA_GROUP * SWA_BLOCK), BF16),
    ("kk", (SWA_KV_HEADS, SEQ_TILE + SWA_BLOCK, SWA_HEAD_DIM), BF16),
    ("vv", (SWA_KV_HEADS, SEQ_TILE + SWA_BLOCK, SWA_HEAD_DIM), BF16),
)


class _Operands:
    def __init__(self, refs):
        for (name, _, _), ref in zip(_OPERAND_BUFFERS, refs):
            setattr(self, name, ref)


def _layer_body(sinks_ref, x_ref, xn_ref, pos_ref, posn_ref, mod_ref, modn_ref, gnorm_ref, invf_ref, tri_ref, blk_ref,
                w_in_ref, w_dec_ref, b_dec_ref, g_gla_ref, w_out_ref, g_fin_ref,
                o_ref, state_ref, *operand_refs):
    ts = SEQ_TILE
    b_idx = pl.program_id(0)
    i = pl.program_id(1)
    last_step = i == pl.num_programs(1) - 1
    nbuf = len(_OPERAND_BUFFERS)
    ops0 = _Operands(operand_refs[:nbuf])
    ops1 = _Operands(operand_refs[nbuf:])

    gate = mod_ref[0, :, 2 * D_MODEL:3 * D_MODEL]

    def norm_modulation(m_ref):
        return (gnorm_ref[...] * (1.0 + m_ref[0, :, D_MODEL:2 * D_MODEL]), m_ref[0, :, 0:D_MODEL])

    n_chunks = ts // GLA_CHUNK
    n_blocks = ts // SWA_BLOCK
    units = [(n, kv) for n in range(n_blocks) for kv in range(SWA_KV_HEADS)]

    def proj(hb, c0, width):
        return _dot(hb, w_in_ref[:, c0:c0 + width])

    def prepare(x, pos, modulation, ops, prev_ops, starts_sequence):
        norm_gain, shift = modulation
        ms = jnp.mean(x * x, axis=-1, keepdims=True)
        hb = ((x * lax.rsqrt(ms + RMS_EPS)) * norm_gain + shift).astype(BF16)
        yield

        d0 = proj(hb, W_GA, LANES + 2 * SWA_KV_WIDTH)
        yield

        ang_t = invf_ref[...] * pos.astype(F32)
        reps = LANES // (SWA_HEAD_DIM // 2)
        cos_f = jnp.transpose(jnp.concatenate([jnp.cos(ang_t)] * reps, axis=0))
        sin_f = jnp.transpose(jnp.concatenate([jnp.sin(ang_t)] * reps, axis=0))
        lane = lax.broadcasted_iota(jnp.int32, (1, LANES), 1)
        first_half = (lane % SWA_HEAD_DIM) < (SWA_HEAD_DIM // 2)
        sin_s = jnp.where(first_half, -sin_f, sin_f)

        def rope(tv):
            rot = jnp.where(first_half,
                            pltpu.roll(tv, LANES - SWA_HEAD_DIM // 2, axis=1),
                            pltpu.roll(tv, SWA_HEAD_DIM // 2, axis=1))
            return tv * cos_f + rot * sin_s
        yield

        z = _dot(d0[:, 0:LANES].astype(BF16), w_dec_ref[...]) + b_dec_ref[...]
        k_rot = rope(d0[:, LANES:LANES + SWA_KV_WIDTH]).astype(BF16)
        for kv in range(SWA_KV_HEADS):
            hs = slice(kv * SWA_HEAD_DIM, (kv + 1) * SWA_HEAD_DIM)
            v0 = LANES + SWA_KV_WIDTH + kv * SWA_HEAD_DIM
            ops.kk[kv, SWA_BLOCK:SWA_BLOCK + ts, :] = k_rot[:, hs]
            ops.vv[kv, SWA_BLOCK:SWA_BLOCK + ts, :] = d0[:, v0:v0 + SWA_HEAD_DIM].astype(BF16)
            zeros = jnp.zeros((SWA_BLOCK, SWA_HEAD_DIM), BF16)
            if starts_sequence is True:
                k_carry, v_carry = zeros, zeros
            else:
                k_carry = prev_ops.kk[kv, ts:ts + SWA_BLOCK, :]
                v_carry = prev_ops.vv[kv, ts:ts + SWA_BLOCK, :]
                if starts_sequence is not False:
                    k_carry = jnp.where(starts_sequence, zeros, k_carry)
                    v_carry = jnp.where(starts_sequence, zeros, v_carry)
            ops.kk[kv, 0:SWA_BLOCK, :] = k_carry
            ops.vv[kv, 0:SWA_BLOCK, :] = v_carry
        yield

        log_a = _log_sigmoid(z) * (1.0 / GLA_GATE_TAU)
        la_hi = log_a.astype(BF16)
        la_lo = (log_a - la_hi.astype(F32)).astype(BF16)
        b2 = _dot(tri_ref[...], jnp.concatenate([la_hi, la_lo], axis=1))
        gqk = proj(hb, W_GQ, 2 * GLA_QK)
        gq = gqk[:, :GLA_QK]
        gk = gqk[:, GLA_QK:]
        yield
        b = b2[:, :GLA_QK] + b2[:, GLA_QK:]
        e_b = jnp.exp(b)
        e_last = jnp.concatenate(
            [jnp.broadcast_to(e_b[(c + 1) * GLA_CHUNK - 1:(c + 1) * GLA_CHUNK, :], (GLA_CHUNK, GLA_QK))
             for c in range(n_chunks)], axis=0)
        ops.dec[...] = jnp.transpose(e_last)
        q_d = gq * (GLA_DK ** -0.5) * e_b
        yield
        k_d = gk * (1.0 / e_b)
        ops.kd[...] = k_d.astype(BF16)
        k_tail = k_d * e_last

        blk = blk_ref[...]
        low_half = lax.broadcasted_iota(jnp.int32, (1, LANES), 1) < GLA_DK
        tile_reps = ts // LANES
        for pair in range(GLA_HEADS // 2):
            pr = slice(pair * LANES, (pair + 1) * LANES)
            q_pair, k_pair = q_d[:, pr], k_tail[:, pr]
            q_swap = pltpu.roll(q_pair, GLA_DK, axis=1)
            k_swap = pltpu.roll(k_pair, GLA_DK, axis=1)
            ops.qs[pair] = q_pair.astype(BF16)
            for j in range(2):
                a, b_ = (q_pair, q_swap) if j == 0 else (q_swap, q_pair)
                c, d_ = (k_pair, k_swap) if j == 0 else (k_swap, k_pair)
                q_dup = jnp.where(low_half, a, b_).astype(BF16)
                k_dup = jnp.where(low_half, c, d_).astype(BF16)
                ops.qb[2 * pair + j] = jnp.concatenate([q_dup] * tile_reps, axis=1) * blk
                ops.kb[2 * pair + j] = jnp.concatenate([k_dup] * tile_reps, axis=1) * blk
            yield

        for c0 in range(0, GLA_WIDTH, PROJ_COLS):
            ops.gv[:, c0:c0 + PROJ_COLS] = proj(hb, W_GV + c0, PROJ_COLS).astype(BF16)
            yield

        sq_all = proj(hb, W_SQ, SWA_WIDTH)
        for kv in range(SWA_KV_HEADS):
            sq = sq_all[:, kv * MXU_COLS:(kv + 1) * MXU_COLS]
            q_rot = jnp.concatenate([rope(sq[:, j * LANES:(j + 1) * LANES]) for j in range(MXU_COLS // LANES)],
                                    axis=1) * (SWA_HEAD_DIM ** -0.5 * LOG2E)
            for n in range(n_blocks):
                q_t = jnp.transpose(q_rot[n * SWA_BLOCK:(n + 1) * SWA_BLOCK, :])
                ops.qt[n * SWA_KV_HEADS + kv] = jnp.concatenate(
                    [q_t[j * SWA_HEAD_DIM:(j + 1) * SWA_HEAD_DIM, :] for j in range(SWA_GROUP)],
                    axis=1).astype(BF16)
            yield

        for c0 in range(0, GLA_WIDTH, PROJ_COLS):
            ops.gg[:, c0:c0 + PROJ_COLS] = _silu(proj(hb, W_GZ + c0, PROJ_COLS)).astype(BF16)
            yield
        for c0 in range(0, SWA_WIDTH, PROJ_COLS):
            ops.sg[:, c0:c0 + PROJ_COLS] = _silu(proj(hb, W_SZ + c0, PROJ_COLS)).astype(BF16)
            yield

    def gla(ops, mixed):
        row_i = lax.broadcasted_iota(jnp.int32, (ts, ts), 0)
        col_i = lax.broadcasted_iota(jnp.int32, (ts, ts), 1)
        causal = ((row_i // GLA_CHUNK) == (col_i // GLA_CHUNK)) & (col_i <= row_i)
        pair_lane = lax.broadcasted_iota(jnp.int32, (1, LANES), 1)
        heads = []
        for hd in range(GLA_HEADS):
            pr = slice((hd // 2) * LANES, (hd // 2 + 1) * LANES)
            v_h = ops.gv[:, hd * GLA_DV:(hd + 1) * GLA_DV]
            k_h = jnp.where((pair_lane // GLA_DK) == (hd % 2), ops.kd[:, pr], jnp.zeros((), BF16))
            s = jnp.where(causal, _dot_nt(ops.qs[hd // 2], k_h), 0.0).astype(BF16)
            u_all = _dot_tn(ops.kb[hd], v_h)
            heads.append((v_h, s, u_all))
            if hd % 2 == 1:
                yield

        staged = []
        for hd, (v_h, s, u_all) in enumerate(heads):
            st = state_ref[hd]
            entering = []
            for c in range(n_chunks):
                entering.append(st)
                last = c * GLA_CHUNK + GLA_CHUNK - 1
                d_col = ops.dec[hd * GLA_DK:(hd + 1) * GLA_DK, last:last + 1]
                st = st * d_col + u_all[c * GLA_DK:(c + 1) * GLA_DK, :]
            state_ref[hd] = st
            s_stack = jnp.concatenate(entering, axis=0).astype(BF16)
            staged.append((s, v_h, s_stack))
        yield

        outs = []
        for hd, (s, v_h, s_stack) in enumerate(staged):
            outs.append(_dot(s, v_h) + _dot(ops.qb[hd], s_stack))
            if hd % 2 == 1:
                yield

        g_gla = g_gla_ref[...]
        for hd, o_h in enumerate(outs):
            vs = slice(hd * GLA_DV, (hd + 1) * GLA_DV)
            o_ms = jnp.mean(o_h * o_h, axis=-1, keepdims=True)
            o_n = o_h * lax.rsqrt(o_ms + RMS_EPS) * g_gla[:, vs]
            mixed.append((o_n * ops.gg[:, vs].astype(F32)).astype(BF16))

    def swa(ops, mixed, first_tile):
        gq_lanes = SWA_GROUP * SWA_BLOCK
        kj = lax.broadcasted_iota(jnp.int32, (2 * SWA_BLOCK, gq_lanes), 0)
        ql = lax.broadcasted_iota(jnp.int32, (2 * SWA_BLOCK, gq_lanes), 1)
        dist = (ql % SWA_BLOCK) + SWA_BLOCK - kj
        valid = (dist >= 0) & (dist < WINDOW)
        head_of_lane = lax.broadcasted_iota(jnp.int32, (1, gq_lanes), 1) // SWA_BLOCK

        scores = []
        for u, (n, kv) in enumerate(units):
            bs = slice(n * SWA_BLOCK, (n + 2) * SWA_BLOCK)
            scores.append(_dot(ops.kk[kv, bs, :], ops.qt[u]))
            if u % 2 == 1:
                yield

        probs = []
        for (n, kv), s_raw in zip(units, scores):
            if n == 0 and first_tile is not None:
                ok = valid & ((kj >= SWA_BLOCK) | jnp.logical_not(first_tile))
            else:
                ok = valid
            s_t = s_raw + jnp.where(ok, 0.0, NEG_BIG)
            sink = jnp.zeros((1, gq_lanes), F32)
            for j in range(SWA_GROUP):
                sink = jnp.where(head_of_lane == j, sinks_ref[kv * SWA_GROUP + j] * LOG2E, sink)
            m = jnp.maximum(jnp.max(s_t, axis=0, keepdims=True), sink)
            e = jnp.exp2(s_t - m)
            den = jnp.sum(e, axis=0, keepdims=True) + jnp.exp2(sink - m)
            probs.append((e.astype(BF16), 1.0 / den))
            yield

        outs = []
        for (n, kv), (e, inv_den) in zip(units, probs):
            bs = slice(n * SWA_BLOCK, (n + 2) * SWA_BLOCK)
            outs.append(_dot_tn(ops.vv[kv, bs, :], e) * inv_den)
        yield

        for n in range(n_blocks):
            rs = slice(n * SWA_BLOCK, (n + 1) * SWA_BLOCK)
            o_groups = []
            for kv in range(SWA_KV_HEADS):
                o_t = outs[n * SWA_KV_HEADS + kv]
                o_stack = jnp.concatenate([o_t[:, j * SWA_BLOCK:(j + 1) * SWA_BLOCK] for j in range(SWA_GROUP)],
                                          axis=0)
                o_groups.append(jnp.transpose(o_stack))
            o_swa = jnp.concatenate(o_groups, axis=1)
            mixed.append((o_swa * ops.sg[rs, :].astype(F32)).astype(BF16))

    def finish(gla_heads, swa_blocks, rows):
        cat = jnp.concatenate([jnp.concatenate(gla_heads, axis=1), jnp.concatenate(swa_blocks, axis=0)], axis=1)
        y = _dot(cat, w_out_ref[...])
        yield
        x = x_ref[0, rows, :]
        xo = x + gate * y
        ms_o = jnp.mean(xo * xo, axis=-1, keepdims=True)
        o_ref[0, rows, :] = xo * lax.rsqrt(ms_o + RMS_EPS) * g_fin_ref[...]

    def interleave(primary, filler):
        live = list(primary)
        while live:
            for g in list(live):
                if next(g, StopIteration) is StopIteration:
                    live.remove(g)
                next(filler, None)
        for _ in filler:
            pass

    rows = [slice(k * ts, (k + 1) * ts) for k in range(STEP_TILES)]
    ops = (ops0, ops1)
    mixed = [([], []) for _ in range(STEP_TILES)]

    @pl.when(i == 0)
    def _():
        state_ref[...] = jnp.zeros_like(state_ref)

    @pl.when((i == 0) & (b_idx == 0))
    def _():
        for _ in prepare(x_ref[0, rows[0], :], pos_ref[0, :, rows[0]], norm_modulation(mod_ref), ops0, None, True):
            pass

    for k in range(STEP_TILES):
        cur, nxt = ops[k % 2], ops[(k + 1) % 2]
        primary = [gla(cur, mixed[k][0]), swa(cur, mixed[k][1], (i == 0) if k == 0 else None)]
        if k > 0:
            primary.insert(0, finish(*mixed[k - 1], rows[k - 1]))
        if k + 1 < STEP_TILES:
            filler = prepare(x_ref[0, rows[k + 1], :], pos_ref[0, :, rows[k + 1]], norm_modulation(mod_ref),
                             nxt, cur, False)
        else:
            filler = prepare(xn_ref[0], posn_ref[0], norm_modulation(modn_ref), nxt, cur, last_step)
        interleave(primary, filler)
    for _ in finish(*mixed[STEP_TILES - 1], rows[STEP_TILES - 1]):
        pass


def _chunk_matrices(ts):
    r = np.arange(ts)
    same = (r[:, None] // GLA_CHUNK) == (r[None, :] // GLA_CHUNK)
    return jnp.asarray(same & (r[None, :] <= r[:, None]), BF16), jnp.asarray(same, BF16)


def _rope_inv_freq_column():
    inv = 1.0 / (ROPE_THETA ** (jnp.arange(0, SWA_HEAD_DIM, 2, dtype=F32) / SWA_HEAD_DIM))
    return inv.reshape(SWA_HEAD_DIM // 2, 1)


def kernel(x, c, positions, w_ada, b_ada, g_norm, w_in, w_decay, b_decay, g_gla_head, sinks, w_out, g_final):
    B, S, D = x.shape
    ts = SEQ_TILE
    assert D == D_MODEL and S % (STEP_TILES * ts) == 0 and ts % SWA_BLOCK == 0 and STEP_TILES % 2 == 0
    assert w_ada.shape[0] == 1, "one layer"
    n_tiles = S // ts

    mod, w_in_b, w_out_b = _setup(c.astype(F32), w_ada[0], b_ada[0][None, :], jnp.transpose(w_in[0]), w_out[0])
    mod = mod.reshape(B, 1, 3 * D)
    w_dec = jnp.pad(w_decay[0].astype(BF16), ((0, LANES - GLA_GATE_RANK), (0, 0)))
    tri, blk = _chunk_matrices(ts)
    pos3 = positions.reshape(B, 1, S)

    const2 = lambda b, i, s: (0, 0)
    full = lambda a: pl.BlockSpec(a.shape, const2, pipeline_mode=pl.Buffered(1))
    n_steps = n_tiles // STEP_TILES

    def next_row_tile(b, i):
        wraps = i == n_steps - 1
        at_end = wraps & (b == B - 1)
        row = jnp.where(wraps & jnp.logical_not(at_end), b + 1, b)
        tile = jnp.where(wraps, jnp.where(at_end, n_tiles - 1, 0), STEP_TILES * (i + 1))
        return row, tile

    operands = [
        x, x, pos3, pos3,
        mod, mod,
        g_norm[0][None, :],
        _rope_inv_freq_column(),
        tri, blk,
        w_in_b,
        w_dec, b_decay[0][None, :], g_gla_head[0][None, :],
        w_out_b, g_final[None, :],
    ]
    in_specs = [
        pl.BlockSpec((1, STEP_TILES * ts, D), lambda b, i, s: (b, i, 0)),
        pl.BlockSpec((1, ts, D), lambda b, i, s: next_row_tile(b, i) + (0,)),
        pl.BlockSpec((1, 1, STEP_TILES * ts), lambda b, i, s: (b, 0, i)),
        pl.BlockSpec((1, 1, ts), lambda b, i, s: (next_row_tile(b, i)[0], 0, next_row_tile(b, i)[1])),
        pl.BlockSpec((1, 1, 3 * D), lambda b, i, s: (b, 0, 0)),
        pl.BlockSpec((1, 1, 3 * D), lambda b, i, s: (next_row_tile(b, i)[0], 0, 0)),
    ] + [full(a) for a in operands[6:]]

    operand_scratch = [pltpu.VMEM(shape, dtype) for _, shape, dtype in _OPERAND_BUFFERS]
    grid_spec = pltpu.PrefetchScalarGridSpec(
        num_scalar_prefetch=1,
        grid=(B, n_steps),
        in_specs=in_specs,
        out_specs=pl.BlockSpec((1, STEP_TILES * ts, D), lambda b, i, s: (b, i, 0)),
        scratch_shapes=[
            pltpu.VMEM((GLA_HEADS, GLA_DK, GLA_DV), F32),
        ] + operand_scratch + operand_scratch,
    )
    return pl.pallas_call(
        _layer_body,
        grid_spec=grid_spec,
        out_shape=jax.ShapeDtypeStruct((B, S, D), x.dtype),
        compiler_params=pltpu.CompilerParams(
            dimension_semantics=("arbitrary", "arbitrary"),
            vmem_limit_bytes=VMEM_LIMIT_BYTES,
        ),
        name="hymba_layer",
    )(sinks[0].astype(F32), *operands)
```
